```python
import jax, jax.numpy as jnp
from jax import lax
import numpy as np

D_MODEL = 1024
BATCH = 2
SEQ = 8192
DEPTH = 2

PLE_DIM = 256
N_MIXERS = 2
N_FOX_LAYERS = (DEPTH + 1) // 2
N_DIL_LAYERS = DEPTH // 2

FOX_HEADS = 16
FOX_HEAD_DIM = D_MODEL // FOX_HEADS
FOX_WIDTH = FOX_HEADS * FOX_HEAD_DIM
FOX_IN = 4 * FOX_WIDTH + FOX_HEADS
QUERY_BLOCK = 128
FORGET_BIAS_CENTER = 2.0

DIL_PATTERN = ((128, 1), (512, 4), (2048, 16))
DIL_GROUPS = len(DIL_PATTERN)
DIL_HEADS_PER_GROUP = 8
DIL_HEAD_DIM = D_MODEL // DIL_HEADS_PER_GROUP
DIL_HEADS = DIL_GROUPS * DIL_HEADS_PER_GROUP
DIL_QKV = DIL_HEADS * DIL_HEAD_DIM
DIL_WIDTH = DIL_HEADS_PER_GROUP * DIL_HEAD_DIM
DIL_IN = 3 * DIL_QKV + DIL_WIDTH
ALIBI_MAX_EXP = 8.0

RMS_EPS = 1e-6

kernel_name = "fox_dilated_hybrid_trunk"


def rms_norm(x, g):
    xf = x.astype(jnp.float32)
    y = xf * lax.rsqrt(jnp.mean(xf * xf, axis=-1, keepdims=True) + RMS_EPS)
    return (y * g.astype(jnp.float32)).astype(x.dtype)


def alibi_slopes(n):
    return 2.0 ** (-ALIBI_MAX_EXP * jnp.arange(1, n + 1, dtype=jnp.float32) / n)


def fox_mixer(h, w_in, b_f, w_out):
    B, S, _ = h.shape
    proj = h @ w_in
    q = proj[..., :FOX_WIDTH].reshape(B, S, FOX_HEADS, FOX_HEAD_DIM)
    k = proj[..., FOX_WIDTH:2 * FOX_WIDTH].reshape(B, S, FOX_HEADS, FOX_HEAD_DIM)
    v = proj[..., 2 * FOX_WIDTH:3 * FOX_WIDTH].reshape(B, S, FOX_HEADS, FOX_HEAD_DIM)
    z = proj[..., 3 * FOX_WIDTH:4 * FOX_WIDTH]
    f_logit = proj[..., 4 * FOX_WIDTH:]
    log_f = jax.nn.log_sigmoid((f_logit + b_f).astype(jnp.float32))
    c = jnp.cumsum(log_f, axis=1)
    nb = S // QUERY_BLOCK
    qb = q.reshape(B, nb, QUERY_BLOCK, FOX_HEADS, FOX_HEAD_DIM).transpose(1, 0, 2, 3, 4)
    cqb = c.reshape(B, nb, QUERY_BLOCK, FOX_HEADS).transpose(1, 0, 3, 2)
    qpos = jnp.arange(S).reshape(nb, QUERY_BLOCK)
    kpos = jnp.arange(S)
    ck = c.transpose(0, 2, 1)
    scale = FOX_HEAD_DIM ** -0.5

    def block(args):
        qi, ci, pi = args
        s = jnp.einsum('bqhd,bkhd->bhqk', qi, k).astype(jnp.float32) * scale
        s = s + ci[..., None] - ck[:, :, None, :]
        s = jnp.where((kpos[None, :] <= pi[:, None])[None, None], s, -jnp.inf)
        pr = jax.nn.softmax(s, axis=-1)
        return jnp.einsum('bhqk,bkhd->bqhd', pr.astype(v.dtype), v)

    o = lax.map(block, (qb, cqb, qpos))
    o = o.transpose(1, 0, 2, 3, 4).reshape(B, S, FOX_WIDTH)
    return (o * jax.nn.silu(z)) @ w_out


def dilated_window_attention(q, k, v, slopes, window, dilation):
    B, S, Hg, hd = q.shape
    L = S // dilation
    nW = window // dilation
    nb = -(-L // nW)
    Lp = nb * nW
    Bd = B * dilation

    def to_blocks(t):
        t = t.reshape(B, L, dilation, Hg, hd).transpose(0, 2, 1, 3, 4).reshape(Bd, L, Hg, hd)
        t = jnp.pad(t, ((0, 0), (0, Lp - L), (0, 0), (0, 0)))
        return t.reshape(Bd, nb, nW, Hg, hd)

    def with_prev(t):
        prev = jnp.pad(t, ((0, 0), (1, 0), (0, 0), (0, 0), (0, 0)))[:, :-1]
        return jnp.concatenate([prev, t], axis=2)

    qb = to_blocks(q)
    kk = with_prev(to_blocks(k))
    vv = with_prev(to_blocks(v))
    i = jnp.arange(nW)[:, None]
    j = jnp.arange(2 * nW)[None, :]
    dist = nW + i - j
    key_pos = (jnp.arange(nb)[:, None, None] - 1) * nW + j[None]
    valid = (dist >= 0)[None] & (dist <= nW)[None] & (key_pos >= 0)
    bias = -slopes.astype(jnp.float32)[:, None, None] * (dist * dilation).astype(jnp.float32)
    scale = hd ** -0.5
    s = jnp.einsum('znqhd,znkhd->znhqk', qb, kk).astype(jnp.float32) * scale + bias[None, None]
    s = jnp.where(valid[None, :, None], s, -jnp.inf)
    lse = jax.nn.logsumexp(s, axis=-1, keepdims=True)
    pr = jnp.exp(s - lse)
    o = jnp.einsum('znhqk,znkhd->znqhd', pr.astype(v.dtype), vv)
    o = o.reshape(Bd, Lp, Hg, hd)[:, :L]
    o = o.reshape(B, dilation, L, Hg, hd).transpose(0, 2, 1, 3, 4).reshape(B, S, Hg, hd)
    lse = lse[..., 0].transpose(0, 1, 3, 2).reshape(Bd, Lp, Hg)[:, :L]
    lse = lse.reshape(B, dilation, L, Hg).transpose(0, 2, 1, 3).reshape(B, S, Hg)
    return o, lse


def dilated_mixer(h, w_in, w_out):
    B, S, _ = h.shape
    proj = h @ w_in
    shp = (B, S, DIL_GROUPS, DIL_HEADS_PER_GROUP, DIL_HEAD_DIM)
    q = proj[..., :DIL_QKV].reshape(shp)
    k = proj[..., DIL_QKV:2 * DIL_QKV].reshape(shp)
    v = proj[..., 2 * DIL_QKV:3 * DIL_QKV].reshape(shp)
    z = proj[..., 3 * DIL_QKV:]
    slopes = alibi_slopes(DIL_HEADS).reshape(DIL_GROUPS, DIL_HEADS_PER_GROUP)
    outs, lses = [], []
    for g, (window, dilation) in enumerate(DIL_PATTERN):
        o, l = dilated_window_attention(q[:, :, g], k[:, :, g], v[:, :, g], slopes[g], window, dilation)
        outs.append(o)
        lses.append(l)
    wts = jax.nn.softmax(jnp.stack(lses), axis=0)
    o = jnp.sum(wts[..., None] * jnp.stack(outs).astype(jnp.float32), axis=0)
    o = o.astype(h.dtype).reshape(B, S, DIL_WIDTH)
    return (o * jax.nn.silu(z)) @ w_out


def setup_inputs(seed: int = 0) -> dict:
    key = jax.random.key(seed)
    ks = jax.random.split(key, 14)
    f32 = jnp.float32
    nrm = lambda k, shape, fan_in: jax.random.normal(k, shape, f32) * fan_in ** -0.5
    return {
        "x": jax.random.normal(ks[0], (BATCH, SEQ, D_MODEL), f32),
        "p": jax.random.normal(ks[1], (DEPTH, BATCH, SEQ, PLE_DIM), f32),
        "fox_norm": 1.0 + 0.02 * jax.random.normal(ks[2], (N_FOX_LAYERS, D_MODEL), f32),
        "fox_w_in": nrm(ks[3], (N_FOX_LAYERS, D_MODEL, FOX_IN), D_MODEL),
        "fox_b_f": FORGET_BIAS_CENTER + 0.5 * jax.random.normal(ks[4], (N_FOX_LAYERS, FOX_HEADS), f32),
        "fox_w_out": nrm(ks[5], (N_FOX_LAYERS, FOX_WIDTH, D_MODEL), FOX_WIDTH),
        "dil_norm": 1.0 + 0.02 * jax.random.normal(ks[6], (N_DIL_LAYERS, D_MODEL), f32),
        "dil_w_in": nrm(ks[7], (N_DIL_LAYERS, D_MODEL, DIL_IN), D_MODEL),
        "dil_w_out": nrm(ks[8], (N_DIL_LAYERS, DIL_WIDTH, D_MODEL), DIL_WIDTH),
        "ple_w_up": nrm(ks[9], (DEPTH, PLE_DIM, D_MODEL), PLE_DIM),
        "ple_w_gate": nrm(ks[10], (DEPTH, D_MODEL, D_MODEL), D_MODEL),
        "final_norm": 1.0 + 0.02 * jax.random.normal(ks[11], (D_MODEL,), f32),
    }


def reference(x, p, fox_norm, fox_w_in, fox_b_f, fox_w_out, dil_norm, dil_w_in, dil_w_out,
              ple_w_up, ple_w_gate, final_norm):
    h = x
    for i in range(DEPTH):
        j = i // N_MIXERS
        if i % N_MIXERS == 0:
            h = h + fox_mixer(rms_norm(h, fox_norm[j]), fox_w_in[j], fox_b_f[j], fox_w_out[j])
        else:
            h = h + dilated_mixer(rms_norm(h, dil_norm[j]), dil_w_in[j], dil_w_out[j])
        h = h + (p[i] @ ple_w_up[i]) * jax.nn.sigmoid(h @ ple_w_gate[i])
    return rms_norm(h, final_norm)
```

```python
import functools

import numpy as np
import jax
import jax.numpy as jnp
from jax import lax
from jax.experimental import pallas as pl
from jax.experimental.pallas import tpu as pltpu

F32 = jnp.float32
BF16 = jnp.bfloat16

RMS_EPS = 1e-6
FOX_HEADS = 16
FOX_HEAD_DIM = 64
FOX_PAIRS = FOX_HEADS // 2
DIL_PATTERN = ((128, 1), (512, 4), (2048, 16))
DIL_HEADS_PER_GROUP = 8
DIL_HEAD_DIM = 128
DIL_WINDOW_STEPS = 128
ALIBI_MAX_EXP = 8.0
MASK_VALUE = -1e30

LANES = 128
AUG_SLOTS_PER_HEAD = 16
VMEM_LIMIT_BYTES = 56 * 1024 * 1024

NT_DIMS = (((1,), (1,)), ((), ()))
TN_DIMS = (((0,), (0,)), ((), ()))


def _cparams(n_axes):
    return pltpu.CompilerParams(
        dimension_semantics=("arbitrary",) * n_axes,
        vmem_limit_bytes=VMEM_LIMIT_BYTES,
    )


def _rms_norm(x, g):
    ms = jnp.mean(x * x, axis=-1, keepdims=True)
    return x * lax.rsqrt(ms + RMS_EPS) * g


def _log_sigmoid(x):
    return jnp.minimum(x, 0.0) - jnp.log1p(jnp.exp(-jnp.abs(x)))


def _split3(x):
    hi = x.astype(BF16).astype(F32)
    r1 = x - hi
    mid = r1.astype(BF16).astype(F32)
    lo = (r1 - mid).astype(BF16).astype(F32)
    return hi, mid, lo


def _dot(a, b):
    return jnp.dot(a, b, preferred_element_type=F32)


def _fox_in_kernel(x_ref, g_ref, wnat_ref, wt_ref, wft_ref, bnat_ref, bt_ref, tril_ref,
                   triu_ref, pk_ref, pqt_ref,
                   kaug_ref, qt_ref, vt_ref, zt_ref, augq_ref,
                   carry_ref, carryt_ref, *, tm, ch):
    i = pl.program_id(1)

    @pl.when(i == 0)
    def _():
        carry_ref[...] = jnp.zeros_like(carry_ref)
        carryt_ref[...] = jnp.zeros_like(carryt_ref)

    hn = _rms_norm(x_ref[0], g_ref[...]).astype(BF16)
    width = FOX_HEADS * FOX_HEAD_DIM

    nat = _dot(hn, wnat_ref[...])
    k = nat[:, :width]
    logf = _log_sigmoid(nat[:, width:] + bnat_ref[...])
    hi, mid, lo = _split3(logf)
    tril = tril_ref[...]
    c = (_dot(tril, hi.astype(BF16)) + _dot(tril, mid.astype(BF16))
         + _dot(tril, lo.astype(BF16)) + carry_ref[...])
    carry_ref[...] = c[tm - 1:tm, :]
    nhi, nmid, nlo = _split3(-c)
    lane = lax.broadcasted_iota(jnp.int32, (tm, LANES), 1)
    slots = jnp.where(lane < 16, nhi,
                      jnp.where(lane < 32, pltpu.roll(nmid, 16, 1),
                                jnp.where(lane < 48, pltpu.roll(nlo, 32, 1),
                                          jnp.where(lane == 48, 1.0, 0.0))))
    kaug = _dot(slots.astype(BF16), pk_ref[...])
    for p in range(FOX_PAIRS):
        kaug_ref[0, :, 2 * LANES * p:2 * LANES * p + LANES] = (
            k[:, LANES * p:LANES * (p + 1)].astype(BF16))
        kaug_ref[0, :, 2 * LANES * p + LANES:2 * LANES * (p + 1)] = (
            kaug[:, LANES * p:LANES * (p + 1)].astype(BF16))

    rows = 512
    for c0 in range(0, 3 * width, rows):
        res = lax.dot_general(wt_ref[c0:c0 + rows, :], hn, NT_DIMS,
                              preferred_element_type=F32).astype(BF16)
        which, off = divmod(c0, width)
        if which == 0:
            qt_ref[0, off:off + rows, :] = res
        elif which == 1:
            for j in range(tm // ch):
                vt_ref[0, j, off:off + rows, :] = res[:, j * ch:(j + 1) * ch]
        else:
            zt_ref[0, off:off + rows, :] = res

    ft = lax.dot_general(wft_ref[...], hn, NT_DIMS, preferred_element_type=F32)
    logft = _log_sigmoid(ft + bt_ref[...])
    hi, mid, lo = _split3(logft)
    stack = jnp.concatenate([hi, mid, lo], axis=0).astype(BF16)
    cs = _dot(stack, triu_ref[...])
    ct = cs[0:16] + cs[16:32] + cs[32:48] + carryt_ref[:, 0:1]
    carryt_ref[...] = jnp.broadcast_to(ct[:, tm - 1:tm], carryt_ref.shape)
    hi, mid, lo = _split3(ct)
    ones = jnp.where(lax.broadcasted_iota(jnp.int32, (16, tm), 0) == 0, 1.0, 0.0)
    slots_t = jnp.concatenate([hi, mid, lo, ones], axis=0).astype(BF16)
    augq_ref[0] = _dot(pqt_ref[...], slots_t).astype(BF16)


def _placement_matrices():
    pk = np.zeros((LANES, FOX_PAIRS * LANES), np.float32)
    pqt = np.zeros((FOX_HEADS * AUG_SLOTS_PER_HEAD, 64), np.float32)
    for h in range(FOX_HEADS):
        p, odd = divmod(h, 2)
        base = 6 * odd
        for piece in range(3):
            pk[48, LANES * p + base + piece] = 1.0
            pk[16 * piece + h, LANES * p + base + 3 + piece] = 1.0
            pqt[AUG_SLOTS_PER_HEAD * h + base + piece, 16 * piece + h] = 1.0
            pqt[AUG_SLOTS_PER_HEAD * h + base + 3 + piece, 48] = 1.0
    return jnp.asarray(pk, BF16), jnp.asarray(pqt, BF16)


def _fox_in(x, norm_g, w_in, b_f, *, tm, ch):
    b, s, d = x.shape
    width = FOX_HEADS * FOX_HEAD_DIM
    scale = FOX_HEAD_DIM ** -0.5
    wq = w_in[:, :width] * scale
    wk = w_in[:, width:2 * width]
    wv = w_in[:, 2 * width:3 * width]
    wz = w_in[:, 3 * width:4 * width]
    wf = w_in[:, 4 * width:]
    wnat = jnp.concatenate([wk, jnp.pad(wf, ((0, 0), (0, LANES - FOX_HEADS)))], axis=1).astype(BF16)
    wt = jnp.concatenate([wq, wv, wz], axis=1).T.astype(BF16)
    wft = wf.T.astype(BF16)
    bnat = jnp.pad(b_f, (0, LANES - FOX_HEADS)).reshape(1, LANES)
    bt = b_f.reshape(FOX_HEADS, 1)
    r = np.arange(tm)
    tril = jnp.asarray(r[None, :] <= r[:, None], BF16)
    triu = jnp.asarray(r[:, None] <= r[None, :], BF16)
    pk, pqt = _placement_matrices()

    const = lambda shape: pl.BlockSpec(shape, lambda bi, i: (0,) * len(shape))
    return pl.pallas_call(
        functools.partial(_fox_in_kernel, tm=tm, ch=ch),
        grid=(b, s // tm),
        in_specs=[
            pl.BlockSpec((1, tm, d), lambda bi, i: (bi, i, 0)),
            const((1, d)),
            const(wnat.shape), const(wt.shape), const(wft.shape),
            const((1, LANES)), const((FOX_HEADS, 1)),
            const((tm, tm)), const((tm, tm)),
            const(pk.shape), const(pqt.shape),
        ],
        out_specs=[
            pl.BlockSpec((1, tm, 2 * width), lambda bi, i: (bi, i, 0)),
            pl.BlockSpec((1, width, tm), lambda bi, i: (bi, 0, i)),
            pl.BlockSpec((1, tm // ch, width, ch), lambda bi, i: (bi, i, 0, 0)),
            pl.BlockSpec((1, width, tm), lambda bi, i: (bi, 0, i)),
            pl.BlockSpec((1, FOX_HEADS * AUG_SLOTS_PER_HEAD, tm), lambda bi, i: (bi, 0, i)),
        ],
        out_shape=[
            jax.ShapeDtypeStruct((b, s, 2 * width), BF16),
            jax.ShapeDtypeStruct((b, width, s), BF16),
            jax.ShapeDtypeStruct((b, s // ch, width, ch), BF16),
            jax.ShapeDtypeStruct((b, width, s), BF16),
            jax.ShapeDtypeStruct((b, FOX_HEADS * AUG_SLOTS_PER_HEAD, s), BF16),
        ],
        scratch_shapes=[pltpu.VMEM((1, LANES), F32), pltpu.VMEM((FOX_HEADS, LANES), F32)],
        compiler_params=_cparams(2),
        name="fox_in",
    )(x, norm_g.reshape(1, d), wnat, wt, wft, bnat, bt, tril, triu, pk, pqt)


def _fox_attn_kernel(kaug_ref, vt_ref, qt_ref, augq_ref, ot_ref, *, tq):
    qi = pl.program_id(2)
    hd = FOX_HEAD_DIM
    qt = qt_ref[0]
    aug = augq_ref[0]
    zeros_hd = jnp.zeros((hd, tq), BF16)
    zeros_tail = jnp.zeros((LANES - AUG_SLOTS_PER_HEAD, tq), BF16)
    w_heads = (
        jnp.concatenate([qt[0:hd], zeros_hd, aug[0:16], zeros_tail], axis=0),
        jnp.concatenate([zeros_hd, qt[hd:2 * hd], aug[16:32], zeros_tail], axis=0),
    )

    def block(kj, carry, masked):
        ks = pl.multiple_of(kj * tq, tq)
        kb = kaug_ref[0, pl.ds(ks, tq), :]
        vb = vt_ref[0, kj]
        out = []
        for h in range(2):
            m_old, l_old, acc = carry[h]
            s = _dot(kb, w_heads[h])
            if masked:
                key = lax.broadcasted_iota(jnp.int32, (tq, tq), 0)
                qry = lax.broadcasted_iota(jnp.int32, (tq, tq), 1)
                s = jnp.where(key <= qry, s, MASK_VALUE)
            m_new = jnp.maximum(m_old, jnp.max(s, axis=0, keepdims=True))
            alpha = jnp.exp(m_old - m_new)
            p = jnp.exp(s - m_new)
            l_new = alpha * l_old + jnp.sum(p, axis=0, keepdims=True)
            pv = _dot(vb[hd * h:hd * (h + 1)], p.astype(BF16))
            out.append((m_new, l_new, acc * alpha + pv))
        return tuple(out)

    init = tuple((jnp.full((1, tq), MASK_VALUE, F32), jnp.zeros((1, tq), F32),
                  jnp.zeros((hd, tq), F32)) for _ in range(2))
    carry = lax.fori_loop(0, qi, lambda kj, c: block(kj, c, False), init)
    carry = block(qi, carry, True)
    ot_ref[0] = jnp.concatenate([carry[0][2] / carry[0][1], carry[1][2] / carry[1][1]], axis=0)


def _fox_attn(kaug, qt, vt, augq, *, tq):
    b, s, _ = kaug.shape
    width = FOX_HEADS * FOX_HEAD_DIM
    nk = s // tq
    return pl.pallas_call(
        functools.partial(_fox_attn_kernel, tq=tq),
        grid=(b, FOX_PAIRS, s // tq),
        in_specs=[
            pl.BlockSpec((1, s, 2 * LANES), lambda bi, p, qi: (bi, 0, p)),
            pl.BlockSpec((1, nk, LANES, tq), lambda bi, p, qi: (bi, 0, p, 0)),
            pl.BlockSpec((1, LANES, tq), lambda bi, p, qi: (bi, p, qi)),
            pl.BlockSpec((1, 2 * AUG_SLOTS_PER_HEAD, tq), lambda bi, p, qi: (bi, p, qi)),
        ],
        out_specs=pl.BlockSpec((1, LANES, tq), lambda bi, p, qi: (bi, p, qi)),
        out_shape=jax.ShapeDtypeStruct((b, width, s), F32),
        compiler_params=_cparams(3),
        name="fox_attn",
    )(kaug, vt, qt, augq)


def _ple(h1, p, wup_ref, wgate_ref):
    gate = jax.nn.sigmoid(_dot(h1.astype(BF16), wgate_ref[...]))
    up = _dot(p.astype(BF16), wup_ref[...])
    return h1 + up * gate


def _fox_out_kernel(ot_ref, zt_ref, x_ref, p_ref, wout_ref, wup_ref, wgate_ref, h_ref):
    z = zt_ref[0].astype(F32)
    gt = (ot_ref[0] * (z * jax.nn.sigmoid(z))).astype(BF16)
    y = lax.dot_general(gt, wout_ref[...], TN_DIMS, preferred_element_type=F32)
    h_ref[0] = _ple(x_ref[0] + y, p_ref[0], wup_ref, wgate_ref)


def _fox_out(ot, zt, x, p, w_out, w_up, w_gate, *, tm):
    b, s, d = x.shape
    width = ot.shape[1]
    pd = p.shape[-1]
    const = lambda shape: pl.BlockSpec(shape, lambda bi, i: (0,) * len(shape))
    return pl.pallas_call(
        _fox_out_kernel,
        grid=(b, s // tm),
        in_specs=[
            pl.BlockSpec((1, width, tm), lambda bi, i: (bi, 0, i)),
            pl.BlockSpec((1, width, tm), lambda bi, i: (bi, 0, i)),
            pl.BlockSpec((1, tm, d), lambda bi, i: (bi, i, 0)),
            pl.BlockSpec((1, tm, pd), lambda bi, i: (bi, i, 0)),
            const((width, d)), const((pd, d)), const((d, d)),
        ],
        out_specs=pl.BlockSpec((1, tm, d), lambda bi, i: (bi, i, 0)),
        out_shape=jax.ShapeDtypeStruct((b, s, d), F32),
        compiler_params=_cparams(2),
        name="fox_out",
    )(ot, zt, x, p, w_out.astype(BF16), w_up.astype(BF16), w_gate.astype(BF16))


def _dil_in_kernel(h_ref, g_ref, w_ref, o_ref, hn_ref):
    @pl.when(pl.program_id(2) == 0)
    def _():
        hn_ref[...] = _rms_norm(h_ref[0], g_ref[...]).astype(BF16)

    o_ref[0] = _dot(hn_ref[...], w_ref[...]).astype(BF16)


def _dil_in(h, norm_g, w_in, *, tm, tn):
    b, s, d = h.shape
    n = w_in.shape[1]
    return pl.pallas_call(
        _dil_in_kernel,
        grid=(b, s // tm, n // tn),
        in_specs=[
            pl.BlockSpec((1, tm, d), lambda bi, i, j: (bi, i, 0)),
            pl.BlockSpec((1, d), lambda bi, i, j: (0, 0)),
            pl.BlockSpec((d, tn), lambda bi, i, j: (0, j)),
        ],
        out_specs=pl.BlockSpec((1, tm, tn), lambda bi, i, j: (bi, i, j)),
        out_shape=jax.ShapeDtypeStruct((b, s, n), BF16),
        scratch_shapes=[pltpu.VMEM((tm, d), BF16)],
        compiler_params=_cparams(3),
        name="dil_in",
    )(h, norm_g.reshape(1, d), w_in.astype(BF16))


def _dil_attn_kernel(q_ref, kc_ref, vc_ref, kp_ref, vp_ref, o_ref, st_ref, *,
                     dilation, kb, neg_slopes):
    n = pl.program_id(2)
    nw = DIL_WINDOW_STEPS
    hd = DIL_HEAD_DIM
    scale = hd ** -0.5
    row = lax.broadcasted_iota(jnp.int32, (nw, 2 * nw), 0)
    col = lax.broadcasted_iota(jnp.int32, (nw, 2 * nw), 1)
    dist = nw + row - col
    band = (dist >= 0) & (dist <= nw)
    dist_f = (dist * dilation).astype(F32)
    lane = lax.broadcasted_iota(jnp.int32, (nw, LANES), 1)

    for jb in range(kb):
        if jb == 0:
            kk = jnp.concatenate([kp_ref[0], kc_ref[0, 0:nw]], axis=0)
            vv = jnp.concatenate([vp_ref[0], vc_ref[0, 0:nw]], axis=0)
            valid = band & (col >= jnp.where(n > 0, 0, nw))
        else:
            kk = kc_ref[0, nw * (jb - 1):nw * (jb + 1)]
            vv = vc_ref[0, nw * (jb - 1):nw * (jb + 1)]
            valid = band
        stats = jnp.zeros((nw, LANES), F32)
        for h in range(DIL_HEADS_PER_GROUP):
            hs = slice(hd * h, hd * (h + 1))
            q = q_ref[0, nw * jb:nw * (jb + 1), hs]
            s = lax.dot_general(q, kk[:, hs], NT_DIMS, preferred_element_type=F32)
            s = s * scale + neg_slopes[h] * dist_f
            s = jnp.where(valid, s, MASK_VALUE)
            m = jnp.max(s, axis=1, keepdims=True)
            p = jnp.exp(s - m)
            l = jnp.sum(p, axis=1, keepdims=True)
            o = _dot(p.astype(BF16), vv[:, hs])
            o_ref[0, nw * jb:nw * (jb + 1), hs] = o / l
            stats = jnp.where(lane == h, m + jnp.log(l), stats)
        st_ref[0, nw * jb:nw * (jb + 1), :] = stats


def _alibi_neg_slopes(group):
    n = len(DIL_PATTERN) * DIL_HEADS_PER_GROUP
    k = np.arange(1, n + 1, dtype=np.float32)
    slopes = np.float32(2.0) ** (np.float32(-ALIBI_MAX_EXP) * k / np.float32(n))
    lo = group * DIL_HEADS_PER_GROUP
    return tuple(float(-v) for v in slopes[lo:lo + DIL_HEADS_PER_GROUP])


def _dil_attn(proj, group, *, kb):
    b, s, ncol = proj.shape
    _, dilation = DIL_PATTERN[group]
    nw = DIL_WINDOW_STEPS
    gw = DIL_HEADS_PER_GROUP * DIL_HEAD_DIM
    ncb = ncol // gw
    n_groups = len(DIL_PATTERN)
    length = s // dilation
    rows = nw * kb
    view = proj.reshape(b, length, dilation * ncol)
    qc, kc, vc = group, n_groups + group, 2 * n_groups + group

    def cur(cb):
        return pl.BlockSpec((1, rows, gw), lambda bi, r, n: (bi, n, r * ncb + cb))

    def prev(cb):
        return pl.BlockSpec((1, nw, gw),
                            lambda bi, r, n: (bi, jnp.maximum(n * kb - 1, 0), r * ncb + cb))

    o, st = pl.pallas_call(
        functools.partial(_dil_attn_kernel, dilation=dilation, kb=kb,
                          neg_slopes=_alibi_neg_slopes(group)),
        grid=(b, dilation, length // rows),
        in_specs=[cur(qc), cur(kc), cur(vc), prev(kc), prev(vc)],
        out_specs=[
            pl.BlockSpec((1, rows, gw), lambda bi, r, n: (bi, n, r)),
            pl.BlockSpec((1, rows, LANES), lambda bi, r, n: (bi, n, r)),
        ],
        out_shape=[
            jax.ShapeDtypeStruct((b, length, dilation * gw), F32),
            jax.ShapeDtypeStruct((b, length, dilation * LANES), F32),
        ],
        compiler_params=_cparams(3),
        name=f"dil_attn_g{group}",
    )(view, view, view, view, view)
    return o.reshape(b, s, gw), st.reshape(b, s, LANES)


def _dil_out_kernel(o0_ref, o1_ref, o2_ref, s0_ref, s1_ref, s2_ref, z_ref, h_ref, p_ref,
                    e_ref, wout_ref, wup_ref, wgate_ref, g_ref, out_ref):
    lse = (s0_ref[0], s1_ref[0], s2_ref[0])
    m = jnp.maximum(jnp.maximum(lse[0], lse[1]), lse[2])
    e = [jnp.exp(v - m) for v in lse]
    inv = 1.0 / (e[0] + e[1] + e[2])
    o = None
    for ev, o_ref in zip(e, (o0_ref, o1_ref, o2_ref)):
        w = ev * inv
        w_hi = w.astype(BF16)
        w_lo = (w - w_hi.astype(F32)).astype(BF16)
        w_full = _dot(w_hi, e_ref[...]) + _dot(w_lo, e_ref[...])
        term = w_full * o_ref[0]
        o = term if o is None else o + term
    z = z_ref[0].astype(F32)
    g = (o * (z * jax.nn.sigmoid(z))).astype(BF16)
    h1 = h_ref[0] + _dot(g, wout_ref[...])
    out_ref[0] = _rms_norm(_ple(h1, p_ref[0], wup_ref, wgate_ref), g_ref[...])


def _dil_out(outs, stats, proj, h, p, w_out, w_up, w_gate, final_g, *, tm):
    b, s, d = h.shape
    gw = outs[0].shape[-1]
    pd = p.shape[-1]
    z_block = proj.shape[-1] // gw - 1
    expand = np.zeros((LANES, gw), np.float32)
    for hh in range(DIL_HEADS_PER_GROUP):
        expand[hh, DIL_HEAD_DIM * hh:DIL_HEAD_DIM * (hh + 1)] = 1.0
    const = lambda shape: pl.BlockSpec(shape, lambda bi, i: (0,) * len(shape))
    row = lambda w: pl.BlockSpec((1, tm, w), lambda bi, i: (bi, i, 0))
    return pl.pallas_call(
        _dil_out_kernel,
        grid=(b, s // tm),
        in_specs=[row(gw)] * 3 + [row(LANES)] * 3 + [
            pl.BlockSpec((1, tm, gw), lambda bi, i: (bi, i, z_block)),
            row(d), row(pd),
            const((LANES, gw)), const((gw, d)), const((pd, d)), const((d, d)), const((1, d)),
        ],
        out_specs=row(d),
        out_shape=jax.ShapeDtypeStruct((b, s, d), F32),
        compiler_params=_cparams(2),
        name="dil_out",
    )(*outs, *stats, proj, h, p, jnp.asarray(expand, BF16), w_out.astype(BF16),
      w_up.astype(BF16), w_gate.astype(BF16), final_g.reshape(1, d))


def kernel(x, p, fox_norm, fox_w_in, fox_b_f, fox_w_out, dil_norm, dil_w_in, dil_w_out,
           ple_w_up, ple_w_gate, final_norm):
    s = x.shape[1]
    tm = min(512, s)
    tq = min(256, s)
    kaug, qt, vt, zt, augq = _fox_in(x, fox_norm[0], fox_w_in[0], fox_b_f[0], tm=tm, ch=tq)
    ot = _fox_attn(kaug, qt, vt, augq, tq=tq)
    h = _fox_out(ot, zt, x, p[0], fox_w_out[0], ple_w_up[0], ple_w_gate[0], tm=tm)

    proj = _dil_in(h, dil_norm[0], dil_w_in[0], tm=min(1024, s), tn=2048)
    outs, stats = [], []
    for group, (_, dilation) in enumerate(DIL_PATTERN):
        kb = min(4, s // dilation // DIL_WINDOW_STEPS)
        o, st = _dil_attn(proj, group, kb=kb)
        outs.append(o)
        stats.append(st)
    return _dil_out(outs, stats, proj, h, p[1], dil_w_out[0], ple_w_up[1], ple_w_gate[1],
                    final_norm, tm=tm)
```

```python
import functools

import numpy as np
import jax
import jax.numpy as jnp
from jax import lax
from jax.experimental import pallas as pl
from jax.experimental.pallas import tpu as pltpu

F32 = jnp.float32
BF16 = jnp.bfloat16

RMS_EPS = 1e-6
FOX_HEADS = 16
FOX_HEAD_DIM = 64
FOX_PAIRS = FOX_HEADS // 2
DIL_PATTERN = ((128, 1), (512, 4), (2048, 16))
DIL_HEADS_PER_GROUP = 8
DIL_HEAD_DIM = 128
DIL_WINDOW_STEPS = 128
ALIBI_MAX_EXP = 8.0
MASK_VALUE = -1e30
LOG2E = 1.4426950408889634

LANES = 128
AUG_SLOTS_PER_HEAD = 16
VMEM_LIMIT_BYTES = 56 * 1024 * 1024

NT_DIMS = (((1,), (1,)), ((), ()))
TN_DIMS = (((0,), (0,)), ((), ()))


def _cparams(n_axes):
    return pltpu.CompilerParams(
        dimension_semantics=("arbitrary",) * n_axes,
        vmem_limit_bytes=VMEM_LIMIT_BYTES,
    )


def _rms_norm(x, g):
    ms = jnp.mean(x * x, axis=-1, keepdims=True)
    return x * lax.rsqrt(ms + RMS_EPS) * g


def _log_sigmoid(x):
    return jnp.minimum(x, 0.0) - jnp.log1p(jnp.exp(-jnp.abs(x)))


def _split3(x):
    hi = x.astype(BF16).astype(F32)
    r1 = x - hi
    mid = r1.astype(BF16).astype(F32)
    lo = (r1 - mid).astype(BF16).astype(F32)
    return hi, mid, lo


def _dot(a, b):
    return jnp.dot(a, b, preferred_element_type=F32)


def _fox_in_kernel(x_ref, g_ref, wnat_ref, wt_ref, wft_ref, bnat_ref, bt_ref, tril_ref,
                   triu_ref, pk_ref, pqt_ref,
                   kaug_ref, qt_ref, vt_ref, zt_ref, augq_ref,
                   carry_ref, carryt_ref, *, tm, ch):
    i = pl.program_id(1)

    @pl.when(i == 0)
    def _():
        carry_ref[...] = jnp.zeros_like(carry_ref)
        carryt_ref[...] = jnp.zeros_like(carryt_ref)

    hn = _rms_norm(x_ref[0], g_ref[...]).astype(BF16)
    width = FOX_HEADS * FOX_HEAD_DIM

    nat = _dot(hn, wnat_ref[...])
    k = nat[:, :width]
    logf = _log_sigmoid(nat[:, width:] + bnat_ref[...])
    hi, mid, lo = _split3(logf)
    tril = tril_ref[...]
    c = (_dot(tril, hi.astype(BF16)) + _dot(tril, mid.astype(BF16))
         + _dot(tril, lo.astype(BF16)) + carry_ref[...])
    carry_ref[...] = c[tm - 1:tm, :]
    nhi, nmid, nlo = _split3(c * -LOG2E)
    lane = lax.broadcasted_iota(jnp.int32, (tm, LANES), 1)
    slots = jnp.where(lane < 16, nhi,
                      jnp.where(lane < 32, pltpu.roll(nmid, 16, 1),
                                jnp.where(lane < 48, pltpu.roll(nlo, 32, 1),
                                          jnp.where(lane == 48, 1.0, 0.0))))
    kaug = _dot(slots.astype(BF16), pk_ref[...])
    for p in range(FOX_PAIRS):
        kaug_ref[0, :, 2 * LANES * p:2 * LANES * p + LANES] = (
            k[:, LANES * p:LANES * (p + 1)].astype(BF16))
        kaug_ref[0, :, 2 * LANES * p + LANES:2 * LANES * (p + 1)] = (
            kaug[:, LANES * p:LANES * (p + 1)].astype(BF16))

    rows = 512
    for c0 in range(0, 3 * width, rows):
        res = lax.dot_general(wt_ref[c0:c0 + rows, :], hn, NT_DIMS,
                              preferred_element_type=F32).astype(BF16)
        which, off = divmod(c0, width)
        if which == 0:
            qt_ref[0, off:off + rows, :] = res
        elif which == 1:
            for j in range(tm // ch):
                vt_ref[0, j, off:off + rows, :] = res[:, j * ch:(j + 1) * ch]
        else:
            zt_ref[0, off:off + rows, :] = res

    ft = lax.dot_general(wft_ref[...], hn, NT_DIMS, preferred_element_type=F32)
    logft = _log_sigmoid(ft + bt_ref[...])
    hi, mid, lo = _split3(logft)
    stack = jnp.concatenate([hi, mid, lo], axis=0).astype(BF16)
    cs = _dot(stack, triu_ref[...])
    ct = cs[0:16] + cs[16:32] + cs[32:48] + carryt_ref[:, 0:1]
    carryt_ref[...] = jnp.broadcast_to(ct[:, tm - 1:tm], carryt_ref.shape)
    hi, mid, lo = _split3(ct * LOG2E)
    ones = jnp.where(lax.broadcasted_iota(jnp.int32, (16, tm), 0) == 0, 1.0, 0.0)
    slots_t = jnp.concatenate([hi, mid, lo, ones], axis=0).astype(BF16)
    augq_ref[0] = _dot(pqt_ref[...], slots_t).astype(BF16)


def _placement_matrices():
    pk = np.zeros((LANES, FOX_PAIRS * LANES), np.float32)
    pqt = np.zeros((FOX_HEADS * AUG_SLOTS_PER_HEAD, 64), np.float32)
    for h in range(FOX_HEADS):
        p, odd = divmod(h, 2)
        base = 6 * odd
        for piece in range(3):
            pk[48, LANES * p + base + piece] = 1.0
            pk[16 * piece + h, LANES * p + base + 3 + piece] = 1.0
            pqt[AUG_SLOTS_PER_HEAD * h + base + piece, 16 * piece + h] = 1.0
            pqt[AUG_SLOTS_PER_HEAD * h + base + 3 + piece, 48] = 1.0
    return jnp.asarray(pk, BF16), jnp.asarray(pqt, BF16)


def _fox_in(x, norm_g, w_in, b_f, *, tm, ch):
    b, s, d = x.shape
    width = FOX_HEADS * FOX_HEAD_DIM
    wq = w_in[:, :width] * (FOX_HEAD_DIM ** -0.5 * LOG2E)
    wk = w_in[:, width:2 * width]
    wv = w_in[:, 2 * width:3 * width]
    wz = w_in[:, 3 * width:4 * width]
    wf = w_in[:, 4 * width:]
    wnat = jnp.concatenate([wk, jnp.pad(wf, ((0, 0), (0, LANES - FOX_HEADS)))], axis=1).astype(BF16)
    wt = jnp.concatenate([wq, wv, wz], axis=1).T.astype(BF16)
    wft = wf.T.astype(BF16)
    bnat = jnp.pad(b_f, (0, LANES - FOX_HEADS)).reshape(1, LANES)
    bt = b_f.reshape(FOX_HEADS, 1)
    r = np.arange(tm)
    tril = jnp.asarray(r[None, :] <= r[:, None], BF16)
    triu = jnp.asarray(r[:, None] <= r[None, :], BF16)
    pk, pqt = _placement_matrices()

    const = lambda shape: pl.BlockSpec(shape, lambda bi, i: (0,) * len(shape))
    return pl.pallas_call(
        functools.partial(_fox_in_kernel, tm=tm, ch=ch),
        grid=(b, s // tm),
        in_specs=[
            pl.BlockSpec((1, tm, d), lambda bi, i: (bi, i, 0)),
            const((1, d)),
            const(wnat.shape), const(wt.shape), const(wft.shape),
            const((1, LANES)), const((FOX_HEADS, 1)),
            const((tm, tm)), const((tm, tm)),
            const(pk.shape), const(pqt.shape),
        ],
        out_specs=[
            pl.BlockSpec((1, tm, 2 * width), lambda bi, i: (bi, i, 0)),
            pl.BlockSpec((1, width, tm), lambda bi, i: (bi, 0, i)),
            pl.BlockSpec((1, tm // ch, width, ch), lambda bi, i: (bi, i, 0, 0)),
            pl.BlockSpec((1, width, tm), lambda bi, i: (bi, 0, i)),
            pl.BlockSpec((1, FOX_HEADS * AUG_SLOTS_PER_HEAD, tm), lambda bi, i: (bi, 0, i)),
        ],
        out_shape=[
            jax.ShapeDtypeStruct((b, s, 2 * width), BF16),
            jax.ShapeDtypeStruct((b, width, s), BF16),
            jax.ShapeDtypeStruct((b, s // ch, width, ch), BF16),
            jax.ShapeDtypeStruct((b, width, s), BF16),
            jax.ShapeDtypeStruct((b, FOX_HEADS * AUG_SLOTS_PER_HEAD, s), BF16),
        ],
        scratch_shapes=[pltpu.VMEM((1, LANES), F32), pltpu.VMEM((FOX_HEADS, LANES), F32)],
        compiler_params=_cparams(2),
        name="fox_in",
    )(x, norm_g.reshape(1, d), wnat, wt, wft, bnat, bt, tril, triu, pk, pqt)


def _fox_attn_kernel(kaug_ref, vt_ref, qt_ref, augq_ref, ot_ref, s_scr, p_scr, *, tq, tk):
    qi = pl.program_id(2)
    hd = FOX_HEAD_DIM
    qt = qt_ref[0]
    aug = augq_ref[0]
    zeros_hd = jnp.zeros((hd, tq), BF16)
    zeros_tail = jnp.zeros((LANES - AUG_SLOTS_PER_HEAD, tq), BF16)
    w_heads = (
        jnp.concatenate([qt[0:hd], zeros_hd, aug[0:16], zeros_tail], axis=0),
        jnp.concatenate([zeros_hd, qt[hd:2 * hd], aug[16:32], zeros_tail], axis=0),
    )
    ones_rows = jnp.where(lax.broadcasted_iota(jnp.int32, (16, tk), 0) == 0, 1.0, 0.0).astype(BF16)
    n_blocks = (qi * tq) // tk + 1

    def issue_scores(kj, slot):
        ks = pl.multiple_of(kj * tk, tk)
        kb = kaug_ref[0, pl.ds(ks, tk), :]
        for h in range(2):
            s_scr[slot, h] = _dot(kb, w_heads[h])

    def pv_prev(kj_prev, h):
        v_aug = jnp.concatenate([vt_ref[0, kj_prev, hd * h:hd * (h + 1), :], ones_rows], axis=0)
        return _dot(v_aug, p_scr[h])

    def step(kj, slot, carry, masked):
        if not masked:
            issue_scores(kj + 1, 1 - slot)
        out = []
        for h in range(2):
            m_old, alpha_prev, acc = carry[h]
            pv = pv_prev(jnp.maximum(kj - 1, 0), h)
            s = s_scr[slot, h]
            if masked:
                key = kj * tk + lax.broadcasted_iota(jnp.int32, (tk, tq), 0)
                qry = qi * tq + lax.broadcasted_iota(jnp.int32, (tk, tq), 1)
                s = jnp.where(key <= qry, s, MASK_VALUE)
            m_new = jnp.maximum(m_old, jnp.max(s, axis=0, keepdims=True))
            alpha = jnp.exp2(m_old - m_new)
            p_scr[h] = jnp.exp2(s - m_new).astype(BF16)
            out.append((m_new, alpha, acc * alpha_prev + pv))
        return tuple(out)

    def pair(i, carry):
        carry = step(2 * i, 0, carry, False)
        return step(2 * i + 1, 1, carry, False)

    def finish(carry):
        outs = []
        for h in range(2):
            _, alpha, acc = carry[h]
            acc = acc * alpha + pv_prev(n_blocks - 1, h)
            outs.append(acc[0:hd] / acc[hd:hd + 1])
        ot_ref[0] = jnp.concatenate(outs, axis=0)

    p_scr[...] = jnp.zeros_like(p_scr)
    issue_scores(0, 0)
    init = tuple((jnp.full((1, tq), MASK_VALUE, F32), jnp.ones((1, tq), F32),
                  jnp.zeros((hd + 16, tq), F32)) for _ in range(2))
    n_unmasked = n_blocks - 1
    carry = lax.fori_loop(0, n_unmasked // 2, pair, init)

    @pl.when(lax.rem(n_unmasked, 2) == 0)
    def _():
        finish(step(n_blocks - 1, 0, carry, True))

    @pl.when(lax.rem(n_unmasked, 2) == 1)
    def _():
        finish(step(n_blocks - 1, 1, step(n_blocks - 2, 0, carry, False), True))


def _fox_attn(kaug, qt, vt, augq, *, tq, tk):
    b, s, _ = kaug.shape
    width = FOX_HEADS * FOX_HEAD_DIM
    nk = s // tk
    return pl.pallas_call(
        functools.partial(_fox_attn_kernel, tq=tq, tk=tk),
        grid=(b, FOX_PAIRS, s // tq),
        in_specs=[
            pl.BlockSpec((1, s, 2 * LANES), lambda bi, p, qi: (bi, 0, p)),
            pl.BlockSpec((1, nk, LANES, tk), lambda bi, p, qi: (bi, 0, p, 0)),
            pl.BlockSpec((1, LANES, tq), lambda bi, p, qi: (bi, p, qi)),
            pl.BlockSpec((1, 2 * AUG_SLOTS_PER_HEAD, tq), lambda bi, p, qi: (bi, p, qi)),
        ],
        out_specs=pl.BlockSpec((1, LANES, tq), lambda bi, p, qi: (bi, p, qi)),
        out_shape=jax.ShapeDtypeStruct((b, width, s), F32),
        scratch_shapes=[pltpu.VMEM((2, 2, tk, tq), F32), pltpu.VMEM((2, tk, tq), BF16)],
        compiler_params=_cparams(3),
        name="fox_attn",
    )(kaug, vt, qt, augq)


def _ple(h1, p, wup_ref, wgate_ref):
    gate = jax.nn.sigmoid(_dot(h1.astype(BF16), wgate_ref[...]))
    up = _dot(p.astype(BF16), wup_ref[...])
    return h1 + up * gate


def _fox_out_kernel(ot_ref, zt_ref, x_ref, p_ref, wout_ref, wup_ref, wgate_ref, h_ref):
    z = zt_ref[0].astype(F32)
    gt = (ot_ref[0] * (z * jax.nn.sigmoid(z))).astype(BF16)
    y = lax.dot_general(gt, wout_ref[...], TN_DIMS, preferred_element_type=F32)
    h_ref[0] = _ple(x_ref[0] + y, p_ref[0], wup_ref, wgate_ref)


def _fox_out(ot, zt, x, p, w_out, w_up, w_gate, *, tm):
    b, s, d = x.shape
    width = ot.shape[1]
    pd = p.shape[-1]
    const = lambda shape: pl.BlockSpec(shape, lambda bi, i: (0,) * len(shape))
    return pl.pallas_call(
        _fox_out_kernel,
        grid=(b, s // tm),
        in_specs=[
            pl.BlockSpec((1, width, tm), lambda bi, i: (bi, 0, i)),
            pl.BlockSpec((1, width, tm), lambda bi, i: (bi, 0, i)),
            pl.BlockSpec((1, tm, d), lambda bi, i: (bi, i, 0)),
            pl.BlockSpec((1, tm, pd), lambda bi, i: (bi, i, 0)),
            const((width, d)), const((pd, d)), const((d, d)),
        ],
        out_specs=pl.BlockSpec((1, tm, d), lambda bi, i: (bi, i, 0)),
        out_shape=jax.ShapeDtypeStruct((b, s, d), F32),
        compiler_params=_cparams(2),
        name="fox_out",
    )(ot, zt, x, p, w_out.astype(BF16), w_up.astype(BF16), w_gate.astype(BF16))


def _dil_in_kernel(h_ref, g_ref, w_ref, o_ref, hn_ref):
    @pl.when(pl.program_id(2) == 0)
    def _():
        hn_ref[...] = _rms_norm(h_ref[0], g_ref[...]).astype(BF16)

    o_ref[0] = _dot(hn_ref[...], w_ref[...]).astype(BF16)


def _dil_in(h, norm_g, w_in, *, tm, tn):
    b, s, d = h.shape
    n = w_in.shape[1]
    return pl.pallas_call(
        _dil_in_kernel,
        grid=(b, s // tm, n // tn),
        in_specs=[
            pl.BlockSpec((1, tm, d), lambda bi, i, j: (bi, i, 0)),
            pl.BlockSpec((1, d), lambda bi, i, j: (0, 0)),
            pl.BlockSpec((d, tn), lambda bi, i, j: (0, j)),
        ],
        out_specs=pl.BlockSpec((1, tm, tn), lambda bi, i, j: (bi, i, j)),
        out_shape=jax.ShapeDtypeStruct((b, s, n), BF16),
        scratch_shapes=[pltpu.VMEM((tm, d), BF16)],
        compiler_params=_cparams(3),
        name="dil_in",
    )(h, norm_g.reshape(1, d), w_in.astype(BF16))


def _dil_attn_kernel(q_ref, kc_ref, vc_ref, kp_ref, vp_ref, o_ref, st_ref, *,
                     dilation, kb, neg_slopes):
    n = pl.program_id(2)
    nw = DIL_WINDOW_STEPS
    hd = DIL_HEAD_DIM
    scale = hd ** -0.5
    row = lax.broadcasted_iota(jnp.int32, (nw, 2 * nw), 0)
    col = lax.broadcasted_iota(jnp.int32, (nw, 2 * nw), 1)
    dist = nw + row - col
    band = (dist >= 0) & (dist <= nw)
    dist_f = (dist * dilation).astype(F32)
    lane = lax.broadcasted_iota(jnp.int32, (nw, LANES), 1)

    for jb in range(kb):
        if jb == 0:
            kk = jnp.concatenate([kp_ref[0], kc_ref[0, 0:nw]], axis=0)
            vv = jnp.concatenate([vp_ref[0], vc_ref[0, 0:nw]], axis=0)
            valid = band & (col >= jnp.where(n > 0, 0, nw))
        else:
            kk = kc_ref[0, nw * (jb - 1):nw * (jb + 1)]
            vv = vc_ref[0, nw * (jb - 1):nw * (jb + 1)]
            valid = band
        stats = jnp.zeros((nw, LANES), F32)
        for h in range(DIL_HEADS_PER_GROUP):
            hs = slice(hd * h, hd * (h + 1))
            q = q_ref[0, nw * jb:nw * (jb + 1), hs]
            s = lax.dot_general(q, kk[:, hs], NT_DIMS, preferred_element_type=F32)
            s = s * scale + neg_slopes[h] * dist_f
            s = jnp.where(valid, s, MASK_VALUE)
            m = jnp.max(s, axis=1, keepdims=True)
            p = jnp.exp(s - m)
            l = jnp.sum(p, axis=1, keepdims=True)
            o = _dot(p.astype(BF16), vv[:, hs])
            o_ref[0, nw * jb:nw * (jb + 1), hs] = o / l
            stats = jnp.where(lane == h, m + jnp.log(l), stats)
        st_ref[0, nw * jb:nw * (jb + 1), :] = stats


def _alibi_neg_slopes(group):
    n = len(DIL_PATTERN) * DIL_HEADS_PER_GROUP
    k = np.arange(1, n + 1, dtype=np.float32)
    slopes = np.float32(2.0) ** (np.float32(-ALIBI_MAX_EXP) * k / np.float32(n))
    lo = group * DIL_HEADS_PER_GROUP
    return tuple(float(-v) for v in slopes[lo:lo + DIL_HEADS_PER_GROUP])


def _dil_attn(proj, group, *, kb):
    b, s, ncol = proj.shape
    _, dilation = DIL_PATTERN[group]
    nw = DIL_WINDOW_STEPS
    gw = DIL_HEADS_PER_GROUP * DIL_HEAD_DIM
    ncb = ncol // gw
    n_groups = len(DIL_PATTERN)
    length = s // dilation
    rows = nw * kb
    view = proj.reshape(b, length, dilation * ncol)
    qc, kc, vc = group, n_groups + group, 2 * n_groups + group

    def cur(cb):
        return pl.BlockSpec((1, rows, gw), lambda bi, r, n: (bi, n, r * ncb + cb))

    def prev(cb):
        return pl.BlockSpec((1, nw, gw),
                            lambda bi, r, n: (bi, jnp.maximum(n * kb - 1, 0), r * ncb + cb))

    o, st = pl.pallas_call(
        functools.partial(_dil_attn_kernel, dilation=dilation, kb=kb,
                          neg_slopes=_alibi_neg_slopes(group)),
        grid=(b, dilation, length // rows),
        in_specs=[cur(qc), cur(kc), cur(vc), prev(kc), prev(vc)],
        out_specs=[
            pl.BlockSpec((1, rows, gw), lambda bi, r, n: (bi, n, r)),
            pl.BlockSpec((1, rows, LANES), lambda bi, r, n: (bi, n, r)),
        ],
        out_shape=[
            jax.ShapeDtypeStruct((b, length, dilation * gw), F32),
            jax.ShapeDtypeStruct((b, length, dilation * LANES), F32),
        ],
        compiler_params=_cparams(3),
        name=f"dil_attn_g{group}",
    )(view, view, view, view, view)
    return o.reshape(b, s, gw), st.reshape(b, s, LANES)


def _dil_out_kernel(o0_ref, o1_ref, o2_ref, s0_ref, s1_ref, s2_ref, z_ref, h_ref, p_ref,
                    e_ref, wout_ref, wup_ref, wgate_ref, g_ref, out_ref):
    lse = (s0_ref[0], s1_ref[0], s2_ref[0])
    m = jnp.maximum(jnp.maximum(lse[0], lse[1]), lse[2])
    e = [jnp.exp(v - m) for v in lse]
    inv = 1.0 / (e[0] + e[1] + e[2])
    o = None
    for ev, o_ref in zip(e, (o0_ref, o1_ref, o2_ref)):
        w = ev * inv
        w_hi = w.astype(BF16)
        w_lo = (w - w_hi.astype(F32)).astype(BF16)
        w_full = _dot(w_hi, e_ref[...]) + _dot(w_lo, e_ref[...])
        term = w_full * o_ref[0]
        o = term if o is None else o + term
    z = z_ref[0].astype(F32)
    g = (o * (z * jax.nn.sigmoid(z))).astype(BF16)
    h1 = h_ref[0] + _dot(g, wout_ref[...])
    out_ref[0] = _rms_norm(_ple(h1, p_ref[0], wup_ref, wgate_ref), g_ref[...])


def _dil_out(outs, stats, proj, h, p, w_out, w_up, w_gate, final_g, *, tm):
    b, s, d = h.shape
    gw = outs[0].shape[-1]
    pd = p.shape[-1]
    z_block = proj.shape[-1] // gw - 1
    expand = np.zeros((LANES, gw), np.float32)
    for hh in range(DIL_HEADS_PER_GROUP):
        expand[hh, DIL_HEAD_DIM * hh:DIL_HEAD_DIM * (hh + 1)] = 1.0
    const = lambda shape: pl.BlockSpec(shape, lambda bi, i: (0,) * len(shape))
    row = lambda w: pl.BlockSpec((1, tm, w), lambda bi, i: (bi, i, 0))
    return pl.pallas_call(
        _dil_out_kernel,
        grid=(b, s // tm),
        in_specs=[row(gw)] * 3 + [row(LANES)] * 3 + [
            pl.BlockSpec((1, tm, gw), lambda bi, i: (bi, i, z_block)),
            row(d), row(pd),
            const((LANES, gw)), const((gw, d)), const((pd, d)), const((d, d)), const((1, d)),
        ],
        out_specs=row(d),
        out_shape=jax.ShapeDtypeStruct((b, s, d), F32),
        compiler_params=_cparams(2),
        name="dil_out",
    )(*outs, *stats, proj, h, p, jnp.asarray(expand, BF16), w_out.astype(BF16),
      w_up.astype(BF16), w_gate.astype(BF16), final_g.reshape(1, d))


def kernel(x, p, fox_norm, fox_w_in, fox_b_f, fox_w_out, dil_norm, dil_w_in, dil_w_out,
           ple_w_up, ple_w_gate, final_norm):
    s = x.shape[1]
    tm = min(512, s)
    tq = min(256, s)
    tk = min(512, s)
    kaug, qt, vt, zt, augq = _fox_in(x, fox_norm[0], fox_w_in[0], fox_b_f[0], tm=tm, ch=tk)
    ot = _fox_attn(kaug, qt, vt, augq, tq=tq, tk=tk)
    h = _fox_out(ot, zt, x, p[0], fox_w_out[0], ple_w_up[0], ple_w_gate[0], tm=tm)

    proj = _dil_in(h, dil_norm[0], dil_w_in[0], tm=min(1024, s), tn=2048)
    outs, stats = [], []
    for group, (_, dilation) in enumerate(DIL_PATTERN):
        kb = min(4, s // dilation // DIL_WINDOW_STEPS)
        o, st = _dil_attn(proj, group, kb=kb)
        outs.append(o)
        stats.append(st)
    return _dil_out(outs, stats, proj, h, p[1], dil_w_out[0], ple_w_up[1], ple_w_gate[1],
                    final_norm, tm=tm)
```

```python
import functools

import numpy as np
import jax
import jax.numpy as jnp
from jax import lax
from jax.experimental import pallas as pl
from jax.experimental.pallas import tpu as pltpu

F32 = jnp.float32
BF16 = jnp.bfloat16

RMS_EPS = 1e-6
FOX_HEADS = 16
FOX_HEAD_DIM = 64
FOX_PAIRS = FOX_HEADS // 2
DIL_PATTERN = ((128, 1), (512, 4), (2048, 16))
DIL_HEADS_PER_GROUP = 8
DIL_HEAD_DIM = 128
DIL_WINDOW_STEPS = 128
ALIBI_MAX_EXP = 8.0
MASK_VALUE = -1e30
LOG2E = 1.4426950408889634

LANES = 128
AUG_SLOTS_PER_HEAD = 16
STAT_PIECE_LANES = 8
VMEM_LIMIT_BYTES = 56 * 1024 * 1024

NT_DIMS = (((1,), (1,)), ((), ()))
TN_DIMS = (((0,), (0,)), ((), ()))


def _cparams(n_axes):
    return pltpu.CompilerParams(
        dimension_semantics=("arbitrary",) * n_axes,
        vmem_limit_bytes=VMEM_LIMIT_BYTES,
    )


def _rms_norm(x, g):
    ms = jnp.mean(x * x, axis=-1, keepdims=True)
    return x * lax.rsqrt(ms + RMS_EPS) * g


def _log_sigmoid(x):
    return jnp.minimum(x, 0.0) - jnp.log1p(jnp.exp(-jnp.abs(x)))


def _split3(x):
    hi = x.astype(BF16).astype(F32)
    r1 = x - hi
    mid = r1.astype(BF16).astype(F32)
    lo = (r1 - mid).astype(BF16).astype(F32)
    return hi, mid, lo


def _dot(a, b):
    return jnp.dot(a, b, preferred_element_type=F32)


def _fox_in_kernel(x_ref, g_ref, wnat_ref, wt_ref, wft_ref, bnat_ref, bt_ref, tril_ref,
                   triu_ref, pk_ref, pqt_ref,
                   kaug_ref, qt_ref, vt_ref, zt_ref, augq_ref,
                   carry_ref, carryt_ref, *, tm, ch):
    i = pl.program_id(1)

    @pl.when(i == 0)
    def _():
        carry_ref[...] = jnp.zeros_like(carry_ref)
        carryt_ref[...] = jnp.zeros_like(carryt_ref)

    hn = _rms_norm(x_ref[0], g_ref[...]).astype(BF16)
    width = FOX_HEADS * FOX_HEAD_DIM

    nat = _dot(hn, wnat_ref[...])
    k = nat[:, :width]
    logf = _log_sigmoid(nat[:, width:] + bnat_ref[...])
    hi, mid, lo = _split3(logf)
    tril = tril_ref[...]
    c = (_dot(tril, hi.astype(BF16)) + _dot(tril, mid.astype(BF16))
         + _dot(tril, lo.astype(BF16)) + carry_ref[...])
    carry_ref[...] = c[tm - 1:tm, :]
    nhi, nmid, nlo = _split3(c * -LOG2E)
    lane = lax.broadcasted_iota(jnp.int32, (tm, LANES), 1)
    slots = jnp.where(lane < 16, nhi,
                      jnp.where(lane < 32, pltpu.roll(nmid, 16, 1),
                                jnp.where(lane < 48, pltpu.roll(nlo, 32, 1),
                                          jnp.where(lane == 48, 1.0, 0.0))))
    kaug = _dot(slots.astype(BF16), pk_ref[...])
    for p in range(FOX_PAIRS):
        kaug_ref[0, :, 2 * LANES * p:2 * LANES * p + LANES] = (
            k[:, LANES * p:LANES * (p + 1)].astype(BF16))
        kaug_ref[0, :, 2 * LANES * p + LANES:2 * LANES * (p + 1)] = (
            kaug[:, LANES * p:LANES * (p + 1)].astype(BF16))

    rows = 512
    for c0 in range(0, 3 * width, rows):
        res = lax.dot_general(wt_ref[c0:c0 + rows, :], hn, NT_DIMS,
                              preferred_element_type=F32).astype(BF16)
        which, off = divmod(c0, width)
        if which == 0:
            qt_ref[0, off:off + rows, :] = res
        elif which == 1:
            for j in range(tm // ch):
                vt_ref[0, j, off:off + rows, :] = res[:, j * ch:(j + 1) * ch]
        else:
            zt_ref[0, off:off + rows, :] = res

    ft = lax.dot_general(wft_ref[...], hn, NT_DIMS, preferred_element_type=F32)
    logft = _log_sigmoid(ft + bt_ref[...])
    hi, mid, lo = _split3(logft)
    stack = jnp.concatenate([hi, mid, lo], axis=0).astype(BF16)
    cs = _dot(stack, triu_ref[...])
    ct = cs[0:16] + cs[16:32] + cs[32:48] + carryt_ref[:, 0:1]
    carryt_ref[...] = jnp.broadcast_to(ct[:, tm - 1:tm], carryt_ref.shape)
    hi, mid, lo = _split3(ct * LOG2E)
    ones = jnp.where(lax.broadcasted_iota(jnp.int32, (16, tm), 0) == 0, 1.0, 0.0)
    slots_t = jnp.concatenate([hi, mid, lo, ones], axis=0).astype(BF16)
    augq_ref[0] = _dot(pqt_ref[...], slots_t).astype(BF16)


def _placement_matrices():
    pk = np.zeros((LANES, FOX_PAIRS * LANES), np.float32)
    pqt = np.zeros((FOX_HEADS * AUG_SLOTS_PER_HEAD, 64), np.float32)
    for h in range(FOX_HEADS):
        p, odd = divmod(h, 2)
        base = 6 * odd
        for piece in range(3):
            pk[48, LANES * p + base + piece] = 1.0
            pk[16 * piece + h, LANES * p + base + 3 + piece] = 1.0
            pqt[AUG_SLOTS_PER_HEAD * h + base + piece, 16 * piece + h] = 1.0
            pqt[AUG_SLOTS_PER_HEAD * h + base + 3 + piece, 48] = 1.0
    return jnp.asarray(pk, BF16), jnp.asarray(pqt, BF16)


def _fox_in(x, norm_g, w_in, b_f, *, tm, ch):
    b, s, d = x.shape
    width = FOX_HEADS * FOX_HEAD_DIM
    wq = w_in[:, :width] * (FOX_HEAD_DIM ** -0.5 * LOG2E)
    wk = w_in[:, width:2 * width]
    wv = w_in[:, 2 * width:3 * width]
    wz = w_in[:, 3 * width:4 * width]
    wf = w_in[:, 4 * width:]
    wnat = jnp.concatenate([wk, jnp.pad(wf, ((0, 0), (0, LANES - FOX_HEADS)))], axis=1).astype(BF16)
    wt = jnp.concatenate([wq, wv, wz], axis=1).T.astype(BF16)
    wft = wf.T.astype(BF16)
    bnat = jnp.pad(b_f, (0, LANES - FOX_HEADS)).reshape(1, LANES)
    bt = b_f.reshape(FOX_HEADS, 1)
    r = np.arange(tm)
    tril = jnp.asarray(r[None, :] <= r[:, None], BF16)
    triu = jnp.asarray(r[:, None] <= r[None, :], BF16)
    pk, pqt = _placement_matrices()

    const = lambda shape: pl.BlockSpec(shape, lambda bi, i: (0,) * len(shape))
    return pl.pallas_call(
        functools.partial(_fox_in_kernel, tm=tm, ch=ch),
        grid=(b, s // tm),
        in_specs=[
            pl.BlockSpec((1, tm, d), lambda bi, i: (bi, i, 0)),
            const((1, d)),
            const(wnat.shape), const(wt.shape), const(wft.shape),
            const((1, LANES)), const((FOX_HEADS, 1)),
            const((tm, tm)), const((tm, tm)),
            const(pk.shape), const(pqt.shape),
        ],
        out_specs=[
            pl.BlockSpec((1, tm, 2 * width), lambda bi, i: (bi, i, 0)),
            pl.BlockSpec((1, width, tm), lambda bi, i: (bi, 0, i)),
            pl.BlockSpec((1, tm // ch, width, ch), lambda bi, i: (bi, i, 0, 0)),
            pl.BlockSpec((1, width, tm), lambda bi, i: (bi, 0, i)),
            pl.BlockSpec((1, FOX_HEADS * AUG_SLOTS_PER_HEAD, tm), lambda bi, i: (bi, 0, i)),
        ],
        out_shape=[
            jax.ShapeDtypeStruct((b, s, 2 * width), BF16),
            jax.ShapeDtypeStruct((b, width, s), BF16),
            jax.ShapeDtypeStruct((b, s // ch, width, ch), BF16),
            jax.ShapeDtypeStruct((b, width, s), BF16),
            jax.ShapeDtypeStruct((b, FOX_HEADS * AUG_SLOTS_PER_HEAD, s), BF16),
        ],
        scratch_shapes=[pltpu.VMEM((1, LANES), F32), pltpu.VMEM((FOX_HEADS, LANES), F32)],
        compiler_params=_cparams(2),
        name="fox_in",
    )(x, norm_g.reshape(1, d), wnat, wt, wft, bnat, bt, tril, triu, pk, pqt)


def _fox_attn_kernel(kaug_ref, vt_ref, qt_ref, augq_ref, ot_ref, s_scr, p_scr, *, tq, tk):
    qi = pl.program_id(2)
    hd = FOX_HEAD_DIM
    qt = qt_ref[0]
    aug = augq_ref[0]
    zeros_hd = jnp.zeros((hd, tq), BF16)
    zeros_tail = jnp.zeros((LANES - AUG_SLOTS_PER_HEAD, tq), BF16)
    w_heads = (
        jnp.concatenate([qt[0:hd], zeros_hd, aug[0:16], zeros_tail], axis=0),
        jnp.concatenate([zeros_hd, qt[hd:2 * hd], aug[16:32], zeros_tail], axis=0),
    )
    ones_rows = jnp.where(lax.broadcasted_iota(jnp.int32, (16, tk), 0) == 0, 1.0, 0.0).astype(BF16)
    n_blocks = (qi * tq) // tk + 1

    def issue_scores(kj, slot):
        ks = pl.multiple_of(kj * tk, tk)
        kb = kaug_ref[0, pl.ds(ks, tk), :]
        for h in range(2):
            s_scr[slot, h] = _dot(kb, w_heads[h])

    def pv_prev(kj_prev, h):
        v_aug = jnp.concatenate([vt_ref[0, kj_prev, hd * h:hd * (h + 1), :], ones_rows], axis=0)
        return _dot(v_aug, p_scr[h])

    def step(kj, slot, carry, masked):
        if not masked:
            issue_scores(kj + 1, 1 - slot)
        out = []
        for h in range(2):
            m_old, alpha_prev, acc = carry[h]
            pv = pv_prev(jnp.maximum(kj - 1, 0), h)
            s = s_scr[slot, h]
            if masked:
                key = kj * tk + lax.broadcasted_iota(jnp.int32, (tk, tq), 0)
                qry = qi * tq + lax.broadcasted_iota(jnp.int32, (tk, tq), 1)
                s = jnp.where(key <= qry, s, MASK_VALUE)
            m_new = jnp.maximum(m_old, jnp.max(s, axis=0, keepdims=True))
            alpha = jnp.exp2(m_old - m_new)
            p_scr[h] = jnp.exp2(s - m_new).astype(BF16)
            out.append((m_new, alpha, acc * alpha_prev + pv))
        return tuple(out)

    def pair(i, carry):
        carry = step(2 * i, 0, carry, False)
        return step(2 * i + 1, 1, carry, False)

    def finish(carry):
        outs = []
        for h in range(2):
            _, alpha, acc = carry[h]
            acc = acc * alpha + pv_prev(n_blocks - 1, h)
            outs.append(acc[0:hd] / acc[hd:hd + 1])
        ot_ref[0] = jnp.concatenate(outs, axis=0)

    p_scr[...] = jnp.zeros_like(p_scr)
    issue_scores(0, 0)
    init = tuple((jnp.full((1, tq), MASK_VALUE, F32), jnp.ones((1, tq), F32),
                  jnp.zeros((hd + 16, tq), F32)) for _ in range(2))
    n_unmasked = n_blocks - 1
    carry = lax.fori_loop(0, n_unmasked // 2, pair, init)

    @pl.when(lax.rem(n_unmasked, 2) == 0)
    def _():
        finish(step(n_blocks - 1, 0, carry, True))

    @pl.when(lax.rem(n_unmasked, 2) == 1)
    def _():
        finish(step(n_blocks - 1, 1, step(n_blocks - 2, 0, carry, False), True))


def _fox_attn(kaug, qt, vt, augq, *, tq, tk):
    b, s, _ = kaug.shape
    width = FOX_HEADS * FOX_HEAD_DIM
    nk = s // tk
    return pl.pallas_call(
        functools.partial(_fox_attn_kernel, tq=tq, tk=tk),
        grid=(b, FOX_PAIRS, s // tq),
        in_specs=[
            pl.BlockSpec((1, s, 2 * LANES), lambda bi, p, qi: (bi, 0, p)),
            pl.BlockSpec((1, nk, LANES, tk), lambda bi, p, qi: (bi, 0, p, 0)),
            pl.BlockSpec((1, LANES, tq), lambda bi, p, qi: (bi, p, qi)),
            pl.BlockSpec((1, 2 * AUG_SLOTS_PER_HEAD, tq), lambda bi, p, qi: (bi, p, qi)),
        ],
        out_specs=pl.BlockSpec((1, LANES, tq), lambda bi, p, qi: (bi, p, qi)),
        out_shape=jax.ShapeDtypeStruct((b, width, s), F32),
        scratch_shapes=[pltpu.VMEM((2, 2, tk, tq), F32), pltpu.VMEM((2, tk, tq), BF16)],
        compiler_params=_cparams(3),
        name="fox_attn",
    )(kaug, vt, qt, augq)


def _ple(h1, p, wup_ref, wgate_ref):
    gate = jax.nn.sigmoid(_dot(h1.astype(BF16), wgate_ref[...]))
    up = _dot(p.astype(BF16), wup_ref[...])
    return h1 + up * gate


def _fox_out_kernel(ot_ref, zt_ref, x_ref, p_ref, wout_ref, wup_ref, wgate_ref, h_ref):
    z = zt_ref[0].astype(F32)
    gt = (ot_ref[0] * (z * jax.nn.sigmoid(z))).astype(BF16)
    y = lax.dot_general(gt, wout_ref[...], TN_DIMS, preferred_element_type=F32)
    h_ref[0] = _ple(x_ref[0] + y, p_ref[0], wup_ref, wgate_ref)


def _fox_out(ot, zt, x, p, w_out, w_up, w_gate, *, tm):
    b, s, d = x.shape
    width = ot.shape[1]
    pd = p.shape[-1]
    const = lambda shape: pl.BlockSpec(shape, lambda bi, i: (0,) * len(shape))
    return pl.pallas_call(
        _fox_out_kernel,
        grid=(b, s // tm),
        in_specs=[
            pl.BlockSpec((1, width, tm), lambda bi, i: (bi, 0, i)),
            pl.BlockSpec((1, width, tm), lambda bi, i: (bi, 0, i)),
            pl.BlockSpec((1, tm, d), lambda bi, i: (bi, i, 0)),
            pl.BlockSpec((1, tm, pd), lambda bi, i: (bi, i, 0)),
            const((width, d)), const((pd, d)), const((d, d)),
        ],
        out_specs=pl.BlockSpec((1, tm, d), lambda bi, i: (bi, i, 0)),
        out_shape=jax.ShapeDtypeStruct((b, s, d), F32),
        compiler_params=_cparams(2),
        name="fox_out",
    )(ot, zt, x, p, w_out.astype(BF16), w_up.astype(BF16), w_gate.astype(BF16))


PERM_ROWS = 256


def _phase_major_matrix(dilation):
    n = PERM_ROWS // dilation
    p = np.zeros((PERM_ROWS, PERM_ROWS), np.float32)
    for r in range(dilation):
        for i in range(n):
            p[r * n + i, dilation * i + r] = 1.0
    return p


def _dil_in_kernel(h_ref, g_ref, w_ref, *rest, dilations, tm):
    n_perm = sum(d > 1 for d in dilations)
    perm_refs, out_refs = rest[:n_perm], rest[n_perm:]
    hn = _rms_norm(h_ref[0], g_ref[...]).astype(BF16)
    cols = w_ref.shape[1] // len(dilations)
    perm_refs = list(perm_refs)
    for g, (dil, o_ref) in enumerate(zip(dilations, out_refs)):
        w = w_ref[:, cols * g:cols * (g + 1)]
        if dil == 1:
            o_ref[0] = _dot(hn, w).astype(BF16)
            continue
        perm = perm_refs.pop(0)[...]
        n = PERM_ROWS // dil
        for c in range(tm // PERM_ROWS):
            rows = slice(PERM_ROWS * c, PERM_ROWS * (c + 1))
            hp = _dot(perm, hn[rows]).astype(BF16)
            res = _dot(hp, w).astype(BF16)
            for r in range(dil):
                o_ref[0, n * c:n * (c + 1), cols * r:cols * (r + 1)] = res[n * r:n * (r + 1)]


def _dil_in(h, norm_g, w, dilations, *, tm):
    b, s, d = h.shape
    cols = w.shape[1] // len(dilations)
    perms = [jnp.asarray(_phase_major_matrix(dil), BF16) for dil in dilations if dil > 1]
    const = lambda shape: pl.BlockSpec(shape, lambda bi, i: (0,) * len(shape))
    return pl.pallas_call(
        functools.partial(_dil_in_kernel, dilations=dilations, tm=tm),
        grid=(b, s // tm),
        in_specs=[pl.BlockSpec((1, tm, d), lambda bi, i: (bi, i, 0)), const((1, d)),
                  const(w.shape)] + [const(p.shape) for p in perms],
        out_specs=[pl.BlockSpec((1, tm // dil, dil * cols), lambda bi, i: (bi, i, 0))
                   for dil in dilations],
        out_shape=[jax.ShapeDtypeStruct((b, s // dil, dil * cols), BF16) for dil in dilations],
        compiler_params=_cparams(2),
        name="dil_in_" + "_".join(str(dil) for dil in dilations),
    )(h, norm_g.reshape(1, d), w.astype(BF16), *perms)


def _dil_attn_kernel(q_ref, kc_ref, vc_ref, kp_ref, vp_ref, o_ref, st_ref, *,
                     dilation, kb, neg_slopes):
    n = pl.program_id(2)
    nw = DIL_WINDOW_STEPS
    hd = DIL_HEAD_DIM
    scale = hd ** -0.5
    row = lax.broadcasted_iota(jnp.int32, (nw, 2 * nw), 0)
    col = lax.broadcasted_iota(jnp.int32, (nw, 2 * nw), 1)
    dist = nw + row - col
    band = (dist >= 0) & (dist <= nw)
    dist_f = (dist * dilation).astype(F32)
    lane = lax.broadcasted_iota(jnp.int32, (nw, LANES), 1)

    for jb in range(kb):
        if jb == 0:
            kk = jnp.concatenate([kp_ref[0], kc_ref[0, 0:nw]], axis=0)
            vv = jnp.concatenate([vp_ref[0], vc_ref[0, 0:nw]], axis=0)
            valid = band & (col >= jnp.where(n > 0, 0, nw))
        else:
            kk = kc_ref[0, nw * (jb - 1):nw * (jb + 1)]
            vv = vc_ref[0, nw * (jb - 1):nw * (jb + 1)]
            valid = band
        stats = jnp.zeros((nw, LANES), F32)
        for h in range(DIL_HEADS_PER_GROUP):
            hs = slice(hd * h, hd * (h + 1))
            q = q_ref[0, nw * jb:nw * (jb + 1), hs]
            s = lax.dot_general(q, kk[:, hs], NT_DIMS, preferred_element_type=F32)
            s = s * scale + neg_slopes[h] * dist_f
            s = jnp.where(valid, s, MASK_VALUE)
            m = jnp.max(s, axis=1, keepdims=True)
            p = jnp.exp(s - m)
            l = jnp.sum(p, axis=1, keepdims=True)
            o = _dot(p.astype(BF16), vv[:, hs])
            o_ref[0, nw * jb:nw * (jb + 1), hs] = (o / l).astype(BF16)
            hi, mid, lo = _split3(m + jnp.log(l))
            stats = jnp.where(lane == h, hi, stats)
            stats = jnp.where(lane == h + STAT_PIECE_LANES, mid, stats)
            stats = jnp.where(lane == h + 2 * STAT_PIECE_LANES, lo, stats)
        st_ref[0, nw * jb:nw * (jb + 1), :] = stats.astype(BF16)


def _alibi_neg_slopes(group):
    n = len(DIL_PATTERN) * DIL_HEADS_PER_GROUP
    k = np.arange(1, n + 1, dtype=np.float32)
    slopes = np.float32(2.0) ** (np.float32(-ALIBI_MAX_EXP) * k / np.float32(n))
    lo = group * DIL_HEADS_PER_GROUP
    return tuple(float(-v) for v in slopes[lo:lo + DIL_HEADS_PER_GROUP])


def _dil_attn(view, group, *, kb):
    _, dilation = DIL_PATTERN[group]
    b, length, width = view.shape
    nw = DIL_WINDOW_STEPS
    gw = DIL_HEADS_PER_GROUP * DIL_HEAD_DIM
    ncb = width // dilation // gw
    rows = nw * kb
    qc, kc, vc = 0, 1, 2

    def cur(cb):
        return pl.BlockSpec((1, rows, gw), lambda bi, r, n: (bi, n, r * ncb + cb))

    def prev(cb):
        return pl.BlockSpec((1, nw, gw),
                            lambda bi, r, n: (bi, jnp.maximum(n * kb - 1, 0), r * ncb + cb))

    o, st = pl.pallas_call(
        functools.partial(_dil_attn_kernel, dilation=dilation, kb=kb,
                          neg_slopes=_alibi_neg_slopes(group)),
        grid=(b, dilation, length // rows),
        in_specs=[cur(qc), cur(kc), cur(vc), prev(kc), prev(vc)],
        out_specs=[
            pl.BlockSpec((1, rows, gw), lambda bi, r, n: (bi, n, r)),
            pl.BlockSpec((1, rows, LANES), lambda bi, r, n: (bi, n, r)),
        ],
        out_shape=[
            jax.ShapeDtypeStruct((b, length, dilation * gw), BF16),
            jax.ShapeDtypeStruct((b, length, dilation * LANES), BF16),
        ],
        compiler_params=_cparams(3),
        name=f"dil_attn_g{group}",
    )(view, view, view, view, view)
    return o, st


def _natural_rows(view_ref, perm_t, dilation, tm):
    cols = view_ref.shape[2] // dilation
    if dilation == 1:
        return view_ref[0]
    n = PERM_ROWS // dilation
    chunks = []
    for c in range(tm // PERM_ROWS):
        phase_major = jnp.concatenate(
            [view_ref[0, n * c:n * (c + 1), cols * r:cols * (r + 1)] for r in range(dilation)],
            axis=0)
        chunks.append(_dot(perm_t, phase_major))
    return jnp.concatenate(chunks, axis=0)


def _dil_out_kernel(o0_ref, o1_ref, o2_ref, s0_ref, s1_ref, s2_ref, z_ref, h_ref, p_ref,
                    pt1_ref, pt2_ref, fold_ref, e_ref, wout_ref, wup_ref, wgate_ref, g_ref,
                    out_ref, *, dilations, tm):
    perm_ts = (None, pt1_ref[...], pt2_ref[...])
    lse = []
    for st_ref, perm_t, dil in zip((s0_ref, s1_ref, s2_ref), perm_ts, dilations):
        pieces = _natural_rows(st_ref, perm_t, dil, tm).astype(BF16)
        lse.append(_dot(pieces, fold_ref[...]))
    m = jnp.maximum(jnp.maximum(lse[0], lse[1]), lse[2])
    e = [jnp.exp(v - m) for v in lse]
    inv = 1.0 / (e[0] + e[1] + e[2])
    o = None
    for ev, o_ref, perm_t, dil in zip(e, (o0_ref, o1_ref, o2_ref), perm_ts, dilations):
        w = ev * inv
        w_hi = w.astype(BF16)
        w_lo = (w - w_hi.astype(F32)).astype(BF16)
        w_full = _dot(w_hi, e_ref[...]) + _dot(w_lo, e_ref[...])
        term = w_full * _natural_rows(o_ref, perm_t, dil, tm)
        o = term if o is None else o + term
    z = z_ref[0].astype(F32)
    g = (o * (z * jax.nn.sigmoid(z))).astype(BF16)
    h1 = h_ref[0] + _dot(g, wout_ref[...])
    out_ref[0] = _rms_norm(_ple(h1, p_ref[0], wup_ref, wgate_ref), g_ref[...])


def _dil_out(outs, stats, proj0, h, p, w_out, w_up, w_gate, final_g, *, tm):
    b, s, d = h.shape
    gw = DIL_HEADS_PER_GROUP * DIL_HEAD_DIM
    pd = p.shape[-1]
    dilations = tuple(dil for _, dil in DIL_PATTERN)
    z_block = proj0.shape[-1] // gw - 1
    expand = np.zeros((LANES, gw), np.float32)
    fold = np.zeros((LANES, LANES), np.float32)
    for hh in range(DIL_HEADS_PER_GROUP):
        expand[hh, DIL_HEAD_DIM * hh:DIL_HEAD_DIM * (hh + 1)] = 1.0
        for piece in range(3):
            fold[hh + STAT_PIECE_LANES * piece, hh] = 1.0
    perm_ts = [jnp.asarray(_phase_major_matrix(dil).T, BF16) for dil in dilations[1:]]
    const = lambda shape: pl.BlockSpec(shape, lambda bi, i: (0,) * len(shape))
    row = lambda w: pl.BlockSpec((1, tm, w), lambda bi, i: (bi, i, 0))
    view = lambda w: [pl.BlockSpec((1, tm // dil, dil * w), lambda bi, i: (bi, i, 0))
                      for dil in dilations]
    return pl.pallas_call(
        functools.partial(_dil_out_kernel, dilations=dilations, tm=tm),
        grid=(b, s // tm),
        in_specs=view(gw) + view(LANES) + [
            pl.BlockSpec((1, tm, gw), lambda bi, i: (bi, i, z_block)),
            row(d), row(pd),
            const((PERM_ROWS, PERM_ROWS)), const((PERM_ROWS, PERM_ROWS)), const((LANES, LANES)),
            const((LANES, gw)), const((gw, d)), const((pd, d)), const((d, d)), const((1, d)),
        ],
        out_specs=row(d),
        out_shape=jax.ShapeDtypeStruct((b, s, d), F32),
        compiler_params=_cparams(2),
        name="dil_out",
    )(*outs, *stats, proj0, h, p, *perm_ts, jnp.asarray(fold, BF16), jnp.asarray(expand, BF16),
      w_out.astype(BF16), w_up.astype(BF16), w_gate.astype(BF16), final_g.reshape(1, d))


def kernel(x, p, fox_norm, fox_w_in, fox_b_f, fox_w_out, dil_norm, dil_w_in, dil_w_out,
           ple_w_up, ple_w_gate, final_norm):
    s = x.shape[1]
    tm = min(512, s)
    tq = min(256, s)
    tk = min(512, s)
    kaug, qt, vt, zt, augq = _fox_in(x, fox_norm[0], fox_w_in[0], fox_b_f[0], tm=tm, ch=tk)
    ot = _fox_attn(kaug, qt, vt, augq, tq=tq, tk=tk)
    h = _fox_out(ot, zt, x, p[0], fox_w_out[0], ple_w_up[0], ple_w_gate[0], tm=tm)

    gw = DIL_HEADS_PER_GROUP * DIL_HEAD_DIM
    n_groups = len(DIL_PATTERN)
    w1 = dil_w_in[0]
    qkv = lambda g: [w1[:, (n_groups * part + g) * gw:(n_groups * part + g + 1) * gw]
                     for part in range(3)]
    w_natural = jnp.concatenate(qkv(0) + [w1[:, 3 * n_groups * gw:]], axis=1)
    w_phased = jnp.concatenate(qkv(1) + qkv(2), axis=1)
    (proj0,) = _dil_in(h, dil_norm[0], w_natural, (1,), tm=tm)
    views = [proj0] + list(_dil_in(h, dil_norm[0], w_phased,
                                   (DIL_PATTERN[1][1], DIL_PATTERN[2][1]), tm=min(PERM_ROWS, s)))
    outs, stats = [], []
    for group, (_, dilation) in enumerate(DIL_PATTERN):
        kb = min(4, s // dilation // DIL_WINDOW_STEPS)
        o, st = _dil_attn(views[group], group, kb=kb)
        outs.append(o)
        stats.append(st)
    return _dil_out(outs, stats, proj0, h, p[1], dil_w_out[0], ple_w_up[1], ple_w_gate[1],
                    final_norm, tm=tm)
```

```python
import functools

import numpy as np
import jax
import jax.numpy as jnp
from jax import lax
from jax.experimental import pallas as pl
from jax.experimental.pallas import tpu as pltpu

F32 = jnp.float32
BF16 = jnp.bfloat16

RMS_EPS = 1e-6
FOX_HEADS = 16
FOX_HEAD_DIM = 64
FOX_PAIRS = FOX_HEADS // 2
DIL_PATTERN = ((128, 1), (512, 4), (2048, 16))
DIL_HEADS_PER_GROUP = 8
DIL_HEAD_DIM = 128
DIL_WINDOW_STEPS = 128
ALIBI_MAX_EXP = 8.0
MASK_VALUE = -1e30
LOG2E = 1.4426950408889634

LANES = 128
AUG_SLOTS_PER_HEAD = 16
STAT_PIECE_LANES = 8
QUERY_LANES = 256
VMEM_LIMIT_BYTES = 56 * 1024 * 1024

NT_DIMS = (((1,), (1,)), ((), ()))
TN_DIMS = (((0,), (0,)), ((), ()))


def _cparams(n_axes):
    return pltpu.CompilerParams(
        dimension_semantics=("arbitrary",) * n_axes,
        vmem_limit_bytes=VMEM_LIMIT_BYTES,
    )


def _rms_norm(x, g):
    ms = jnp.mean(x * x, axis=-1, keepdims=True)
    return x * lax.rsqrt(ms + RMS_EPS) * g


def _log_sigmoid(x):
    return jnp.minimum(x, 0.0) - jnp.log1p(jnp.exp(-jnp.abs(x)))


def _split3(x):
    hi = x.astype(BF16).astype(F32)
    r1 = x - hi
    mid = r1.astype(BF16).astype(F32)
    lo = (r1 - mid).astype(BF16).astype(F32)
    return hi, mid, lo


def _dot(a, b):
    return jnp.dot(a, b, preferred_element_type=F32)


def _fox_in_kernel(x_ref, g_ref, wnat_ref, wt_ref, wft_ref, bnat_ref, bt_ref, tril_ref,
                   triu_ref, pk_ref, pqt_ref,
                   kaug_ref, qt_ref, vt_ref, zt_ref, augq_ref,
                   carry_ref, carryt_ref, *, tm, ch):
    i = pl.program_id(1)

    @pl.when(i == 0)
    def _():
        carry_ref[...] = jnp.zeros_like(carry_ref)
        carryt_ref[...] = jnp.zeros_like(carryt_ref)

    hn = _rms_norm(x_ref[0], g_ref[...]).astype(BF16)
    width = FOX_HEADS * FOX_HEAD_DIM

    nat = _dot(hn, wnat_ref[...])
    k = nat[:, :width]
    logf = _log_sigmoid(nat[:, width:] + bnat_ref[...])
    hi, mid, lo = _split3(logf)
    tril = tril_ref[...]
    c = (_dot(tril, hi.astype(BF16)) + _dot(tril, mid.astype(BF16))
         + _dot(tril, lo.astype(BF16)) + carry_ref[...])
    carry_ref[...] = c[tm - 1:tm, :]
    nhi, nmid, nlo = _split3(c * -LOG2E)
    lane = lax.broadcasted_iota(jnp.int32, (tm, LANES), 1)
    slots = jnp.where(lane < 16, nhi,
                      jnp.where(lane < 32, pltpu.roll(nmid, 16, 1),
                                jnp.where(lane < 48, pltpu.roll(nlo, 32, 1),
                                          jnp.where(lane == 48, 1.0, 0.0))))
    kaug = _dot(slots.astype(BF16), pk_ref[...])
    for p in range(FOX_PAIRS):
        kaug_ref[0, :, 2 * LANES * p:2 * LANES * p + LANES] = (
            k[:, LANES * p:LANES * (p + 1)].astype(BF16))
        kaug_ref[0, :, 2 * LANES * p + LANES:2 * LANES * (p + 1)] = (
            kaug[:, LANES * p:LANES * (p + 1)].astype(BF16))

    rows = 512
    for c0 in range(0, 3 * width, rows):
        res = lax.dot_general(wt_ref[c0:c0 + rows, :], hn, NT_DIMS,
                              preferred_element_type=F32).astype(BF16)
        which, off = divmod(c0, width)
        if which == 0:
            qt_ref[0, off:off + rows, :] = res
        elif which == 1:
            for j in range(tm // ch):
                vt_ref[0, j, off:off + rows, :] = res[:, j * ch:(j + 1) * ch]
        else:
            zt_ref[0, off:off + rows, :] = res

    ft = lax.dot_general(wft_ref[...], hn, NT_DIMS, preferred_element_type=F32)
    logft = _log_sigmoid(ft + bt_ref[...])
    hi, mid, lo = _split3(logft)
    stack = jnp.concatenate([hi, mid, lo], axis=0).astype(BF16)
    cs = _dot(stack, triu_ref[...])
    ct = cs[0:16] + cs[16:32] + cs[32:48] + carryt_ref[:, 0:1]
    carryt_ref[...] = jnp.broadcast_to(ct[:, tm - 1:tm], carryt_ref.shape)
    hi, mid, lo = _split3(ct * LOG2E)
    ones = jnp.where(lax.broadcasted_iota(jnp.int32, (16, tm), 0) == 0, 1.0, 0.0)
    slots_t = jnp.concatenate([hi, mid, lo, ones], axis=0).astype(BF16)
    augq_ref[0] = _dot(pqt_ref[...], slots_t).astype(BF16)


def _placement_matrices():
    pk = np.zeros((LANES, FOX_PAIRS * LANES), np.float32)
    pqt = np.zeros((FOX_HEADS * AUG_SLOTS_PER_HEAD, 64), np.float32)
    for h in range(FOX_HEADS):
        p, odd = divmod(h, 2)
        base = 6 * odd
        for piece in range(3):
            pk[48, LANES * p + base + piece] = 1.0
            pk[16 * piece + h, LANES * p + base + 3 + piece] = 1.0
            pqt[AUG_SLOTS_PER_HEAD * h + base + piece, 16 * piece + h] = 1.0
            pqt[AUG_SLOTS_PER_HEAD * h + base + 3 + piece, 48] = 1.0
    return jnp.asarray(pk, BF16), jnp.asarray(pqt, BF16)


def _fox_in(x, norm_g, w_in, b_f, *, tm, ch):
    b, s, d = x.shape
    width = FOX_HEADS * FOX_HEAD_DIM
    wq = w_in[:, :width] * (FOX_HEAD_DIM ** -0.5 * LOG2E)
    wk = w_in[:, width:2 * width]
    wv = w_in[:, 2 * width:3 * width]
    wz = w_in[:, 3 * width:4 * width]
    wf = w_in[:, 4 * width:]
    wnat = jnp.concatenate([wk, jnp.pad(wf, ((0, 0), (0, LANES - FOX_HEADS)))], axis=1).astype(BF16)
    wt = jnp.concatenate([wq, wv, wz], axis=1).T.astype(BF16)
    wft = wf.T.astype(BF16)
    bnat = jnp.pad(b_f, (0, LANES - FOX_HEADS)).reshape(1, LANES)
    bt = b_f.reshape(FOX_HEADS, 1)
    r = np.arange(tm)
    tril = jnp.asarray(r[None, :] <= r[:, None], BF16)
    triu = jnp.asarray(r[:, None] <= r[None, :], BF16)
    pk, pqt = _placement_matrices()

    const = lambda shape: pl.BlockSpec(shape, lambda bi, i: (0,) * len(shape))
    return pl.pallas_call(
        functools.partial(_fox_in_kernel, tm=tm, ch=ch),
        grid=(b, s // tm),
        in_specs=[
            pl.BlockSpec((1, tm, d), lambda bi, i: (bi, i, 0)),
            const((1, d)),
            const(wnat.shape), const(wt.shape), const(wft.shape),
            const((1, LANES)), const((FOX_HEADS, 1)),
            const((tm, tm)), const((tm, tm)),
            const(pk.shape), const(pqt.shape),
        ],
        out_specs=[
            pl.BlockSpec((1, tm, 2 * width), lambda bi, i: (bi, i, 0)),
            pl.BlockSpec((1, width, tm), lambda bi, i: (bi, 0, i)),
            pl.BlockSpec((1, tm // ch, width, ch), lambda bi, i: (bi, i, 0, 0)),
            pl.BlockSpec((1, width, tm), lambda bi, i: (bi, 0, i)),
            pl.BlockSpec((1, FOX_HEADS * AUG_SLOTS_PER_HEAD, tm), lambda bi, i: (bi, 0, i)),
        ],
        out_shape=[
            jax.ShapeDtypeStruct((b, s, 2 * width), BF16),
            jax.ShapeDtypeStruct((b, width, s), BF16),
            jax.ShapeDtypeStruct((b, s // ch, width, ch), BF16),
            jax.ShapeDtypeStruct((b, width, s), BF16),
            jax.ShapeDtypeStruct((b, FOX_HEADS * AUG_SLOTS_PER_HEAD, s), BF16),
        ],
        scratch_shapes=[pltpu.VMEM((1, LANES), F32), pltpu.VMEM((FOX_HEADS, LANES), F32)],
        compiler_params=_cparams(2),
        name="fox_in",
    )(x, norm_g.reshape(1, d), wnat, wt, wft, bnat, bt, tril, triu, pk, pqt)


def _fox_attn_kernel(kaug_ref, vt_ref, qt_ref, augq_ref, ot_ref, s_scr, p_scr, *, tq, tk):
    qi = pl.program_id(2)
    hd = FOX_HEAD_DIM
    qt = qt_ref[0]
    aug = augq_ref[0]
    zeros_hd = jnp.zeros((hd, tq), BF16)
    zeros_tail = jnp.zeros((LANES - AUG_SLOTS_PER_HEAD, tq), BF16)
    w_heads = (
        jnp.concatenate([qt[0:hd], zeros_hd, aug[0:16], zeros_tail], axis=0),
        jnp.concatenate([zeros_hd, qt[hd:2 * hd], aug[16:32], zeros_tail], axis=0),
    )
    ones_rows = jnp.where(lax.broadcasted_iota(jnp.int32, (16, tk), 0) == 0, 1.0, 0.0).astype(BF16)
    n_blocks = (qi * tq) // tk + 1
    chains = [(h, sub) for h in range(2) for sub in range(tq // QUERY_LANES)]

    def issue_scores(kj, slot):
        ks = pl.multiple_of(kj * tk, tk)
        kb = kaug_ref[0, pl.ds(ks, tk), :]
        block_max = []
        for c, (h, sub) in enumerate(chains):
            s = _dot(kb, w_heads[h][:, QUERY_LANES * sub:QUERY_LANES * (sub + 1)])
            s_scr[slot, c] = s
            block_max.append(jnp.max(s, axis=0, keepdims=True))
        return tuple(block_max)

    def pv_prev(kj_prev, c):
        h = chains[c][0]
        v_aug = jnp.concatenate([vt_ref[0, kj_prev, hd * h:hd * (h + 1), :], ones_rows], axis=0)
        return _dot(v_aug, p_scr[c])

    def step(kj, slot, state, masked):
        carry, block_max = state
        next_max = None if masked else issue_scores(kj + 1, 1 - slot)
        out = []
        for c, (h, sub) in enumerate(chains):
            m_old, alpha_prev, acc = carry[c]
            pv = pv_prev(jnp.maximum(kj - 1, 0), c)
            s = s_scr[slot, c]
            if masked:
                key = kj * tk + lax.broadcasted_iota(jnp.int32, (tk, QUERY_LANES), 0)
                qry = (qi * tq + QUERY_LANES * sub
                       + lax.broadcasted_iota(jnp.int32, (tk, QUERY_LANES), 1))
                s = jnp.where(key <= qry, s, MASK_VALUE)
                m_new = jnp.maximum(m_old, jnp.max(s, axis=0, keepdims=True))
            else:
                m_new = jnp.maximum(m_old, block_max[c])
            alpha = jnp.exp2(m_old - m_new)
            p_scr[c] = jnp.exp2(s - m_new).astype(BF16)
            out.append((m_new, alpha, acc * alpha_prev + pv))
        return tuple(out), next_max

    def pair(i, state):
        return step(2 * i + 1, 1, step(2 * i, 0, state, False), False)

    def finish(state):
        carry, _ = state
        for c, (h, sub) in enumerate(chains):
            _, alpha, acc = carry[c]
            acc = acc * alpha + pv_prev(n_blocks - 1, c)
            ot_ref[0, hd * h:hd * (h + 1), QUERY_LANES * sub:QUERY_LANES * (sub + 1)] = (
                acc[0:hd] / acc[hd:hd + 1])

    p_scr[...] = jnp.zeros_like(p_scr)
    first_max = issue_scores(0, 0)
    init = tuple((jnp.full((1, QUERY_LANES), MASK_VALUE, F32), jnp.ones((1, QUERY_LANES), F32),
                  jnp.zeros((hd + 16, QUERY_LANES), F32)) for _ in chains)
    n_unmasked = n_blocks - 1
    state = lax.fori_loop(0, n_unmasked // 2, pair, (init, first_max))

    odd = lax.rem(n_unmasked, 2)
    state = lax.fori_loop(0, odd, lambda _, st: step(n_blocks - 2, 0, st, False), state)
    finish(step(n_blocks - 1, odd, state, True))


def _fox_attn(kaug, qt, vt, augq, *, tq, tk):
    b, s, _ = kaug.shape
    width = FOX_HEADS * FOX_HEAD_DIM
    nk = s // tk
    return pl.pallas_call(
        functools.partial(_fox_attn_kernel, tq=tq, tk=tk),
        grid=(b, FOX_PAIRS, s // tq),
        in_specs=[
            pl.BlockSpec((1, s, 2 * LANES), lambda bi, p, qi: (bi, 0, p)),
            pl.BlockSpec((1, nk, LANES, tk), lambda bi, p, qi: (bi, 0, p, 0)),
            pl.BlockSpec((1, LANES, tq), lambda bi, p, qi: (bi, p, qi)),
            pl.BlockSpec((1, 2 * AUG_SLOTS_PER_HEAD, tq), lambda bi, p, qi: (bi, p, qi)),
        ],
        out_specs=pl.BlockSpec((1, LANES, tq), lambda bi, p, qi: (bi, p, qi)),
        out_shape=jax.ShapeDtypeStruct((b, width, s), F32),
        scratch_shapes=[pltpu.VMEM((2, 2 * (tq // QUERY_LANES), tk, QUERY_LANES), F32),
                        pltpu.VMEM((2 * (tq // QUERY_LANES), tk, QUERY_LANES), BF16)],
        compiler_params=_cparams(3),
        name="fox_attn",
    )(kaug, vt, qt, augq)


def _ple(h1, p, wup_ref, wgate_ref):
    gate = jax.nn.sigmoid(_dot(h1.astype(BF16), wgate_ref[...]))
    up = _dot(p.astype(BF16), wup_ref[...])
    return h1 + up * gate


def _fox_out_kernel(ot_ref, zt_ref, x_ref, p_ref, wout_ref, wup_ref, wgate_ref, h_ref):
    z = zt_ref[0].astype(F32)
    gt = (ot_ref[0] * (z * jax.nn.sigmoid(z))).astype(BF16)
    y = lax.dot_general(gt, wout_ref[...], TN_DIMS, preferred_element_type=F32)
    h_ref[0] = _ple(x_ref[0] + y, p_ref[0], wup_ref, wgate_ref)


def _fox_out(ot, zt, x, p, w_out, w_up, w_gate, *, tm):
    b, s, d = x.shape
    width = ot.shape[1]
    pd = p.shape[-1]
    const = lambda shape: pl.BlockSpec(shape, lambda bi, i: (0,) * len(shape))
    return pl.pallas_call(
        _fox_out_kernel,
        grid=(b, s // tm),
        in_specs=[
            pl.BlockSpec((1, width, tm), lambda bi, i: (bi, 0, i)),
            pl.BlockSpec((1, width, tm), lambda bi, i: (bi, 0, i)),
            pl.BlockSpec((1, tm, d), lambda bi, i: (bi, i, 0)),
            pl.BlockSpec((1, tm, pd), lambda bi, i: (bi, i, 0)),
            const((width, d)), const((pd, d)), const((d, d)),
        ],
        out_specs=pl.BlockSpec((1, tm, d), lambda bi, i: (bi, i, 0)),
        out_shape=jax.ShapeDtypeStruct((b, s, d), F32),
        compiler_params=_cparams(2),
        name="fox_out",
    )(ot, zt, x, p, w_out.astype(BF16), w_up.astype(BF16), w_gate.astype(BF16))


PERM_ROWS = 256


def _phase_major_matrix(dilation):
    n = PERM_ROWS // dilation
    p = np.zeros((PERM_ROWS, PERM_ROWS), np.float32)
    for r in range(dilation):
        for i in range(n):
            p[r * n + i, dilation * i + r] = 1.0
    return p


def _dil_in_kernel(h_ref, g_ref, w_ref, *rest, dilations, tm):
    n_perm = sum(d > 1 for d in dilations)
    perm_refs, out_refs = rest[:n_perm], rest[n_perm:]
    hn = _rms_norm(h_ref[0], g_ref[...]).astype(BF16)
    cols = w_ref.shape[1] // len(dilations)
    perm_refs = list(perm_refs)
    for g, (dil, o_ref) in enumerate(zip(dilations, out_refs)):
        w = w_ref[:, cols * g:cols * (g + 1)]
        if dil == 1:
            o_ref[0] = _dot(hn, w).astype(BF16)
            continue
        perm = perm_refs.pop(0)[...]
        n = PERM_ROWS // dil
        for c in range(tm // PERM_ROWS):
            rows = slice(PERM_ROWS * c, PERM_ROWS * (c + 1))
            hp = _dot(perm, hn[rows]).astype(BF16)
            res = _dot(hp, w).astype(BF16)
            for r in range(dil):
                o_ref[0, n * c:n * (c + 1), cols * r:cols * (r + 1)] = res[n * r:n * (r + 1)]


def _dil_in(h, norm_g, w, dilations, *, tm):
    b, s, d = h.shape
    cols = w.shape[1] // len(dilations)
    perms = [jnp.asarray(_phase_major_matrix(dil), BF16) for dil in dilations if dil > 1]
    const = lambda shape: pl.BlockSpec(shape, lambda bi, i: (0,) * len(shape))
    return pl.pallas_call(
        functools.partial(_dil_in_kernel, dilations=dilations, tm=tm),
        grid=(b, s // tm),
        in_specs=[pl.BlockSpec((1, tm, d), lambda bi, i: (bi, i, 0)), const((1, d)),
                  const(w.shape)] + [const(p.shape) for p in perms],
        out_specs=[pl.BlockSpec((1, tm // dil, dil * cols), lambda bi, i: (bi, i, 0))
                   for dil in dilations],
        out_shape=[jax.ShapeDtypeStruct((b, s // dil, dil * cols), BF16) for dil in dilations],
        compiler_params=_cparams(2),
        name="dil_in_" + "_".join(str(dil) for dil in dilations),
    )(h, norm_g.reshape(1, d), w.astype(BF16), *perms)


def _dil_attn_kernel(q_ref, kc_ref, vc_ref, kp_ref, vp_ref, o_ref, st_ref, *,
                     dilation, kb, neg_slopes):
    n = pl.program_id(2)
    nw = DIL_WINDOW_STEPS
    hd = DIL_HEAD_DIM
    scale = hd ** -0.5
    row = lax.broadcasted_iota(jnp.int32, (nw, 2 * nw), 0)
    col = lax.broadcasted_iota(jnp.int32, (nw, 2 * nw), 1)
    dist = nw + row - col
    band = (dist >= 0) & (dist <= nw)
    dist_f = (dist * dilation).astype(F32)
    lane = lax.broadcasted_iota(jnp.int32, (nw, LANES), 1)

    for jb in range(kb):
        if jb == 0:
            kk = jnp.concatenate([kp_ref[0], kc_ref[0, 0:nw]], axis=0)
            vv = jnp.concatenate([vp_ref[0], vc_ref[0, 0:nw]], axis=0)
            valid = band & (col >= jnp.where(n > 0, 0, nw))
        else:
            kk = kc_ref[0, nw * (jb - 1):nw * (jb + 1)]
            vv = vc_ref[0, nw * (jb - 1):nw * (jb + 1)]
            valid = band
        stats = jnp.zeros((nw, LANES), F32)
        for h in range(DIL_HEADS_PER_GROUP):
            hs = slice(hd * h, hd * (h + 1))
            q = q_ref[0, nw * jb:nw * (jb + 1), hs]
            s = lax.dot_general(q, kk[:, hs], NT_DIMS, preferred_element_type=F32)
            s = s * scale + neg_slopes[h] * dist_f
            s = jnp.where(valid, s, MASK_VALUE)
            m = jnp.max(s, axis=1, keepdims=True)
            p = jnp.exp(s - m)
            l = jnp.sum(p, axis=1, keepdims=True)
            o = _dot(p.astype(BF16), vv[:, hs])
            o_ref[0, nw * jb:nw * (jb + 1), hs] = (o / l).astype(BF16)
            hi, mid, lo = _split3(m + jnp.log(l))
            stats = jnp.where(lane == h, hi, stats)
            stats = jnp.where(lane == h + STAT_PIECE_LANES, mid, stats)
            stats = jnp.where(lane == h + 2 * STAT_PIECE_LANES, lo, stats)
        st_ref[0, nw * jb:nw * (jb + 1), :] = stats.astype(BF16)


def _alibi_neg_slopes(group):
    n = len(DIL_PATTERN) * DIL_HEADS_PER_GROUP
    k = np.arange(1, n + 1, dtype=np.float32)
    slopes = np.float32(2.0) ** (np.float32(-ALIBI_MAX_EXP) * k / np.float32(n))
    lo = group * DIL_HEADS_PER_GROUP
    return tuple(float(-v) for v in slopes[lo:lo + DIL_HEADS_PER_GROUP])


def _dil_attn(view, group, *, kb):
    _, dilation = DIL_PATTERN[group]
    b, length, width = view.shape
    nw = DIL_WINDOW_STEPS
    gw = DIL_HEADS_PER_GROUP * DIL_HEAD_DIM
    ncb = width // dilation // gw
    rows = nw * kb
    qc, kc, vc = 0, 1, 2

    def cur(cb):
        return pl.BlockSpec((1, rows, gw), lambda bi, r, n: (bi, n, r * ncb + cb))

    def prev(cb):
        return pl.BlockSpec((1, nw, gw),
                            lambda bi, r, n: (bi, jnp.maximum(n * kb - 1, 0), r * ncb + cb))

    o, st = pl.pallas_call(
        functools.partial(_dil_attn_kernel, dilation=dilation, kb=kb,
                          neg_slopes=_alibi_neg_slopes(group)),
        grid=(b, dilation, length // rows),
        in_specs=[cur(qc), cur(kc), cur(vc), prev(kc), prev(vc)],
        out_specs=[
            pl.BlockSpec((1, rows, gw), lambda bi, r, n: (bi, n, r)),
            pl.BlockSpec((1, rows, LANES), lambda bi, r, n: (bi, n, r)),
        ],
        out_shape=[
            jax.ShapeDtypeStruct((b, length, dilation * gw), BF16),
            jax.ShapeDtypeStruct((b, length, dilation * LANES), BF16),
        ],
        compiler_params=_cparams(3),
        name=f"dil_attn_g{group}",
    )(view, view, view, view, view)
    return o, st


def _natural_rows(view_ref, perm_t, dilation, tm):
    cols = view_ref.shape[2] // dilation
    if dilation == 1:
        return view_ref[0]
    n = PERM_ROWS // dilation
    chunks = []
    for c in range(tm // PERM_ROWS):
        phase_major = jnp.concatenate(
            [view_ref[0, n * c:n * (c + 1), cols * r:cols * (r + 1)] for r in range(dilation)],
            axis=0)
        chunks.append(_dot(perm_t, phase_major))
    return jnp.concatenate(chunks, axis=0)


def _dil_out_kernel(o0_ref, o1_ref, o2_ref, s0_ref, s1_ref, s2_ref, z_ref, h_ref, p_ref,
                    pt1_ref, pt2_ref, fold_ref, e_ref, wout_ref, wup_ref, wgate_ref, g_ref,
                    out_ref, *, dilations, tm):
    perm_ts = (None, pt1_ref[...], pt2_ref[...])
    lse = []
    for st_ref, perm_t, dil in zip((s0_ref, s1_ref, s2_ref), perm_ts, dilations):
        pieces = _natural_rows(st_ref, perm_t, dil, tm).astype(BF16)
        lse.append(_dot(pieces, fold_ref[...]))
    m = jnp.maximum(jnp.maximum(lse[0], lse[1]), lse[2])
    e = [jnp.exp(v - m) for v in lse]
    inv = 1.0 / (e[0] + e[1] + e[2])
    o = None
    for ev, o_ref, perm_t, dil in zip(e, (o0_ref, o1_ref, o2_ref), perm_ts, dilations):
        w = ev * inv
        w_hi = w.astype(BF16)
        w_lo = (w - w_hi.astype(F32)).astype(BF16)
        w_full = _dot(w_hi, e_ref[...]) + _dot(w_lo, e_ref[...])
        term = w_full * _natural_rows(o_ref, perm_t, dil, tm)
        o = term if o is None else o + term
    z = z_ref[0].astype(F32)
    g = (o * (z * jax.nn.sigmoid(z))).astype(BF16)
    h1 = h_ref[0] + _dot(g, wout_ref[...])
    out_ref[0] = _rms_norm(_ple(h1, p_ref[0], wup_ref, wgate_ref), g_ref[...])


def _dil_out(outs, stats, proj0, h, p, w_out, w_up, w_gate, final_g, *, tm):
    b, s, d = h.shape
    gw = DIL_HEADS_PER_GROUP * DIL_HEAD_DIM
    pd = p.shape[-1]
    dilations = tuple(dil for _, dil in DIL_PATTERN)
    z_block = proj0.shape[-1] // gw - 1
    expand = np.zeros((LANES, gw), np.float32)
    fold = np.zeros((LANES, LANES), np.float32)
    for hh in range(DIL_HEADS_PER_GROUP):
        expand[hh, DIL_HEAD_DIM * hh:DIL_HEAD_DIM * (hh + 1)] = 1.0
        for piece in range(3):
            fold[hh + STAT_PIECE_LANES * piece, hh] = 1.0
    perm_ts = [jnp.asarray(_phase_major_matrix(dil).T, BF16) for dil in dilations[1:]]
    const = lambda shape: pl.BlockSpec(shape, lambda bi, i: (0,) * len(shape))
    row = lambda w: pl.BlockSpec((1, tm, w), lambda bi, i: (bi, i, 0))
    view = lambda w: [pl.BlockSpec((1, tm // dil, dil * w), lambda bi, i: (bi, i, 0))
                      for dil in dilations]
    return pl.pallas_call(
        functools.partial(_dil_out_kernel, dilations=dilations, tm=tm),
        grid=(b, s // tm),
        in_specs=view(gw) + view(LANES) + [
            pl.BlockSpec((1, tm, gw), lambda bi, i: (bi, i, z_block)),
            row(d), row(pd),
            const((PERM_ROWS, PERM_ROWS)), const((PERM_ROWS, PERM_ROWS)), const((LANES, LANES)),
            const((LANES, gw)), const((gw, d)), const((pd, d)), const((d, d)), const((1, d)),
        ],
        out_specs=row(d),
        out_shape=jax.ShapeDtypeStruct((b, s, d), F32),
        compiler_params=_cparams(2),
        name="dil_out",
    )(*outs, *stats, proj0, h, p, *perm_ts, jnp.asarray(fold, BF16), jnp.asarray(expand, BF16),
      w_out.astype(BF16), w_up.astype(BF16), w_gate.astype(BF16), final_g.reshape(1, d))


def kernel(x, p, fox_norm, fox_w_in, fox_b_f, fox_w_out, dil_norm, dil_w_in, dil_w_out,
           ple_w_up, ple_w_gate, final_norm):
    s = x.shape[1]
    tm = min(512, s)
    tq = min(512, s)
    tk = min(512, s)
    kaug, qt, vt, zt, augq = _fox_in(x, fox_norm[0], fox_w_in[0], fox_b_f[0], tm=tm, ch=tk)
    ot = _fox_attn(kaug, qt, vt, augq, tq=tq, tk=tk)
    h = _fox_out(ot, zt, x, p[0], fox_w_out[0], ple_w_up[0], ple_w_gate[0], tm=tm)

    gw = DIL_HEADS_PER_GROUP * DIL_HEAD_DIM
    n_groups = len(DIL_PATTERN)
    w1 = dil_w_in[0]
    qkv = lambda g: [w1[:, (n_groups * part + g) * gw:(n_groups * part + g + 1) * gw]
                     for part in range(3)]
    w_natural = jnp.concatenate(qkv(0) + [w1[:, 3 * n_groups * gw:]], axis=1)
    w_phased = jnp.concatenate(qkv(1) + qkv(2), axis=1)
    (proj0,) = _dil_in(h, dil_norm[0], w_natural, (1,), tm=tm)
    views = [proj0] + list(_dil_in(h, dil_norm[0], w_phased,
                                   (DIL_PATTERN[1][1], DIL_PATTERN[2][1]), tm=min(PERM_ROWS, s)))
    outs, stats = [], []
    for group, (_, dilation) in enumerate(DIL_PATTERN):
        kb = min(4, s // dilation // DIL_WINDOW_STEPS)
        o, st = _dil_attn(views[group], group, kb=kb)
        outs.append(o)
        stats.append(st)
    return _dil_out(outs, stats, proj0, h, p[1], dil_w_out[0], ple_w_up[1], ple_w_gate[1],
                    final_norm, tm=tm)
```

```python
import functools

import numpy as np
import jax
import jax.numpy as jnp
from jax import lax
from jax.experimental import pallas as pl
from jax.experimental.pallas import tpu as pltpu

F32 = jnp.float32
BF16 = jnp.bfloat16

RMS_EPS = 1e-6
FOX_HEADS = 16
FOX_HEAD_DIM = 64
FOX_PAIRS = FOX_HEADS // 2
DIL_PATTERN = ((128, 1), (512, 4), (2048, 16))
DIL_HEADS_PER_GROUP = 8
DIL_HEAD_DIM = 128
DIL_WINDOW_STEPS = 128
ALIBI_MAX_EXP = 8.0
MASK_VALUE = -1e30
LOG2E = 1.4426950408889634

LANES = 128
AUG_SLOTS_PER_HEAD = 16
QUERY_LANES = 256
PRUNE_LOG2 = 50.0
NORM_SLACK = 1.02
VMEM_LIMIT_BYTES = 56 * 1024 * 1024

NT_DIMS = (((1,), (1,)), ((), ()))
TN_DIMS = (((0,), (0,)), ((), ()))


def _cparams(n_axes):
    return pltpu.CompilerParams(
        dimension_semantics=("arbitrary",) * n_axes,
        vmem_limit_bytes=VMEM_LIMIT_BYTES,
    )


def _rms_norm(x, g):
    ms = jnp.mean(x * x, axis=-1, keepdims=True)
    return x * lax.rsqrt(ms + RMS_EPS) * g


def _log_sigmoid(x):
    return jnp.minimum(x, 0.0) - jnp.log1p(jnp.exp(-jnp.abs(x)))


def _split3(x):
    hi = x.astype(BF16).astype(F32)
    r1 = x - hi
    mid = r1.astype(BF16).astype(F32)
    lo = (r1 - mid).astype(BF16).astype(F32)
    return hi, mid, lo


def _dot(a, b):
    return jnp.dot(a, b, preferred_element_type=F32)


def _fox_in_kernel(x_ref, g_ref, wnat_ref, wt_ref, wft_ref, bnat_ref, bt_ref, tril_ref,
                   triu_ref, pk_ref, pqt_ref, gk_ref, gq_ref,
                   kaug_ref, qt_ref, vt_ref, zt_ref, augq_ref, kst_ref, qst_ref,
                   carry_ref, carryt_ref, *, tm, ch):
    i = pl.program_id(1)

    @pl.when(i == 0)
    def _():
        carry_ref[...] = jnp.zeros_like(carry_ref)
        carryt_ref[...] = jnp.zeros_like(carryt_ref)

    hn = _rms_norm(x_ref[0], g_ref[...]).astype(BF16)
    width = FOX_HEADS * FOX_HEAD_DIM

    nat = _dot(hn, wnat_ref[...])
    k = nat[:, :width]
    logf = _log_sigmoid(nat[:, width:] + bnat_ref[...])
    hi, mid, lo = _split3(logf)
    tril = tril_ref[...]
    c = (_dot(tril, hi.astype(BF16)) + _dot(tril, mid.astype(BF16))
         + _dot(tril, lo.astype(BF16)) + carry_ref[...])
    carry_ref[...] = c[tm - 1:tm, :]
    nhi, nmid, nlo = _split3(c * -LOG2E)
    lane = lax.broadcasted_iota(jnp.int32, (tm, LANES), 1)
    slots = jnp.where(lane < 16, nhi,
                      jnp.where(lane < 32, pltpu.roll(nmid, 16, 1),
                                jnp.where(lane < 48, pltpu.roll(nlo, 32, 1),
                                          jnp.where(lane == 48, 1.0, 0.0))))
    kaug = _dot(slots.astype(BF16), pk_ref[...])
    kb = k.astype(BF16)
    for p in range(FOX_PAIRS):
        kaug_ref[0, :, 2 * LANES * p:2 * LANES * p + LANES] = kb[:, LANES * p:LANES * (p + 1)]
        kaug_ref[0, :, 2 * LANES * p + LANES:2 * LANES * (p + 1)] = (
            kaug[:, LANES * p:LANES * (p + 1)].astype(BF16))
    kf = kb.astype(F32)
    kn2 = jnp.max(_dot((kf * kf).astype(BF16), gk_ref[...]), axis=0, keepdims=True)
    kst_ref[0, 0] = jnp.concatenate(
        [kn2, c[0:1, :], c[tm - 1:tm, :], jnp.zeros((5, LANES), F32)], axis=0)

    rows = 512
    for c0 in range(0, 3 * width, rows):
        res = lax.dot_general(wt_ref[c0:c0 + rows, :], hn, NT_DIMS,
                              preferred_element_type=F32).astype(BF16)
        which, off = divmod(c0, width)
        if which == 0:
            qt_ref[0, off:off + rows, :] = res
        elif which == 1:
            for j in range(tm // ch):
                vt_ref[0, j, off:off + rows, :] = res[:, j * ch:(j + 1) * ch]
        else:
            zt_ref[0, off:off + rows, :] = res

    ft = lax.dot_general(wft_ref[...], hn, NT_DIMS, preferred_element_type=F32)
    logft = _log_sigmoid(ft + bt_ref[...])
    hi, mid, lo = _split3(logft)
    stack = jnp.concatenate([hi, mid, lo], axis=0).astype(BF16)
    cs = _dot(stack, triu_ref[...])
    ct = cs[0:16] + cs[16:32] + cs[32:48] + carryt_ref[:, 0:1]
    carryt_ref[...] = jnp.broadcast_to(ct[:, tm - 1:tm], carryt_ref.shape)
    hi, mid, lo = _split3(ct * LOG2E)
    ones = jnp.where(lax.broadcasted_iota(jnp.int32, (16, tm), 0) == 0, 1.0, 0.0)
    slots_t = jnp.concatenate([hi, mid, lo, ones], axis=0).astype(BF16)
    augq_ref[0] = _dot(pqt_ref[...], slots_t).astype(BF16)
    qf = qt_ref[0].astype(F32)
    qn2 = jnp.max(_dot(gq_ref[...], (qf * qf).astype(BF16)), axis=1, keepdims=True)
    qst_ref[0, 0] = jnp.broadcast_to(qn2, (FOX_HEADS, LANES))


def _placement_matrices():
    pk = np.zeros((LANES, FOX_PAIRS * LANES), np.float32)
    pqt = np.zeros((FOX_HEADS * AUG_SLOTS_PER_HEAD, 64), np.float32)
    for h in range(FOX_HEADS):
        p, odd = divmod(h, 2)
        base = 6 * odd
        for piece in range(3):
            pk[48, LANES * p + base + piece] = 1.0
            pk[16 * piece + h, LANES * p + base + 3 + piece] = 1.0
            pqt[AUG_SLOTS_PER_HEAD * h + base + piece, 16 * piece + h] = 1.0
            pqt[AUG_SLOTS_PER_HEAD * h + base + 3 + piece, 48] = 1.0
    return jnp.asarray(pk, BF16), jnp.asarray(pqt, BF16)


def _fox_in(x, norm_g, w_in, b_f, *, tm, ch):
    b, s, d = x.shape
    width = FOX_HEADS * FOX_HEAD_DIM
    wq = w_in[:, :width] * (FOX_HEAD_DIM ** -0.5 * LOG2E)
    wk = w_in[:, width:2 * width]
    wv = w_in[:, 2 * width:3 * width]
    wz = w_in[:, 3 * width:4 * width]
    wf = w_in[:, 4 * width:]
    wnat = jnp.concatenate([wk, jnp.pad(wf, ((0, 0), (0, LANES - FOX_HEADS)))], axis=1).astype(BF16)
    wt = jnp.concatenate([wq, wv, wz], axis=1).T.astype(BF16)
    wft = wf.T.astype(BF16)
    bnat = jnp.pad(b_f, (0, LANES - FOX_HEADS)).reshape(1, LANES)
    bt = b_f.reshape(FOX_HEADS, 1)
    r = np.arange(tm)
    tril = jnp.asarray(r[None, :] <= r[:, None], BF16)
    triu = jnp.asarray(r[:, None] <= r[None, :], BF16)
    pk, pqt = _placement_matrices()
    head_of = np.arange(width) // FOX_HEAD_DIM
    gk = jnp.asarray(head_of[:, None] == np.arange(LANES)[None, :], BF16)
    gq = jnp.asarray(np.arange(FOX_HEADS)[:, None] == head_of[None, :], BF16)

    const = lambda shape: pl.BlockSpec(shape, lambda bi, i: (0,) * len(shape))
    return pl.pallas_call(
        functools.partial(_fox_in_kernel, tm=tm, ch=ch),
        grid=(b, s // tm),
        in_specs=[
            pl.BlockSpec((1, tm, d), lambda bi, i: (bi, i, 0)),
            const((1, d)),
            const(wnat.shape), const(wt.shape), const(wft.shape),
            const((1, LANES)), const((FOX_HEADS, 1)),
            const((tm, tm)), const((tm, tm)),
            const(pk.shape), const(pqt.shape), const(gk.shape), const(gq.shape),
        ],
        out_specs=[
            pl.BlockSpec((1, tm, 2 * width), lambda bi, i: (bi, i, 0)),
            pl.BlockSpec((1, width, tm), lambda bi, i: (bi, 0, i)),
            pl.BlockSpec((1, tm // ch, width, ch), lambda bi, i: (bi, i, 0, 0)),
            pl.BlockSpec((1, width, tm), lambda bi, i: (bi, 0, i)),
            pl.BlockSpec((1, FOX_HEADS * AUG_SLOTS_PER_HEAD, tm), lambda bi, i: (bi, 0, i)),
            pl.BlockSpec((1, 1, 8, LANES), lambda bi, i: (bi, i, 0, 0)),
            pl.BlockSpec((1, 1, FOX_HEADS, LANES), lambda bi, i: (bi, i, 0, 0)),
        ],
        out_shape=[
            jax.ShapeDtypeStruct((b, s, 2 * width), BF16),
            jax.ShapeDtypeStruct((b, width, s), BF16),
            jax.ShapeDtypeStruct((b, s // ch, width, ch), BF16),
            jax.ShapeDtypeStruct((b, width, s), BF16),
            jax.ShapeDtypeStruct((b, FOX_HEADS * AUG_SLOTS_PER_HEAD, s), BF16),
            jax.ShapeDtypeStruct((b, s // tm, 8, LANES), F32),
            jax.ShapeDtypeStruct((b, s // tm, FOX_HEADS, LANES), F32),
        ],
        scratch_shapes=[pltpu.VMEM((1, LANES), F32), pltpu.VMEM((FOX_HEADS, LANES), F32)],
        compiler_params=_cparams(2),
        name="fox_in",
    )(x, norm_g.reshape(1, d), wnat, wt, wft, bnat, bt, tril, triu, pk, pqt, gk, gq)


def _fox_attn_kernel(kstart_ref, kaug_ref, vt_ref, qt_ref, augq_ref, ot_ref, s_scr, p_scr, *, tq, tk):
    qi = pl.program_id(2)
    hd = FOX_HEAD_DIM
    qt = qt_ref[0]
    aug = augq_ref[0]
    zeros_hd = jnp.zeros((hd, tq), BF16)
    zeros_tail = jnp.zeros((LANES - AUG_SLOTS_PER_HEAD, tq), BF16)
    w_heads = (
        jnp.concatenate([qt[0:hd], zeros_hd, aug[0:16], zeros_tail], axis=0),
        jnp.concatenate([zeros_hd, qt[hd:2 * hd], aug[16:32], zeros_tail], axis=0),
    )
    ones_rows = jnp.where(lax.broadcasted_iota(jnp.int32, (16, tk), 0) == 0, 1.0, 0.0).astype(BF16)
    n_blocks = (qi * tq) // tk + 1
    k_first = kstart_ref[pl.program_id(0), pl.program_id(1), qi]
    chains = [(h, sub) for h in range(2) for sub in range(tq // QUERY_LANES)]

    def issue_scores(kj, slot):
        ks = pl.multiple_of(kj * tk, tk)
        kb = kaug_ref[0, pl.ds(ks, tk), :]
        block_max = []
        for c, (h, sub) in enumerate(chains):
            s = _dot(kb, w_heads[h][:, QUERY_LANES * sub:QUERY_LANES * (sub + 1)])
            s_scr[slot, c] = s
            block_max.append(jnp.max(s, axis=0, keepdims=True))
        return tuple(block_max)

    def pv_prev(kj_prev, c):
        h = chains[c][0]
        v_aug = jnp.concatenate([vt_ref[0, kj_prev, hd * h:hd * (h + 1), :], ones_rows], axis=0)
        return _dot(v_aug, p_scr[c])

    def step(kj, slot, state, masked):
        carry, block_max = state
        next_max = None if masked else issue_scores(kj + 1, 1 - slot)
        out = []
        for c, (h, sub) in enumerate(chains):
            m_old, alpha_prev, acc = carry[c]
            pv = pv_prev(jnp.maximum(kj - 1, 0), c)
            s = s_scr[slot, c]
            if masked:
                key = kj * tk + lax.broadcasted_iota(jnp.int32, (tk, QUERY_LANES), 0)
                qry = (qi * tq + QUERY_LANES * sub
                       + lax.broadcasted_iota(jnp.int32, (tk, QUERY_LANES), 1))
                s = jnp.where(key <= qry, s, MASK_VALUE)
                m_new = jnp.maximum(m_old, jnp.max(s, axis=0, keepdims=True))
            else:
                m_new = jnp.maximum(m_old, block_max[c])
            alpha = jnp.exp2(m_old - m_new)
            p_scr[c] = jnp.exp2(s - m_new).astype(BF16)
            out.append((m_new, alpha, acc * alpha_prev + pv))
        return tuple(out), next_max

    def pair(i, state):
        kj = k_first + 2 * i
        return step(kj + 1, 1, step(kj, 0, state, False), False)

    def finish(state):
        carry, _ = state
        for c, (h, sub) in enumerate(chains):
            _, alpha, acc = carry[c]
            acc = acc * alpha + pv_prev(n_blocks - 1, c)
            ot_ref[0, hd * h:hd * (h + 1), QUERY_LANES * sub:QUERY_LANES * (sub + 1)] = (
                acc[0:hd] / acc[hd:hd + 1])

    p_scr[...] = jnp.zeros_like(p_scr)
    first_max = issue_scores(k_first, 0)
    init = tuple((jnp.full((1, QUERY_LANES), MASK_VALUE, F32), jnp.ones((1, QUERY_LANES), F32),
                  jnp.zeros((hd + 16, QUERY_LANES), F32)) for _ in chains)
    n_unmasked = n_blocks - 1 - k_first
    state = lax.fori_loop(0, n_unmasked // 2, pair, (init, first_max))

    odd = lax.rem(n_unmasked, 2)
    state = lax.fori_loop(0, odd, lambda _, st: step(n_blocks - 2, 0, st, False), state)
    finish(step(n_blocks - 1, odd, state, True))


def _fox_prune_table(kst, qst):
    kn = jnp.sqrt(kst[:, :, 0, :FOX_HEADS]) * NORM_SLACK
    qn = jnp.sqrt(qst[:, :, :, 0]) * NORM_SLACK
    c_first = kst[:, :, 1, :FOX_HEADS] * LOG2E
    c_last = kst[:, :, 2, :FOX_HEADS] * LOG2E
    bound = (qn[:, :, None, :] * kn[:, None, :, :] + c_first[:, :, None, :] - c_last[:, None, :, :]
             + (qn * kn)[:, :, None, :])
    n = kst.shape[1]
    earlier = jnp.arange(n)[None, :, None] < jnp.arange(n)[:, None, None]
    skip = (bound < -PRUNE_LOG2) & earlier[None]
    lead = jnp.sum(jnp.cumprod(skip.astype(jnp.int32), axis=2), axis=2)
    lead = jnp.min(lead.reshape(lead.shape[0], n, FOX_PAIRS, 2), axis=3)
    return jnp.transpose(lead, (0, 2, 1)).astype(jnp.int32)


def _fox_attn(kaug, qt, vt, augq, kstart, *, tq, tk):
    b, s, _ = kaug.shape
    width = FOX_HEADS * FOX_HEAD_DIM
    nk = s // tk
    grid_spec = pltpu.PrefetchScalarGridSpec(
        num_scalar_prefetch=1,
        grid=(b, FOX_PAIRS, s // tq),
        in_specs=[
            pl.BlockSpec((1, s, 2 * LANES), lambda bi, p, qi, ks: (bi, 0, p)),
            pl.BlockSpec((1, nk, LANES, tk), lambda bi, p, qi, ks: (bi, 0, p, 0)),
            pl.BlockSpec((1, LANES, tq), lambda bi, p, qi, ks: (bi, p, qi)),
            pl.BlockSpec((1, 2 * AUG_SLOTS_PER_HEAD, tq), lambda bi, p, qi, ks: (bi, p, qi)),
        ],
        out_specs=pl.BlockSpec((1, LANES, tq), lambda bi, p, qi, ks: (bi, p, qi)),
        scratch_shapes=[pltpu.VMEM((2, 2 * (tq // QUERY_LANES), tk, QUERY_LANES), F32),
                        pltpu.VMEM((2 * (tq // QUERY_LANES), tk, QUERY_LANES), BF16)],
    )
    return pl.pallas_call(
        functools.partial(_fox_attn_kernel, tq=tq, tk=tk),
        grid_spec=grid_spec,
        out_shape=jax.ShapeDtypeStruct((b, width, s), F32),
        compiler_params=_cparams(3),
        name="fox_attn",
    )(kstart, kaug, vt, qt, augq)


def _ple(h1, p, wup_ref, wgate_ref):
    gate = jax.nn.sigmoid(_dot(h1.astype(BF16), wgate_ref[...]))
    up = _dot(p.astype(BF16), wup_ref[...])
    return h1 + up * gate


def _fox_out_kernel(ot_ref, zt_ref, x_ref, p_ref, wout_ref, wup_ref, wgate_ref, h_ref):
    z = zt_ref[0].astype(F32)
    gt = (ot_ref[0] * (z * jax.nn.sigmoid(z))).astype(BF16)
    y = lax.dot_general(gt, wout_ref[...], TN_DIMS, preferred_element_type=F32)
    h_ref[0] = _ple(x_ref[0] + y, p_ref[0], wup_ref, wgate_ref)


def _fox_out(ot, zt, x, p, w_out, w_up, w_gate, *, tm):
    b, s, d = x.shape
    width = ot.shape[1]
    pd = p.shape[-1]
    const = lambda shape: pl.BlockSpec(shape, lambda bi, i: (0,) * len(shape))
    return pl.pallas_call(
        _fox_out_kernel,
        grid=(b, s // tm),
        in_specs=[
            pl.BlockSpec((1, width, tm), lambda bi, i: (bi, 0, i)),
            pl.BlockSpec((1, width, tm), lambda bi, i: (bi, 0, i)),
            pl.BlockSpec((1, tm, d), lambda bi, i: (bi, i, 0)),
            pl.BlockSpec((1, tm, pd), lambda bi, i: (bi, i, 0)),
            const((width, d)), const((pd, d)), const((d, d)),
        ],
        out_specs=pl.BlockSpec((1, tm, d), lambda bi, i: (bi, i, 0)),
        out_shape=jax.ShapeDtypeStruct((b, s, d), F32),
        compiler_params=_cparams(2),
        name="fox_out",
    )(ot, zt, x, p, w_out.astype(BF16), w_up.astype(BF16), w_gate.astype(BF16))


PERM_ROWS = 256


def _phase_major_matrix(dilation):
    n = PERM_ROWS // dilation
    p = np.zeros((PERM_ROWS, PERM_ROWS), np.float32)
    for r in range(dilation):
        for i in range(n):
            p[r * n + i, dilation * i + r] = 1.0
    return p


def _dil_in_kernel(h_ref, g_ref, w_ref, *rest, dilations, tm):
    n_perm = sum(d > 1 for d in dilations)
    perm_refs, out_refs = rest[:n_perm], rest[n_perm:]
    hn = _rms_norm(h_ref[0], g_ref[...]).astype(BF16)
    cols = w_ref.shape[1] // len(dilations)
    perm_refs = list(perm_refs)
    for g, (dil, o_ref) in enumerate(zip(dilations, out_refs)):
        w = w_ref[:, cols * g:cols * (g + 1)]
        if dil == 1:
            o_ref[0] = _dot(hn, w).astype(BF16)
            continue
        perm = perm_refs.pop(0)[...]
        n = PERM_ROWS // dil
        for c in range(tm // PERM_ROWS):
            rows = slice(PERM_ROWS * c, PERM_ROWS * (c + 1))
            hp = _dot(perm, hn[rows]).astype(BF16)
            res = _dot(hp, w).astype(BF16)
            for r in range(dil):
                o_ref[0, n * c:n * (c + 1), cols * r:cols * (r + 1)] = res[n * r:n * (r + 1)]


def _dil_in(h, norm_g, w, dilations, *, tm):
    b, s, d = h.shape
    cols = w.shape[1] // len(dilations)
    perms = [jnp.asarray(_phase_major_matrix(dil), BF16) for dil in dilations if dil > 1]
    const = lambda shape: pl.BlockSpec(shape, lambda bi, i: (0,) * len(shape))
    return pl.pallas_call(
        functools.partial(_dil_in_kernel, dilations=dilations, tm=tm),
        grid=(b, s // tm),
        in_specs=[pl.BlockSpec((1, tm, d), lambda bi, i: (bi, i, 0)), const((1, d)),
                  const(w.shape)] + [const(p.shape) for p in perms],
        out_specs=[pl.BlockSpec((1, tm // dil, dil * cols), lambda bi, i: (bi, i, 0))
                   for dil in dilations],
        out_shape=[jax.ShapeDtypeStruct((b, s // dil, dil * cols), BF16) for dil in dilations],
        compiler_params=_cparams(2),
        name="dil_in_" + "_".join(str(dil) for dil in dilations),
    )(h, norm_g.reshape(1, d), w.astype(BF16), *perms)


def _dil_attn_kernel(q_ref, kc_ref, vc_ref, kp_ref, vp_ref, o_ref, st_ref,
                     kcat, vcat, bias_scr, *, dilation, kb, neg_slopes):
    n = pl.program_id(2)
    nw = DIL_WINDOW_STEPS
    hd = DIL_HEAD_DIM
    qk_scale = hd ** -0.5 * LOG2E

    @pl.when((pl.program_id(0) == 0) & (pl.program_id(1) == 0) & (n == 0))
    def _():
        row = lax.broadcasted_iota(jnp.int32, (nw, 2 * nw), 0)
        col = lax.broadcasted_iota(jnp.int32, (nw, 2 * nw), 1)
        dist = nw + row - col
        band = (dist >= 0) & (dist <= nw)
        dist_f = (dist * dilation).astype(F32)
        for h in range(DIL_HEADS_PER_GROUP):
            bias_scr[h] = jnp.where(band, neg_slopes[h] * dist_f * LOG2E, MASK_VALUE)

    kcat[0:nw] = kp_ref[0]
    kcat[nw:nw * (kb + 1)] = kc_ref[0]
    vcat[0:nw] = vp_ref[0]
    vcat[nw:nw * (kb + 1)] = vc_ref[0]
    lane = lax.broadcasted_iota(jnp.int32, (nw, LANES), 1)
    col = lax.broadcasted_iota(jnp.int32, (nw, 2 * nw), 1)
    no_history = jnp.where(col < jnp.where(n == 0, nw, 0), MASK_VALUE, 0.0)

    heads = [slice(hd * h, hd * (h + 1)) for h in range(DIL_HEADS_PER_GROUP)]
    scores = [[lax.dot_general(q_ref[0, nw * jb:nw * (jb + 1), hs],
                               kcat[nw * jb:nw * (jb + 2), hs], NT_DIMS,
                               preferred_element_type=F32) for hs in heads] for jb in range(kb)]
    for jb in range(kb):
        rows = slice(nw * jb, nw * (jb + 1))
        stat_tile = jnp.zeros((nw, LANES), F32)
        for h, hs in enumerate(heads):
            t = scores[jb][h] * qk_scale + bias_scr[h]
            if jb == 0:
                t = t + no_history
            m = jnp.max(t, axis=1, keepdims=True)
            p = jnp.exp2(t - m)
            l = jnp.sum(p, axis=1, keepdims=True)
            o_ref[0, rows, hs] = _dot(p.astype(BF16), vcat[nw * jb:nw * (jb + 2), hs]).astype(BF16)
            stat_tile = jnp.where(lane == h, m, stat_tile)
            stat_tile = jnp.where(lane == h + DIL_HEADS_PER_GROUP, l, stat_tile)
        st_ref[0, rows, :] = stat_tile


def _alibi_neg_slopes(group):
    n = len(DIL_PATTERN) * DIL_HEADS_PER_GROUP
    k = np.arange(1, n + 1, dtype=np.float32)
    slopes = np.float32(2.0) ** (np.float32(-ALIBI_MAX_EXP) * k / np.float32(n))
    lo = group * DIL_HEADS_PER_GROUP
    return tuple(float(-v) for v in slopes[lo:lo + DIL_HEADS_PER_GROUP])


def _dil_attn(view, group, *, kb):
    _, dilation = DIL_PATTERN[group]
    b, length, width = view.shape
    nw = DIL_WINDOW_STEPS
    gw = DIL_HEADS_PER_GROUP * DIL_HEAD_DIM
    ncb = width // dilation // gw
    rows = nw * kb
    qc, kc, vc = 0, 1, 2

    def cur(cb):
        return pl.BlockSpec((1, rows, gw), lambda bi, r, n: (bi, n, r * ncb + cb))

    def prev(cb):
        return pl.BlockSpec((1, nw, gw),
                            lambda bi, r, n: (bi, jnp.maximum(n * kb - 1, 0), r * ncb + cb))

    o, st = pl.pallas_call(
        functools.partial(_dil_attn_kernel, dilation=dilation, kb=kb,
                          neg_slopes=_alibi_neg_slopes(group)),
        grid=(b, dilation, length // rows),
        in_specs=[cur(qc), cur(kc), cur(vc), prev(kc), prev(vc)],
        out_specs=[
            pl.BlockSpec((1, rows, gw), lambda bi, r, n: (bi, n, r)),
            pl.BlockSpec((1, rows, LANES), lambda bi, r, n: (bi, n, r)),
        ],
        out_shape=[
            jax.ShapeDtypeStruct((b, length, dilation * gw), BF16),
            jax.ShapeDtypeStruct((b, length, dilation * LANES), F32),
        ],
        scratch_shapes=[pltpu.VMEM((nw * (kb + 1), gw), BF16), pltpu.VMEM((nw * (kb + 1), gw), BF16),
                        pltpu.VMEM((DIL_HEADS_PER_GROUP, nw, 2 * nw), F32)],
        compiler_params=_cparams(3),
        name=f"dil_attn_g{group}",
    )(view, view, view, view, view)
    return o, st


def _natural_rows(blk, perm_t, dilation, tm):
    cols = blk.shape[1] // dilation
    if dilation == 1:
        return blk
    n = PERM_ROWS // dilation
    chunks = []
    for c in range(tm // PERM_ROWS):
        phase_major = jnp.concatenate(
            [blk[n * c:n * (c + 1), cols * r:cols * (r + 1)] for r in range(dilation)], axis=0)
        chunks.append(_dot(perm_t, phase_major))
    return jnp.concatenate(chunks, axis=0)


def _dil_out_kernel(o0_ref, o1_ref, o2_ref, s0_ref, s1_ref, s2_ref, z_ref, h_ref, p_ref,
                    pt1_ref, pt2_ref, e_ref, wout_ref, wup_ref, wgate_ref, g_ref,
                    out_ref, *, dilations, tm):
    perm_ts = (None, pt1_ref[...], pt2_ref[...])
    n_heads = DIL_HEADS_PER_GROUP
    row_max, row_sum = [], []
    for st_ref, perm_t, dil in zip((s0_ref, s1_ref, s2_ref), perm_ts, dilations):
        if dil == 1:
            st = st_ref[0]
        else:
            hi, mid, lo = (_natural_rows(piece.astype(BF16), perm_t, dil, tm)
                           for piece in _split3(st_ref[0]))
            st = hi + mid + lo
        row_max.append(st)
        row_sum.append(pltpu.roll(st, LANES - n_heads, 1))
    m = jnp.maximum(jnp.maximum(row_max[0], row_max[1]), row_max[2])
    e = [jnp.exp2(v - m) for v in row_max]
    inv = 1.0 / (e[0] * row_sum[0] + e[1] * row_sum[1] + e[2] * row_sum[2])
    live = lax.broadcasted_iota(jnp.int32, (tm, LANES), 1) < n_heads
    o = None
    for ev, o_ref, perm_t, dil in zip(e, (o0_ref, o1_ref, o2_ref), perm_ts, dilations):
        w = jnp.where(live, ev * inv, 0.0)
        w_hi = w.astype(BF16)
        w_lo = (w - w_hi.astype(F32)).astype(BF16)
        w_full = _dot(jnp.concatenate([w_hi, w_lo], axis=1), e_ref[...])
        term = w_full * _natural_rows(o_ref[0], perm_t, dil, tm)
        o = term if o is None else o + term
    z = z_ref[0].astype(F32)
    g = (o * (z * jax.nn.sigmoid(z))).astype(BF16)
    h1 = h_ref[0] + _dot(g, wout_ref[...])
    out_ref[0] = _rms_norm(_ple(h1, p_ref[0], wup_ref, wgate_ref), g_ref[...])


def _dil_out(outs, stats, proj0, h, p, w_out, w_up, w_gate, final_g, *, tm):
    b, s, d = h.shape
    gw = DIL_HEADS_PER_GROUP * DIL_HEAD_DIM
    pd = p.shape[-1]
    dilations = tuple(dil for _, dil in DIL_PATTERN)
    z_block = proj0.shape[-1] // gw - 1
    expand = np.zeros((2 * LANES, gw), np.float32)
    for hh in range(DIL_HEADS_PER_GROUP):
        expand[hh, DIL_HEAD_DIM * hh:DIL_HEAD_DIM * (hh + 1)] = 1.0
        expand[LANES + hh, DIL_HEAD_DIM * hh:DIL_HEAD_DIM * (hh + 1)] = 1.0
    perm_ts = [jnp.asarray(_phase_major_matrix(dil).T, BF16) for dil in dilations[1:]]
    const = lambda shape: pl.BlockSpec(shape, lambda bi, i: (0,) * len(shape))
    row = lambda w: pl.BlockSpec((1, tm, w), lambda bi, i: (bi, i, 0))
    view = lambda w: [pl.BlockSpec((1, tm // dil, dil * w), lambda bi, i: (bi, i, 0))
                      for dil in dilations]
    return pl.pallas_call(
        functools.partial(_dil_out_kernel, dilations=dilations, tm=tm),
        grid=(b, s // tm),
        in_specs=view(gw) + view(LANES) + [
            pl.BlockSpec((1, tm, gw), lambda bi, i: (bi, i, z_block)),
            row(d), row(pd),
            const((PERM_ROWS, PERM_ROWS)), const((PERM_ROWS, PERM_ROWS)),
            const((2 * LANES, gw)), const((gw, d)), const((pd, d)), const((d, d)), const((1, d)),
        ],
        out_specs=row(d),
        out_shape=jax.ShapeDtypeStruct((b, s, d), F32),
        compiler_params=_cparams(2),
        name="dil_out",
    )(*outs, *stats, proj0, h, p, *perm_ts, jnp.asarray(expand, BF16),
      w_out.astype(BF16), w_up.astype(BF16), w_gate.astype(BF16), final_g.reshape(1, d))


def kernel(x, p, fox_norm, fox_w_in, fox_b_f, fox_w_out, dil_norm, dil_w_in, dil_w_out,
           ple_w_up, ple_w_gate, final_norm):
    s = x.shape[1]
    tm = min(512, s)
    tq = min(512, s)
    tk = min(512, s)
    assert tm == tq == tk, "the pruning table pairs fox_in tiles with query tiles and key blocks"
    kaug, qt, vt, zt, augq, kst, qst = _fox_in(x, fox_norm[0], fox_w_in[0], fox_b_f[0], tm=tm, ch=tk)
    ot = _fox_attn(kaug, qt, vt, augq, _fox_prune_table(kst, qst), tq=tq, tk=tk)
    h = _fox_out(ot, zt, x, p[0], fox_w_out[0], ple_w_up[0], ple_w_gate[0], tm=tm)

    gw = DIL_HEADS_PER_GROUP * DIL_HEAD_DIM
    n_groups = len(DIL_PATTERN)
    w1 = dil_w_in[0]
    qkv = lambda g: [w1[:, (n_groups * part + g) * gw:(n_groups * part + g + 1) * gw]
                     for part in range(3)]
    w_natural = jnp.concatenate(qkv(0) + [w1[:, 3 * n_groups * gw:]], axis=1)
    w_phased = jnp.concatenate(qkv(1) + qkv(2), axis=1)
    (proj0,) = _dil_in(h, dil_norm[0], w_natural, (1,), tm=tm)
    views = [proj0] + list(_dil_in(h, dil_norm[0], w_phased,
                                   (DIL_PATTERN[1][1], DIL_PATTERN[2][1]), tm=min(PERM_ROWS, s)))
    outs, stats = [], []
    for group, (_, dilation) in enumerate(DIL_PATTERN):
        kb = min(4, s // dilation // DIL_WINDOW_STEPS)
        o, st = _dil_attn(views[group], group, kb=kb)
        outs.append(o)
        stats.append(st)
    return _dil_out(outs, stats, proj0, h, p[1], dil_w_out[0], ple_w_up[1], ple_w_gate[1],
                    final_norm, tm=tm)
```

```python
import functools

import numpy as np
import jax
import jax.numpy as jnp
from jax import lax
from jax.experimental import pallas as pl
from jax.experimental.pallas import tpu as pltpu

F32 = jnp.float32
BF16 = jnp.bfloat16

RMS_EPS = 1e-6
FOX_HEADS = 16
FOX_HEAD_DIM = 64
FOX_PAIRS = FOX_HEADS // 2
DIL_PATTERN = ((128, 1), (512, 4), (2048, 16))
DIL_HEADS_PER_GROUP = 8
DIL_HEAD_DIM = 128
DIL_WINDOW_STEPS = 128
ALIBI_MAX_EXP = 8.0
MASK_VALUE = -1e30
LOG2E = 1.4426950408889634

LANES = 128
AUG_SLOTS_PER_HEAD = 16
QUERY_LANES = 256
PRUNE_LOG2 = 50.0
NORM_SLACK = 1.02
VMEM_LIMIT_BYTES = 56 * 1024 * 1024

NT_DIMS = (((1,), (1,)), ((), ()))
TN_DIMS = (((0,), (0,)), ((), ()))


def _cparams(n_axes):
    return pltpu.CompilerParams(
        dimension_semantics=("arbitrary",) * n_axes,
        vmem_limit_bytes=VMEM_LIMIT_BYTES,
    )


def _rms_norm(x, g):
    ms = jnp.mean(x * x, axis=-1, keepdims=True)
    return x * lax.rsqrt(ms + RMS_EPS) * g


def _log_sigmoid(x):
    return jnp.minimum(x, 0.0) - jnp.log1p(jnp.exp(-jnp.abs(x)))


def _split3(x):
    hi = x.astype(BF16).astype(F32)
    r1 = x - hi
    mid = r1.astype(BF16).astype(F32)
    lo = (r1 - mid).astype(BF16).astype(F32)
    return hi, mid, lo


def _dot(a, b):
    return jnp.dot(a, b, preferred_element_type=F32)


def _fox_in_kernel(x_ref, g_ref, wnat_ref, wt_ref, wft_ref, bnat_ref, bt_ref, tril_ref,
                   triu_ref, pk_ref, pqt_ref, gk_ref, gq_ref,
                   kaug_ref, qt_ref, vt_ref, zt_ref, augq_ref, kst_ref, qst_ref,
                   carry_ref, carryt_ref, *, tm, ch):
    i = pl.program_id(1)

    @pl.when(i == 0)
    def _():
        carry_ref[...] = jnp.zeros_like(carry_ref)
        carryt_ref[...] = jnp.zeros_like(carryt_ref)

    hn = _rms_norm(x_ref[0], g_ref[...]).astype(BF16)
    width = FOX_HEADS * FOX_HEAD_DIM

    nat = _dot(hn, wnat_ref[...])
    k = nat[:, :width]
    logf = _log_sigmoid(nat[:, width:] + bnat_ref[...])
    hi, mid, lo = _split3(logf)
    tril = tril_ref[...]
    c = (_dot(tril, hi.astype(BF16)) + _dot(tril, mid.astype(BF16))
         + _dot(tril, lo.astype(BF16)) + carry_ref[...])
    carry_ref[...] = c[tm - 1:tm, :]
    nhi, nmid, nlo = _split3(c * -LOG2E)
    lane = lax.broadcasted_iota(jnp.int32, (tm, LANES), 1)
    slots = jnp.where(lane < 16, nhi,
                      jnp.where(lane < 32, pltpu.roll(nmid, 16, 1),
                                jnp.where(lane < 48, pltpu.roll(nlo, 32, 1),
                                          jnp.where(lane == 48, 1.0, 0.0))))
    kaug = _dot(slots.astype(BF16), pk_ref[...])
    kb = k.astype(BF16)
    for p in range(FOX_PAIRS):
        kaug_ref[0, :, 2 * LANES * p:2 * LANES * p + LANES] = kb[:, LANES * p:LANES * (p + 1)]
        kaug_ref[0, :, 2 * LANES * p + LANES:2 * LANES * (p + 1)] = (
            kaug[:, LANES * p:LANES * (p + 1)].astype(BF16))
    kf = kb.astype(F32)
    kn2 = jnp.max(_dot((kf * kf).astype(BF16), gk_ref[...]), axis=0, keepdims=True)
    kst_ref[0, 0] = jnp.concatenate(
        [kn2, c[0:1, :], c[tm - 1:tm, :], jnp.zeros((5, LANES), F32)], axis=0)

    rows = 512
    for c0 in range(0, 3 * width, rows):
        res = lax.dot_general(wt_ref[c0:c0 + rows, :], hn, NT_DIMS,
                              preferred_element_type=F32).astype(BF16)
        which, off = divmod(c0, width)
        if which == 0:
            qt_ref[0, off:off + rows, :] = res
        elif which == 1:
            for j in range(tm // ch):
                vt_ref[0, j, off:off + rows, :] = res[:, j * ch:(j + 1) * ch]
        else:
            zt_ref[0, off:off + rows, :] = res

    ft = lax.dot_general(wft_ref[...], hn, NT_DIMS, preferred_element_type=F32)
    logft = _log_sigmoid(ft + bt_ref[...])
    hi, mid, lo = _split3(logft)
    stack = jnp.concatenate([hi, mid, lo], axis=0).astype(BF16)
    cs = _dot(stack, triu_ref[...])
    ct = cs[0:16] + cs[16:32] + cs[32:48] + carryt_ref[:, 0:1]
    carryt_ref[...] = jnp.broadcast_to(ct[:, tm - 1:tm], carryt_ref.shape)
    hi, mid, lo = _split3(ct * LOG2E)
    ones = jnp.where(lax.broadcasted_iota(jnp.int32, (16, tm), 0) == 0, 1.0, 0.0)
    slots_t = jnp.concatenate([hi, mid, lo, ones], axis=0).astype(BF16)
    augq_ref[0] = _dot(pqt_ref[...], slots_t).astype(BF16)
    qf = qt_ref[0].astype(F32)
    qn2 = jnp.max(_dot(gq_ref[...], (qf * qf).astype(BF16)), axis=1, keepdims=True)
    qst_ref[0, 0] = jnp.broadcast_to(qn2, (FOX_HEADS, LANES))


def _placement_matrices():
    pk = np.zeros((LANES, FOX_PAIRS * LANES), np.float32)
    pqt = np.zeros((FOX_HEADS * AUG_SLOTS_PER_HEAD, 64), np.float32)
    for h in range(FOX_HEADS):
        p, odd = divmod(h, 2)
        base = 6 * odd
        for piece in range(3):
            pk[48, LANES * p + base + piece] = 1.0
            pk[16 * piece + h, LANES * p + base + 3 + piece] = 1.0
            pqt[AUG_SLOTS_PER_HEAD * h + base + piece, 16 * piece + h] = 1.0
            pqt[AUG_SLOTS_PER_HEAD * h + base + 3 + piece, 48] = 1.0
    return jnp.asarray(pk, BF16), jnp.asarray(pqt, BF16)


def _fox_in(x, norm_g, w_in, b_f, *, tm, ch):
    b, s, d = x.shape
    width = FOX_HEADS * FOX_HEAD_DIM
    wq = w_in[:, :width] * (FOX_HEAD_DIM ** -0.5 * LOG2E)
    wk = w_in[:, width:2 * width]
    wv = w_in[:, 2 * width:3 * width]
    wz = w_in[:, 3 * width:4 * width]
    wf = w_in[:, 4 * width:]
    wnat = jnp.concatenate([wk, jnp.pad(wf, ((0, 0), (0, LANES - FOX_HEADS)))], axis=1).astype(BF16)
    wt = jnp.concatenate([wq, wv, wz], axis=1).T.astype(BF16)
    wft = wf.T.astype(BF16)
    bnat = jnp.pad(b_f, (0, LANES - FOX_HEADS)).reshape(1, LANES)
    bt = b_f.reshape(FOX_HEADS, 1)
    r = np.arange(tm)
    tril = jnp.asarray(r[None, :] <= r[:, None], BF16)
    triu = jnp.asarray(r[:, None] <= r[None, :], BF16)
    pk, pqt = _placement_matrices()
    head_of = np.arange(width) // FOX_HEAD_DIM
    gk = jnp.asarray(head_of[:, None] == np.arange(LANES)[None, :], BF16)
    gq = jnp.asarray(np.arange(FOX_HEADS)[:, None] == head_of[None, :], BF16)

    const = lambda shape: pl.BlockSpec(shape, lambda bi, i: (0,) * len(shape))
    return pl.pallas_call(
        functools.partial(_fox_in_kernel, tm=tm, ch=ch),
        grid=(b, s // tm),
        in_specs=[
            pl.BlockSpec((1, tm, d), lambda bi, i: (bi, i, 0)),
            const((1, d)),
            const(wnat.shape), const(wt.shape), const(wft.shape),
            const((1, LANES)), const((FOX_HEADS, 1)),
            const((tm, tm)), const((tm, tm)),
            const(pk.shape), const(pqt.shape), const(gk.shape), const(gq.shape),
        ],
        out_specs=[
            pl.BlockSpec((1, tm, 2 * width), lambda bi, i: (bi, i, 0)),
            pl.BlockSpec((1, width, tm), lambda bi, i: (bi, 0, i)),
            pl.BlockSpec((1, tm // ch, width, ch), lambda bi, i: (bi, i, 0, 0)),
            pl.BlockSpec((1, width, tm), lambda bi, i: (bi, 0, i)),
            pl.BlockSpec((1, FOX_HEADS * AUG_SLOTS_PER_HEAD, tm), lambda bi, i: (bi, 0, i)),
            pl.BlockSpec((1, 1, 8, LANES), lambda bi, i: (bi, i, 0, 0)),
            pl.BlockSpec((1, 1, FOX_HEADS, LANES), lambda bi, i: (bi, i, 0, 0)),
        ],
        out_shape=[
            jax.ShapeDtypeStruct((b, s, 2 * width), BF16),
            jax.ShapeDtypeStruct((b, width, s), BF16),
            jax.ShapeDtypeStruct((b, s // ch, width, ch), BF16),
            jax.ShapeDtypeStruct((b, width, s), BF16),
            jax.ShapeDtypeStruct((b, FOX_HEADS * AUG_SLOTS_PER_HEAD, s), BF16),
            jax.ShapeDtypeStruct((b, s // tm, 8, LANES), F32),
            jax.ShapeDtypeStruct((b, s // tm, FOX_HEADS, LANES), F32),
        ],
        scratch_shapes=[pltpu.VMEM((1, LANES), F32), pltpu.VMEM((FOX_HEADS, LANES), F32)],
        compiler_params=_cparams(2),
        name="fox_in",
    )(x, norm_g.reshape(1, d), wnat, wt, wft, bnat, bt, tril, triu, pk, pqt, gk, gq)


def _fox_attn_kernel(kstart_ref, kaug_ref, vt_ref, qt_ref, augq_ref, ot_ref,
                     s_scr, p_scr, *, tq, tk, pairs_per_step):
    qi = pl.program_id(2)
    hd = FOX_HEAD_DIM
    zeros_hd = jnp.zeros((hd, tq), BF16)
    zeros_tail = jnp.zeros((LANES - AUG_SLOTS_PER_HEAD, tq), BF16)
    ones_block = lambda n: jnp.where(lax.broadcasted_iota(jnp.int32, (16, n), 0) == 0, 1.0, 0.0).astype(BF16)
    ones_rows = ones_block(tk)
    n_blocks = (qi * tq) // tk + 1
    k_first = kstart_ref[pl.program_id(0), pl.program_id(1), qi]

    heads = [(u, h) for u in range(pairs_per_step) for h in range(2)]
    w_head = {}
    for u, h in heads:
        q_rows = qt_ref[0, LANES * u + hd * h:LANES * u + hd * (h + 1), :]
        aug_rows = augq_ref[0, 2 * AUG_SLOTS_PER_HEAD * u + AUG_SLOTS_PER_HEAD * h:
                            2 * AUG_SLOTS_PER_HEAD * u + AUG_SLOTS_PER_HEAD * (h + 1), :]
        q_part = [q_rows, zeros_hd] if h == 0 else [zeros_hd, q_rows]
        w_head[u, h] = jnp.concatenate(q_part + [aug_rows, zeros_tail], axis=0)
    chains = [(u, h, sub) for u, h in heads for sub in range(tq // QUERY_LANES)]

    def issue_scores(kj, slot):
        ks = pl.multiple_of(kj * tk, tk)
        block_max = []
        for c, (u, h, sub) in enumerate(chains):
            kb = kaug_ref[0, pl.ds(ks, tk), 2 * LANES * u:2 * LANES * (u + 1)]
            s = _dot(kb, w_head[u, h][:, QUERY_LANES * sub:QUERY_LANES * (sub + 1)])
            s_scr[slot, c] = s
            block_max.append(jnp.max(s, axis=0, keepdims=True))
        return tuple(block_max)

    tri_key = lax.broadcasted_iota(jnp.int32, (QUERY_LANES, QUERY_LANES), 0)
    tri_qry = lax.broadcasted_iota(jnp.int32, (QUERY_LANES, QUERY_LANES), 1)
    causal = jnp.where(tri_key <= tri_qry, 0.0, MASK_VALUE)

    def pv_block(kj, c):
        u, h, _ = chains[c]
        v_rows = vt_ref[0, kj, LANES * u + hd * h:LANES * u + hd * (h + 1), :]
        return _dot(jnp.concatenate([v_rows, ones_rows], axis=0), p_scr[c])

    def step(kj, slot, state):
        carry, block_max = state
        next_max = issue_scores(kj + 1, 1 - slot)
        out = []
        for c in range(len(chains)):
            m_old, alpha_prev, acc = carry[c]
            acc = acc * alpha_prev + pv_block(jnp.maximum(kj - 1, 0), c)
            m_new = jnp.maximum(m_old, block_max[c])
            p_scr[c] = jnp.exp2(s_scr[slot, c] - m_new).astype(BF16)
            out.append((m_new, jnp.exp2(m_old - m_new), acc))
        return tuple(out), next_max

    def diagonal(kj, slot, state):
        carry, _ = state
        for c, (u, h, sub) in enumerate(chains):
            m_old, alpha_prev, acc = carry[c]
            acc = acc * alpha_prev + pv_block(jnp.maximum(kj - 1, 0), c)
            lo = QUERY_LANES * sub
            vis = lo + QUERY_LANES
            mid = s_scr[slot, c, lo:vis, :] + causal
            m_new = jnp.maximum(m_old, jnp.max(mid, axis=0, keepdims=True))
            if lo:
                top = s_scr[slot, c, 0:lo, :]
                m_new = jnp.maximum(m_new, jnp.max(top, axis=0, keepdims=True))
                p = jnp.concatenate([jnp.exp2(top - m_new).astype(BF16),
                                     jnp.exp2(mid - m_new).astype(BF16)], axis=0)
            else:
                p = jnp.exp2(mid - m_new).astype(BF16)
            v_rows = vt_ref[0, kj, LANES * u + hd * h:LANES * u + hd * (h + 1), 0:vis]
            pv = _dot(jnp.concatenate([v_rows, ones_block(vis)], axis=0), p)
            acc = acc * jnp.exp2(m_old - m_new) + pv
            ot_ref[0, LANES * u + hd * h:LANES * u + hd * (h + 1),
                   QUERY_LANES * sub:QUERY_LANES * (sub + 1)] = acc[0:hd] / acc[hd:hd + 1]

    p_scr[...] = jnp.zeros_like(p_scr)
    row = lambda value: jnp.full((1, QUERY_LANES), value, F32)
    carry = tuple((row(MASK_VALUE), row(1.0), jnp.zeros((hd + 16, QUERY_LANES), F32)) for _ in chains)
    state = (carry, issue_scores(k_first, 0))
    n_unmasked = n_blocks - 1 - k_first

    def pair(i, st):
        kj = k_first + 2 * i
        return step(kj + 1, 1, step(kj, 0, st))

    state = lax.fori_loop(0, n_unmasked // 2, pair, state)
    odd = lax.rem(n_unmasked, 2)
    state = lax.fori_loop(0, odd, lambda _, st: step(n_blocks - 2, 0, st), state)
    diagonal(n_blocks - 1, odd, state)


def _fox_prune_table(kst, qst, heads_per_step):
    kn = jnp.sqrt(kst[:, :, 0, :FOX_HEADS]) * NORM_SLACK
    qn = jnp.sqrt(qst[:, :, :, 0]) * NORM_SLACK
    c_first = kst[:, :, 1, :FOX_HEADS] * LOG2E
    c_last = kst[:, :, 2, :FOX_HEADS] * LOG2E
    bound = (qn[:, :, None, :] * kn[:, None, :, :] + c_first[:, :, None, :] - c_last[:, None, :, :]
             + (qn * kn)[:, :, None, :])
    n = kst.shape[1]
    earlier = jnp.arange(n)[None, :, None] < jnp.arange(n)[:, None, None]
    skip = (bound < -PRUNE_LOG2) & earlier[None]
    block = jnp.arange(n, dtype=jnp.int32)[None, None, :, None]
    lead = jnp.min(jnp.where(skip, n, block), axis=2)
    lead = jnp.min(lead.reshape(lead.shape[0], n, FOX_HEADS // heads_per_step, heads_per_step), axis=3)
    return jnp.transpose(lead, (0, 2, 1)).astype(jnp.int32)


def _fox_attn(kaug, qt, vt, augq, kstart, *, tq, tk, pairs_per_step):
    b, s, _ = kaug.shape
    width = FOX_HEADS * FOX_HEAD_DIM
    nk = s // tk
    pps = pairs_per_step
    n_chains = 2 * pps * (tq // QUERY_LANES)
    grid_spec = pltpu.PrefetchScalarGridSpec(
        num_scalar_prefetch=1,
        grid=(b, FOX_PAIRS // pps, s // tq),
        in_specs=[
            pl.BlockSpec((1, s, 2 * LANES * pps), lambda bi, p, qi, ks: (bi, 0, p)),
            pl.BlockSpec((1, nk, LANES * pps, tk), lambda bi, p, qi, ks: (bi, 0, p, 0)),
            pl.BlockSpec((1, LANES * pps, tq), lambda bi, p, qi, ks: (bi, p, qi)),
            pl.BlockSpec((1, 2 * AUG_SLOTS_PER_HEAD * pps, tq), lambda bi, p, qi, ks: (bi, p, qi)),
        ],
        out_specs=pl.BlockSpec((1, LANES * pps, tq), lambda bi, p, qi, ks: (bi, p, qi)),
        scratch_shapes=[pltpu.VMEM((2, n_chains, tk, QUERY_LANES), F32),
                        pltpu.VMEM((n_chains, tk, QUERY_LANES), BF16)],
    )
    return pl.pallas_call(
        functools.partial(_fox_attn_kernel, tq=tq, tk=tk, pairs_per_step=pps),
        grid_spec=grid_spec,
        out_shape=jax.ShapeDtypeStruct((b, width, s), F32),
        compiler_params=_cparams(3),
        name="fox_attn",
    )(kstart, kaug, vt, qt, augq)


def _ple(h1, p, wup_ref, wgate_ref):
    gate = jax.nn.sigmoid(_dot(h1.astype(BF16), wgate_ref[...]))
    up = _dot(p.astype(BF16), wup_ref[...])
    return h1 + up * gate


def _fox_out_kernel(ot_ref, zt_ref, x_ref, p_ref, wout_ref, wup_ref, wgate_ref, h_ref):
    z = zt_ref[0].astype(F32)
    gt = (ot_ref[0] * (z * jax.nn.sigmoid(z))).astype(BF16)
    y = lax.dot_general(gt, wout_ref[...], TN_DIMS, preferred_element_type=F32)
    h_ref[0] = _ple(x_ref[0] + y, p_ref[0], wup_ref, wgate_ref)


def _fox_out(ot, zt, x, p, w_out, w_up, w_gate, *, tm):
    b, s, d = x.shape
    width = ot.shape[1]
    pd = p.shape[-1]
    const = lambda shape: pl.BlockSpec(shape, lambda bi, i: (0,) * len(shape))
    return pl.pallas_call(
        _fox_out_kernel,
        grid=(b, s // tm),
        in_specs=[
            pl.BlockSpec((1, width, tm), lambda bi, i: (bi, 0, i)),
            pl.BlockSpec((1, width, tm), lambda bi, i: (bi, 0, i)),
            pl.BlockSpec((1, tm, d), lambda bi, i: (bi, i, 0)),
            pl.BlockSpec((1, tm, pd), lambda bi, i: (bi, i, 0)),
            const((width, d)), const((pd, d)), const((d, d)),
        ],
        out_specs=pl.BlockSpec((1, tm, d), lambda bi, i: (bi, i, 0)),
        out_shape=jax.ShapeDtypeStruct((b, s, d), F32),
        compiler_params=_cparams(2),
        name="fox_out",
    )(ot, zt, x, p, w_out.astype(BF16), w_up.astype(BF16), w_gate.astype(BF16))


PERM_ROWS = 256


def _phase_major_matrix(dilation):
    n = PERM_ROWS // dilation
    p = np.zeros((PERM_ROWS, PERM_ROWS), np.float32)
    for r in range(dilation):
        for i in range(n):
            p[r * n + i, dilation * i + r] = 1.0
    return p


def _dil_in_kernel(h_ref, g_ref, w_ref, *rest, dilations, tm):
    n_perm = sum(d > 1 for d in dilations)
    perm_refs, out_refs = rest[:n_perm], rest[n_perm:]
    hn = _rms_norm(h_ref[0], g_ref[...]).astype(BF16)
    cols = w_ref.shape[1] // len(dilations)
    perm_refs = list(perm_refs)
    for g, (dil, o_ref) in enumerate(zip(dilations, out_refs)):
        w = w_ref[:, cols * g:cols * (g + 1)]
        if dil == 1:
            o_ref[0] = _dot(hn, w).astype(BF16)
            continue
        perm = perm_refs.pop(0)[...]
        n = PERM_ROWS // dil
        for c in range(tm // PERM_ROWS):
            rows = slice(PERM_ROWS * c, PERM_ROWS * (c + 1))
            hp = _dot(perm, hn[rows]).astype(BF16)
            res = _dot(hp, w).astype(BF16)
            for r in range(dil):
                o_ref[0, n * c:n * (c + 1), cols * r:cols * (r + 1)] = res[n * r:n * (r + 1)]


def _dil_in(h, norm_g, w, dilations, *, tm):
    b, s, d = h.shape
    cols = w.shape[1] // len(dilations)
    perms = [jnp.asarray(_phase_major_matrix(dil), BF16) for dil in dilations if dil > 1]
    const = lambda shape: pl.BlockSpec(shape, lambda bi, i: (0,) * len(shape))
    return pl.pallas_call(
        functools.partial(_dil_in_kernel, dilations=dilations, tm=tm),
        grid=(b, s // tm),
        in_specs=[pl.BlockSpec((1, tm, d), lambda bi, i: (bi, i, 0)), const((1, d)),
                  const(w.shape)] + [const(p.shape) for p in perms],
        out_specs=[pl.BlockSpec((1, tm // dil, dil * cols), lambda bi, i: (bi, i, 0))
                   for dil in dilations],
        out_shape=[jax.ShapeDtypeStruct((b, s // dil, dil * cols), BF16) for dil in dilations],
        compiler_params=_cparams(2),
        name="dil_in_" + "_".join(str(dil) for dil in dilations),
    )(h, norm_g.reshape(1, d), w.astype(BF16), *perms)


def _dil_attn_kernel(q_ref, kc_ref, vc_ref, kp_ref, vp_ref, o_ref, st_ref,
                     kcat, vcat, bias_scr, *, dilation, kb, neg_slopes):
    n = pl.program_id(2)
    nw = DIL_WINDOW_STEPS
    hd = DIL_HEAD_DIM

    @pl.when((pl.program_id(0) == 0) & (pl.program_id(1) == 0) & (n == 0))
    def _():
        row = lax.broadcasted_iota(jnp.int32, (nw, 2 * nw), 0)
        col = lax.broadcasted_iota(jnp.int32, (nw, 2 * nw), 1)
        dist = nw + row - col
        band = (dist >= 0) & (dist <= nw)
        dist_f = (dist * dilation).astype(F32)
        for h in range(DIL_HEADS_PER_GROUP):
            bias_scr[h] = jnp.where(band, neg_slopes[h] * dist_f * LOG2E, MASK_VALUE)

    kcat[0:nw] = kp_ref[0]
    kcat[nw:nw * (kb + 1)] = kc_ref[0]
    vcat[0:nw] = vp_ref[0]
    vcat[nw:nw * (kb + 1)] = vc_ref[0]
    lane = lax.broadcasted_iota(jnp.int32, (nw, LANES), 1)
    col = lax.broadcasted_iota(jnp.int32, (nw, 2 * nw), 1)
    no_history = jnp.where(col < jnp.where(n == 0, nw, 0), MASK_VALUE, 0.0)

    heads = [slice(hd * h, hd * (h + 1)) for h in range(DIL_HEADS_PER_GROUP)]
    scores = [[lax.dot_general(q_ref[0, nw * jb:nw * (jb + 1), hs],
                               kcat[nw * jb:nw * (jb + 2), hs], NT_DIMS,
                               preferred_element_type=F32) for hs in heads] for jb in range(kb)]
    for jb in range(kb):
        rows = slice(nw * jb, nw * (jb + 1))
        stat_tile = jnp.zeros((nw, LANES), F32)
        for h, hs in enumerate(heads):
            t = scores[jb][h] + bias_scr[h]
            if jb == 0:
                t = t + no_history
            m = jnp.max(t, axis=1, keepdims=True)
            p = jnp.exp2(t - m)
            l = jnp.sum(p, axis=1, keepdims=True)
            o_ref[0, rows, hs] = _dot(p.astype(BF16), vcat[nw * jb:nw * (jb + 2), hs]).astype(BF16)
            stat_tile = jnp.where(lane == h, m, stat_tile)
            stat_tile = jnp.where(lane == h + DIL_HEADS_PER_GROUP, l, stat_tile)
        st_ref[0, rows, :] = stat_tile


def _alibi_neg_slopes(group):
    n = len(DIL_PATTERN) * DIL_HEADS_PER_GROUP
    k = np.arange(1, n + 1, dtype=np.float32)
    slopes = np.float32(2.0) ** (np.float32(-ALIBI_MAX_EXP) * k / np.float32(n))
    lo = group * DIL_HEADS_PER_GROUP
    return tuple(float(-v) for v in slopes[lo:lo + DIL_HEADS_PER_GROUP])


def _dil_attn(view, group, *, kb):
    _, dilation = DIL_PATTERN[group]
    b, length, width = view.shape
    nw = DIL_WINDOW_STEPS
    gw = DIL_HEADS_PER_GROUP * DIL_HEAD_DIM
    ncb = width // dilation // gw
    rows = nw * kb
    qc, kc, vc = 0, 1, 2

    def cur(cb):
        return pl.BlockSpec((1, rows, gw), lambda bi, r, n: (bi, n, r * ncb + cb))

    def prev(cb):
        return pl.BlockSpec((1, nw, gw),
                            lambda bi, r, n: (bi, jnp.maximum(n * kb - 1, 0), r * ncb + cb))

    o, st = pl.pallas_call(
        functools.partial(_dil_attn_kernel, dilation=dilation, kb=kb,
                          neg_slopes=_alibi_neg_slopes(group)),
        grid=(b, dilation, length // rows),
        in_specs=[cur(qc), cur(kc), cur(vc), prev(kc), prev(vc)],
        out_specs=[
            pl.BlockSpec((1, rows, gw), lambda bi, r, n: (bi, n, r)),
            pl.BlockSpec((1, rows, LANES), lambda bi, r, n: (bi, n, r)),
        ],
        out_shape=[
            jax.ShapeDtypeStruct((b, length, dilation * gw), BF16),
            jax.ShapeDtypeStruct((b, length, dilation * LANES), F32),
        ],
        scratch_shapes=[pltpu.VMEM((nw * (kb + 1), gw), BF16), pltpu.VMEM((nw * (kb + 1), gw), BF16),
                        pltpu.VMEM((DIL_HEADS_PER_GROUP, nw, 2 * nw), F32)],
        compiler_params=_cparams(3),
        name=f"dil_attn_g{group}",
    )(view, view, view, view, view)
    return o, st


def _natural_rows(blk, perm_t, dilation, tm):
    cols = blk.shape[1] // dilation
    if dilation == 1:
        return blk
    n = PERM_ROWS // dilation
    chunks = []
    for c in range(tm // PERM_ROWS):
        phase_major = jnp.concatenate(
            [blk[n * c:n * (c + 1), cols * r:cols * (r + 1)] for r in range(dilation)], axis=0)
        chunks.append(_dot(perm_t, phase_major))
    return jnp.concatenate(chunks, axis=0)


def _dil_out_kernel(o0_ref, o1_ref, o2_ref, s0_ref, s1_ref, s2_ref, z_ref, h_ref, p_ref,
                    pt1_ref, pt2_ref, e_ref, wout_ref, wup_ref, wgate_ref, g_ref,
                    out_ref, *, dilations, tm):
    perm_ts = (None, pt1_ref[...], pt2_ref[...])
    n_heads = DIL_HEADS_PER_GROUP
    row_max, row_sum = [], []
    for st_ref, perm_t, dil in zip((s0_ref, s1_ref, s2_ref), perm_ts, dilations):
        if dil == 1:
            st = st_ref[0]
        else:
            hi, mid, lo = (_natural_rows(piece.astype(BF16), perm_t, dil, tm)
                           for piece in _split3(st_ref[0]))
            st = hi + mid + lo
        row_max.append(st)
        row_sum.append(pltpu.roll(st, LANES - n_heads, 1))
    m = jnp.maximum(jnp.maximum(row_max[0], row_max[1]), row_max[2])
    e = [jnp.exp2(v - m) for v in row_max]
    inv = 1.0 / (e[0] * row_sum[0] + e[1] * row_sum[1] + e[2] * row_sum[2])
    live = lax.broadcasted_iota(jnp.int32, (tm, LANES), 1) < n_heads
    o = None
    for ev, o_ref, perm_t, dil in zip(e, (o0_ref, o1_ref, o2_ref), perm_ts, dilations):
        w = jnp.where(live, ev * inv, 0.0)
        w_hi = w.astype(BF16)
        w_lo = (w - w_hi.astype(F32)).astype(BF16)
        w_full = _dot(jnp.concatenate([w_hi, w_lo], axis=1), e_ref[...])
        term = w_full * _natural_rows(o_ref[0], perm_t, dil, tm)
        o = term if o is None else o + term
    z = z_ref[0].astype(F32)
    g = (o * (z * jax.nn.sigmoid(z))).astype(BF16)
    h1 = h_ref[0] + _dot(g, wout_ref[...])
    out_ref[0] = _rms_norm(_ple(h1, p_ref[0], wup_ref, wgate_ref), g_ref[...])


def _dil_out(outs, stats, proj0, h, p, w_out, w_up, w_gate, final_g, *, tm):
    b, s, d = h.shape
    gw = DIL_HEADS_PER_GROUP * DIL_HEAD_DIM
    pd = p.shape[-1]
    dilations = tuple(dil for _, dil in DIL_PATTERN)
    z_block = proj0.shape[-1] // gw - 1
    expand = np.zeros((2 * LANES, gw), np.float32)
    for hh in range(DIL_HEADS_PER_GROUP):
        expand[hh, DIL_HEAD_DIM * hh:DIL_HEAD_DIM * (hh + 1)] = 1.0
        expand[LANES + hh, DIL_HEAD_DIM * hh:DIL_HEAD_DIM * (hh + 1)] = 1.0
    perm_ts = [jnp.asarray(_phase_major_matrix(dil).T, BF16) for dil in dilations[1:]]
    const = lambda shape: pl.BlockSpec(shape, lambda bi, i: (0,) * len(shape))
    row = lambda w: pl.BlockSpec((1, tm, w), lambda bi, i: (bi, i, 0))
    view = lambda w: [pl.BlockSpec((1, tm // dil, dil * w), lambda bi, i: (bi, i, 0))
                      for dil in dilations]
    return pl.pallas_call(
        functools.partial(_dil_out_kernel, dilations=dilations, tm=tm),
        grid=(b, s // tm),
        in_specs=view(gw) + view(LANES) + [
            pl.BlockSpec((1, tm, gw), lambda bi, i: (bi, i, z_block)),
            row(d), row(pd),
            const((PERM_ROWS, PERM_ROWS)), const((PERM_ROWS, PERM_ROWS)),
            const((2 * LANES, gw)), const((gw, d)), const((pd, d)), const((d, d)), const((1, d)),
        ],
        out_specs=row(d),
        out_shape=jax.ShapeDtypeStruct((b, s, d), F32),
        compiler_params=_cparams(2),
        name="dil_out",
    )(*outs, *stats, proj0, h, p, *perm_ts, jnp.asarray(expand, BF16),
      w_out.astype(BF16), w_up.astype(BF16), w_gate.astype(BF16), final_g.reshape(1, d))


def kernel(x, p, fox_norm, fox_w_in, fox_b_f, fox_w_out, dil_norm, dil_w_in, dil_w_out,
           ple_w_up, ple_w_gate, final_norm):
    s = x.shape[1]
    tm = min(512, s)
    tq = min(512, s)
    tk = min(512, s)
    assert tm == tq == tk, "the pruning table pairs fox_in tiles with query tiles and key blocks"
    kaug, qt, vt, zt, augq, kst, qst = _fox_in(x, fox_norm[0], fox_w_in[0], fox_b_f[0], tm=tm, ch=tk)
    pps = 2
    ot = _fox_attn(kaug, qt, vt, augq, _fox_prune_table(kst, qst, 2 * pps), tq=tq, tk=tk,
                   pairs_per_step=pps)
    h = _fox_out(ot, zt, x, p[0], fox_w_out[0], ple_w_up[0], ple_w_gate[0], tm=tm)

    gw = DIL_HEADS_PER_GROUP * DIL_HEAD_DIM
    n_groups = len(DIL_PATTERN)
    w1 = dil_w_in[0]
    k_scale = DIL_HEAD_DIM ** -0.5 * LOG2E
    qkv = lambda g: [w1[:, (n_groups * part + g) * gw:(n_groups * part + g + 1) * gw]
                     * (k_scale if part == 1 else 1.0) for part in range(3)]
    w_natural = jnp.concatenate(qkv(0) + [w1[:, 3 * n_groups * gw:]], axis=1)
    w_phased = jnp.concatenate(qkv(1) + qkv(2), axis=1)
    (proj0,) = _dil_in(h, dil_norm[0], w_natural, (1,), tm=tm)
    views = [proj0] + list(_dil_in(h, dil_norm[0], w_phased,
                                   (DIL_PATTERN[1][1], DIL_PATTERN[2][1]), tm=min(PERM_ROWS, s)))
    outs, stats = [], []
    for group, (_, dilation) in enumerate(DIL_PATTERN):
        kb = min(4, s // dilation // DIL_WINDOW_STEPS)
        o, st = _dil_attn(views[group], group, kb=kb)
        outs.append(o)
        stats.append(st)
    return _dil_out(outs, stats, proj0, h, p[1], dil_w_out[0], ple_w_up[1], ple_w_gate[1],
                    final_norm, tm=tm)
```

```python
import functools

import numpy as np
import jax
import jax.numpy as jnp
from jax import lax
from jax.experimental import pallas as pl
from jax.experimental.pallas import tpu as pltpu

F32 = jnp.float32
BF16 = jnp.bfloat16

RMS_EPS = 1e-6
FOX_HEADS = 16
FOX_HEAD_DIM = 64
FOX_PAIRS = FOX_HEADS // 2
DIL_PATTERN = ((128, 1), (512, 4), (2048, 16))
DIL_HEADS_PER_GROUP = 8
DIL_HEAD_DIM = 128
DIL_WINDOW_STEPS = 128
ALIBI_MAX_EXP = 8.0
MASK_VALUE = -1e30
LOG2E = 1.4426950408889634

LANES = 128
AUG_SLOTS_PER_HEAD = 16
QUERY_LANES = 256
PRUNE_LOG2 = 50.0
NORM_SLACK = 1.02
VMEM_LIMIT_BYTES = 56 * 1024 * 1024

NT_DIMS = (((1,), (1,)), ((), ()))
TN_DIMS = (((0,), (0,)), ((), ()))


def _cparams(n_axes):
    return pltpu.CompilerParams(
        dimension_semantics=("arbitrary",) * n_axes,
        vmem_limit_bytes=VMEM_LIMIT_BYTES,
    )


def _rms_norm(x, g):
    ms = jnp.mean(x * x, axis=-1, keepdims=True)
    return x * lax.rsqrt(ms + RMS_EPS) * g


def _log_sigmoid(x):
    return jnp.minimum(x, 0.0) - jnp.log1p(jnp.exp(-jnp.abs(x)))


def _split3(x):
    hi = x.astype(BF16).astype(F32)
    r1 = x - hi
    mid = r1.astype(BF16).astype(F32)
    lo = (r1 - mid).astype(BF16).astype(F32)
    return hi, mid, lo


def _dot(a, b):
    return jnp.dot(a, b, preferred_element_type=F32)


def _fox_in_kernel(x_ref, g_ref, wk_ref, wt_ref, wft_ref, bt_ref,
                   triu_ref, pk_ref, pqt_ref, gk_ref, gq_ref,
                   kaug_ref, qt_ref, vt_ref, zt_ref, augq_ref, kst_ref, qst_ref,
                   carryt_ref, *, tm, ch):
    i = pl.program_id(1)

    @pl.when(i == 0)
    def _():
        carryt_ref[...] = jnp.zeros_like(carryt_ref)

    hn = _rms_norm(x_ref[0], g_ref[...]).astype(BF16)
    width = FOX_HEADS * FOX_HEAD_DIM

    ft = lax.dot_general(wft_ref[...], hn, NT_DIMS, preferred_element_type=F32)
    hi, mid, lo = _split3(_log_sigmoid(ft + bt_ref[...]))
    stack = jnp.concatenate([hi, mid, lo], axis=0).astype(BF16)
    cs = _dot(stack, triu_ref[...])
    ct = cs[0:16] + cs[16:32] + cs[32:48] + carryt_ref[:, 0:1]
    carryt_ref[...] = jnp.broadcast_to(ct[:, tm - 1:tm], carryt_ref.shape)
    c = jnp.concatenate([ct, jnp.zeros((LANES - FOX_HEADS, tm), F32)], axis=0).T

    k = _dot(hn, wk_ref[...])
    nhi, nmid, nlo = _split3(c * -LOG2E)
    lane = lax.broadcasted_iota(jnp.int32, (tm, LANES), 1)
    slots = jnp.where(lane < 16, nhi,
                      jnp.where(lane < 32, pltpu.roll(nmid, 16, 1),
                                jnp.where(lane < 48, pltpu.roll(nlo, 32, 1),
                                          jnp.where(lane == 48, 1.0, 0.0))))
    kaug = _dot(slots.astype(BF16), pk_ref[...])
    kb = k.astype(BF16)
    for p in range(FOX_PAIRS):
        kaug_ref[0, :, 2 * LANES * p:2 * LANES * p + LANES] = kb[:, LANES * p:LANES * (p + 1)]
        kaug_ref[0, :, 2 * LANES * p + LANES:2 * LANES * (p + 1)] = (
            kaug[:, LANES * p:LANES * (p + 1)].astype(BF16))
    kf = kb.astype(F32)
    kn2 = jnp.max(_dot((kf * kf).astype(BF16), gk_ref[...]), axis=0, keepdims=True)
    kst_ref[0, 0] = jnp.concatenate(
        [kn2, c[0:1, :], c[tm - 1:tm, :], jnp.zeros((5, LANES), F32)], axis=0)

    rows = 512
    for c0 in range(0, 3 * width, rows):
        res = lax.dot_general(wt_ref[c0:c0 + rows, :], hn, NT_DIMS,
                              preferred_element_type=F32).astype(BF16)
        which, off = divmod(c0, width)
        if which == 0:
            qt_ref[0, off:off + rows, :] = res
        elif which == 1:
            for j in range(tm // ch):
                vt_ref[0, j, off:off + rows, :] = res[:, j * ch:(j + 1) * ch]
        else:
            zt_ref[0, off:off + rows, :] = res

    hi, mid, lo = _split3(ct * LOG2E)
    ones = jnp.where(lax.broadcasted_iota(jnp.int32, (16, tm), 0) == 0, 1.0, 0.0)
    slots_t = jnp.concatenate([hi, mid, lo, ones], axis=0).astype(BF16)
    augq_ref[0] = _dot(pqt_ref[...], slots_t).astype(BF16)
    qf = qt_ref[0].astype(F32)
    qn2 = jnp.max(_dot(gq_ref[...], (qf * qf).astype(BF16)), axis=1, keepdims=True)
    qst_ref[0, 0] = jnp.broadcast_to(qn2, (FOX_HEADS, LANES))


def _placement_matrices():
    pk = np.zeros((LANES, FOX_PAIRS * LANES), np.float32)
    pqt = np.zeros((FOX_HEADS * AUG_SLOTS_PER_HEAD, 64), np.float32)
    for h in range(FOX_HEADS):
        p, odd = divmod(h, 2)
        base = 6 * odd
        for piece in range(3):
            pk[48, LANES * p + base + piece] = 1.0
            pk[16 * piece + h, LANES * p + base + 3 + piece] = 1.0
            pqt[AUG_SLOTS_PER_HEAD * h + base + piece, 16 * piece + h] = 1.0
            pqt[AUG_SLOTS_PER_HEAD * h + base + 3 + piece, 48] = 1.0
    return jnp.asarray(pk, BF16), jnp.asarray(pqt, BF16)


def _fox_in(x, norm_g, w_in, b_f, *, tm, ch):
    b, s, d = x.shape
    width = FOX_HEADS * FOX_HEAD_DIM
    wq = w_in[:, :width] * (FOX_HEAD_DIM ** -0.5 * LOG2E)
    wk = w_in[:, width:2 * width]
    wv = w_in[:, 2 * width:3 * width]
    wz = w_in[:, 3 * width:4 * width]
    wf = w_in[:, 4 * width:]
    wt = jnp.concatenate([wq, wv, wz], axis=1).T.astype(BF16)
    wft = wf.T.astype(BF16)
    bt = b_f.reshape(FOX_HEADS, 1)
    r = np.arange(tm)
    triu = jnp.asarray(r[:, None] <= r[None, :], BF16)
    pk, pqt = _placement_matrices()
    head_of = np.arange(width) // FOX_HEAD_DIM
    gk = jnp.asarray(head_of[:, None] == np.arange(LANES)[None, :], BF16)
    gq = jnp.asarray(np.arange(FOX_HEADS)[:, None] == head_of[None, :], BF16)

    const = lambda shape: pl.BlockSpec(shape, lambda bi, i: (0,) * len(shape))
    return pl.pallas_call(
        functools.partial(_fox_in_kernel, tm=tm, ch=ch),
        grid=(b, s // tm),
        in_specs=[
            pl.BlockSpec((1, tm, d), lambda bi, i: (bi, i, 0)),
            const((1, d)),
            const(wk.shape), const(wt.shape), const(wft.shape),
            const((FOX_HEADS, 1)),
            const((tm, tm)),
            const(pk.shape), const(pqt.shape), const(gk.shape), const(gq.shape),
        ],
        out_specs=[
            pl.BlockSpec((1, tm, 2 * width), lambda bi, i: (bi, i, 0)),
            pl.BlockSpec((1, width, tm), lambda bi, i: (bi, 0, i)),
            pl.BlockSpec((1, tm // ch, width, ch), lambda bi, i: (bi, i, 0, 0)),
            pl.BlockSpec((1, width, tm), lambda bi, i: (bi, 0, i)),
            pl.BlockSpec((1, FOX_HEADS * AUG_SLOTS_PER_HEAD, tm), lambda bi, i: (bi, 0, i)),
            pl.BlockSpec((1, 1, 8, LANES), lambda bi, i: (bi, i, 0, 0)),
            pl.BlockSpec((1, 1, FOX_HEADS, LANES), lambda bi, i: (bi, i, 0, 0)),
        ],
        out_shape=[
            jax.ShapeDtypeStruct((b, s, 2 * width), BF16),
            jax.ShapeDtypeStruct((b, width, s), BF16),
            jax.ShapeDtypeStruct((b, s // ch, width, ch), BF16),
            jax.ShapeDtypeStruct((b, width, s), BF16),
            jax.ShapeDtypeStruct((b, FOX_HEADS * AUG_SLOTS_PER_HEAD, s), BF16),
            jax.ShapeDtypeStruct((b, s // tm, 8, LANES), F32),
            jax.ShapeDtypeStruct((b, s // tm, FOX_HEADS, LANES), F32),
        ],
        scratch_shapes=[pltpu.VMEM((FOX_HEADS, LANES), F32)],
        compiler_params=_cparams(2),
        name="fox_in",
    )(x, norm_g.reshape(1, d), wk.astype(BF16), wt, wft, bt, triu, pk, pqt, gk, gq)


def _fox_attn_kernel(kstart_ref, kaug_ref, vt_ref, qt_ref, augq_ref, ot_ref,
                     s_scr, p_scr, *, tq, tk, pairs_per_step):
    qi = pl.program_id(2)
    hd = FOX_HEAD_DIM
    zeros_hd = jnp.zeros((hd, tq), BF16)
    zeros_tail = jnp.zeros((LANES - AUG_SLOTS_PER_HEAD, tq), BF16)
    ones_block = lambda n: jnp.where(lax.broadcasted_iota(jnp.int32, (16, n), 0) == 0, 1.0, 0.0).astype(BF16)
    ones_rows = ones_block(tk)
    n_blocks = (qi * tq) // tk + 1
    k_first = kstart_ref[pl.program_id(0), pl.program_id(1), qi]

    heads = [(u, h) for u in range(pairs_per_step) for h in range(2)]
    w_head = {}
    for u, h in heads:
        q_rows = qt_ref[0, LANES * u + hd * h:LANES * u + hd * (h + 1), :]
        aug_rows = augq_ref[0, 2 * AUG_SLOTS_PER_HEAD * u + AUG_SLOTS_PER_HEAD * h:
                            2 * AUG_SLOTS_PER_HEAD * u + AUG_SLOTS_PER_HEAD * (h + 1), :]
        q_part = [q_rows, zeros_hd] if h == 0 else [zeros_hd, q_rows]
        w_head[u, h] = jnp.concatenate(q_part + [aug_rows, zeros_tail], axis=0)
    chains = [(u, h, sub) for u, h in heads for sub in range(tq // QUERY_LANES)]

    def issue_scores(kj, slot):
        ks = pl.multiple_of(kj * tk, tk)
        block_max = []
        for c, (u, h, sub) in enumerate(chains):
            kb = kaug_ref[0, pl.ds(ks, tk), 2 * LANES * u:2 * LANES * (u + 1)]
            s = _dot(kb, w_head[u, h][:, QUERY_LANES * sub:QUERY_LANES * (sub + 1)])
            s_scr[slot, c] = s
            block_max.append(jnp.max(s, axis=0, keepdims=True))
        return tuple(block_max)

    tri_key = lax.broadcasted_iota(jnp.int32, (QUERY_LANES, QUERY_LANES), 0)
    tri_qry = lax.broadcasted_iota(jnp.int32, (QUERY_LANES, QUERY_LANES), 1)
    causal = jnp.where(tri_key <= tri_qry, 0.0, MASK_VALUE)

    def pv_block(kj, c):
        u, h, _ = chains[c]
        v_rows = vt_ref[0, kj, LANES * u + hd * h:LANES * u + hd * (h + 1), :]
        return _dot(jnp.concatenate([v_rows, ones_rows], axis=0), p_scr[c])

    def step(kj, slot, state):
        carry, block_max = state
        next_max = issue_scores(kj + 1, 1 - slot)
        out = []
        for c in range(len(chains)):
            m_old, alpha_prev, acc = carry[c]
            acc = acc * alpha_prev + pv_block(jnp.maximum(kj - 1, 0), c)
            m_new = jnp.maximum(m_old, block_max[c])
            p_scr[c] = jnp.exp2(s_scr[slot, c] - m_new).astype(BF16)
            out.append((m_new, jnp.exp2(m_old - m_new), acc))
        return tuple(out), next_max

    def diagonal(kj, slot, state):
        carry, _ = state
        for c, (u, h, sub) in enumerate(chains):
            m_old, alpha_prev, acc = carry[c]
            acc = acc * alpha_prev + pv_block(jnp.maximum(kj - 1, 0), c)
            lo = QUERY_LANES * sub
            vis = lo + QUERY_LANES
            mid = s_scr[slot, c, lo:vis, :] + causal
            m_new = jnp.maximum(m_old, jnp.max(mid, axis=0, keepdims=True))
            if lo:
                top = s_scr[slot, c, 0:lo, :]
                m_new = jnp.maximum(m_new, jnp.max(top, axis=0, keepdims=True))
                p = jnp.concatenate([jnp.exp2(top - m_new).astype(BF16),
                                     jnp.exp2(mid - m_new).astype(BF16)], axis=0)
            else:
                p = jnp.exp2(mid - m_new).astype(BF16)
            v_rows = vt_ref[0, kj, LANES * u + hd * h:LANES * u + hd * (h + 1), 0:vis]
            pv = _dot(jnp.concatenate([v_rows, ones_block(vis)], axis=0), p)
            acc = acc * jnp.exp2(m_old - m_new) + pv
            ot_ref[0, LANES * u + hd * h:LANES * u + hd * (h + 1),
                   QUERY_LANES * sub:QUERY_LANES * (sub + 1)] = acc[0:hd] / acc[hd:hd + 1]

    p_scr[...] = jnp.zeros_like(p_scr)
    row = lambda value: jnp.full((1, QUERY_LANES), value, F32)
    carry = tuple((row(MASK_VALUE), row(1.0), jnp.zeros((hd + 16, QUERY_LANES), F32)) for _ in chains)
    state = (carry, tuple(row(0.0) for _ in chains))
    n_unmasked = n_blocks - 1 - k_first
    has_unmasked = jnp.minimum(n_unmasked, 1)

    def first(_, st):
        return step(k_first, 0, (st[0], issue_scores(k_first, 0)))

    def only_diagonal(_, st):
        return st[0], issue_scores(k_first, 0)

    def pair(i, st):
        kj = k_first + 1 + 2 * i
        return step(kj + 1, 0, step(kj, 1, st))

    state = lax.fori_loop(0, has_unmasked, first, state)
    state = lax.fori_loop(0, 1 - has_unmasked, only_diagonal, state)
    n_rest = jnp.maximum(n_unmasked - 1, 0)
    state = lax.fori_loop(0, n_rest // 2, pair, state)
    state = lax.fori_loop(0, lax.rem(n_rest, 2), lambda _, st: step(n_blocks - 2, 1, st), state)
    diagonal(n_blocks - 1, lax.rem(n_unmasked, 2), state)


def _fox_prune_table(kst, qst, heads_per_step):
    kn = jnp.sqrt(kst[:, :, 0, :FOX_HEADS]) * NORM_SLACK
    qn = jnp.sqrt(qst[:, :, :, 0]) * NORM_SLACK
    c_first = kst[:, :, 1, :FOX_HEADS] * LOG2E
    c_last = kst[:, :, 2, :FOX_HEADS] * LOG2E
    bound = (qn[:, :, None, :] * kn[:, None, :, :] + c_first[:, :, None, :] - c_last[:, None, :, :]
             + (qn * kn)[:, :, None, :])
    n = kst.shape[1]
    earlier = jnp.arange(n)[None, :, None] < jnp.arange(n)[:, None, None]
    skip = (bound < -PRUNE_LOG2) & earlier[None]
    block = jnp.arange(n, dtype=jnp.int32)[None, None, :, None]
    lead = jnp.min(jnp.where(skip, n, block), axis=2)
    lead = jnp.min(lead.reshape(lead.shape[0], n, FOX_HEADS // heads_per_step, heads_per_step), axis=3)
    return jnp.transpose(lead, (0, 2, 1)).astype(jnp.int32)


def _fox_attn(kaug, qt, vt, augq, kstart, *, tq, tk, pairs_per_step):
    b, s, _ = kaug.shape
    width = FOX_HEADS * FOX_HEAD_DIM
    nk = s // tk
    pps = pairs_per_step
    n_chains = 2 * pps * (tq // QUERY_LANES)
    grid_spec = pltpu.PrefetchScalarGridSpec(
        num_scalar_prefetch=1,
        grid=(b, FOX_PAIRS // pps, s // tq),
        in_specs=[
            pl.BlockSpec((1, s, 2 * LANES * pps), lambda bi, p, qi, ks: (bi, 0, p)),
            pl.BlockSpec((1, nk, LANES * pps, tk), lambda bi, p, qi, ks: (bi, 0, p, 0)),
            pl.BlockSpec((1, LANES * pps, tq), lambda bi, p, qi, ks: (bi, p, qi)),
            pl.BlockSpec((1, 2 * AUG_SLOTS_PER_HEAD * pps, tq), lambda bi, p, qi, ks: (bi, p, qi)),
        ],
        out_specs=pl.BlockSpec((1, LANES * pps, tq), lambda bi, p, qi, ks: (bi, p, qi)),
        scratch_shapes=[pltpu.VMEM((2, n_chains, tk, QUERY_LANES), F32),
                        pltpu.VMEM((n_chains, tk, QUERY_LANES), BF16)],
    )
    return pl.pallas_call(
        functools.partial(_fox_attn_kernel, tq=tq, tk=tk, pairs_per_step=pps),
        grid_spec=grid_spec,
        out_shape=jax.ShapeDtypeStruct((b, width, s), F32),
        compiler_params=_cparams(3),
        name="fox_attn",
    )(kstart, kaug, vt, qt, augq)


def _ple(h1, p, wup_ref, wgate_ref):
    gate = jax.nn.sigmoid(_dot(h1.astype(BF16), wgate_ref[...]))
    up = _dot(p.astype(BF16), wup_ref[...])
    return h1 + up * gate


def _fox_out_kernel(ot_ref, zt_ref, x_ref, p_ref, wout_ref, wup_ref, wgate_ref, h_ref):
    z = zt_ref[0].astype(F32)
    gt = (ot_ref[0] * (z * jax.nn.sigmoid(z))).astype(BF16)
    y = lax.dot_general(gt, wout_ref[...], TN_DIMS, preferred_element_type=F32)
    h_ref[0] = _ple(x_ref[0] + y, p_ref[0], wup_ref, wgate_ref)


def _fox_out(ot, zt, x, p, w_out, w_up, w_gate, *, tm):
    b, s, d = x.shape
    width = ot.shape[1]
    pd = p.shape[-1]
    const = lambda shape: pl.BlockSpec(shape, lambda bi, i: (0,) * len(shape))
    return pl.pallas_call(
        _fox_out_kernel,
        grid=(b, s // tm),
        in_specs=[
            pl.BlockSpec((1, width, tm), lambda bi, i: (bi, 0, i)),
            pl.BlockSpec((1, width, tm), lambda bi, i: (bi, 0, i)),
            pl.BlockSpec((1, tm, d), lambda bi, i: (bi, i, 0)),
            pl.BlockSpec((1, tm, pd), lambda bi, i: (bi, i, 0)),
            const((width, d)), const((pd, d)), const((d, d)),
        ],
        out_specs=pl.BlockSpec((1, tm, d), lambda bi, i: (bi, i, 0)),
        out_shape=jax.ShapeDtypeStruct((b, s, d), F32),
        compiler_params=_cparams(2),
        name="fox_out",
    )(ot, zt, x, p, w_out.astype(BF16), w_up.astype(BF16), w_gate.astype(BF16))


PERM_ROWS = 256


def _phase_major_matrix(dilation):
    n = PERM_ROWS // dilation
    p = np.zeros((PERM_ROWS, PERM_ROWS), np.float32)
    for r in range(dilation):
        for i in range(n):
            p[r * n + i, dilation * i + r] = 1.0
    return p


def _dil_in_kernel(h_ref, g_ref, w_ref, *rest, dilations, tm):
    n_perm = sum(d > 1 for d in dilations)
    perm_refs, out_refs = rest[:n_perm], rest[n_perm:]
    hn = _rms_norm(h_ref[0], g_ref[...]).astype(BF16)
    cols = w_ref.shape[1] // len(dilations)
    perm_refs = list(perm_refs)
    for g, (dil, o_ref) in enumerate(zip(dilations, out_refs)):
        w = w_ref[:, cols * g:cols * (g + 1)]
        if dil == 1:
            o_ref[0] = _dot(hn, w).astype(BF16)
            continue
        perm = perm_refs.pop(0)[...]
        n = PERM_ROWS // dil
        for c in range(tm // PERM_ROWS):
            rows = slice(PERM_ROWS * c, PERM_ROWS * (c + 1))
            hp = _dot(perm, hn[rows]).astype(BF16)
            res = _dot(hp, w).astype(BF16)
            for r in range(dil):
                o_ref[0, n * c:n * (c + 1), cols * r:cols * (r + 1)] = res[n * r:n * (r + 1)]


def _dil_in(h, norm_g, w, dilations, *, tm):
    b, s, d = h.shape
    cols = w.shape[1] // len(dilations)
    perms = [jnp.asarray(_phase_major_matrix(dil), BF16) for dil in dilations if dil > 1]
    const = lambda shape: pl.BlockSpec(shape, lambda bi, i: (0,) * len(shape))
    return pl.pallas_call(
        functools.partial(_dil_in_kernel, dilations=dilations, tm=tm),
        grid=(b, s // tm),
        in_specs=[pl.BlockSpec((1, tm, d), lambda bi, i: (bi, i, 0)), const((1, d)),
                  const(w.shape)] + [const(p.shape) for p in perms],
        out_specs=[pl.BlockSpec((1, tm // dil, dil * cols), lambda bi, i: (bi, i, 0))
                   for dil in dilations],
        out_shape=[jax.ShapeDtypeStruct((b, s // dil, dil * cols), BF16) for dil in dilations],
        compiler_params=_cparams(2),
        name="dil_in_" + "_".join(str(dil) for dil in dilations),
    )(h, norm_g.reshape(1, d), w.astype(BF16), *perms)


def _dil_attn_kernel(q_ref, kc_ref, vc_ref, kp_ref, vp_ref, o_ref, st_ref,
                     kcat, vcat, bias_scr, *, dilation, kb, neg_slopes):
    n = pl.program_id(2)
    nw = DIL_WINDOW_STEPS
    hd = DIL_HEAD_DIM

    @pl.when((pl.program_id(0) == 0) & (pl.program_id(1) == 0) & (n == 0))
    def _():
        row = lax.broadcasted_iota(jnp.int32, (nw, 2 * nw), 0)
        col = lax.broadcasted_iota(jnp.int32, (nw, 2 * nw), 1)
        dist = nw + row - col
        band = (dist >= 0) & (dist <= nw)
        dist_f = (dist * dilation).astype(F32)
        for h in range(DIL_HEADS_PER_GROUP):
            bias_scr[h] = jnp.where(band, neg_slopes[h] * dist_f * LOG2E, MASK_VALUE)

    kcat[0:nw] = kp_ref[0]
    kcat[nw:nw * (kb + 1)] = kc_ref[0]
    vcat[0:nw] = vp_ref[0]
    vcat[nw:nw * (kb + 1)] = vc_ref[0]
    lane = lax.broadcasted_iota(jnp.int32, (nw, LANES), 1)
    col = lax.broadcasted_iota(jnp.int32, (nw, 2 * nw), 1)
    no_history = jnp.where(col < jnp.where(n == 0, nw, 0), MASK_VALUE, 0.0)

    heads = [slice(hd * h, hd * (h + 1)) for h in range(DIL_HEADS_PER_GROUP)]
    scores = [[lax.dot_general(q_ref[0, nw * jb:nw * (jb + 1), hs],
                               kcat[nw * jb:nw * (jb + 2), hs], NT_DIMS,
                               preferred_element_type=F32) for hs in heads] for jb in range(kb)]
    for jb in range(kb):
        rows = slice(nw * jb, nw * (jb + 1))
        stat_tile = jnp.zeros((nw, LANES), F32)
        for h, hs in enumerate(heads):
            t = scores[jb][h] + bias_scr[h]
            if jb == 0:
                t = t + no_history
            m = jnp.max(t, axis=1, keepdims=True)
            p = jnp.exp2(t - m)
            l = jnp.sum(p, axis=1, keepdims=True)
            o_ref[0, rows, hs] = _dot(p.astype(BF16), vcat[nw * jb:nw * (jb + 2), hs]).astype(BF16)
            stat_tile = jnp.where(lane == h, m, stat_tile)
            stat_tile = jnp.where(lane == h + DIL_HEADS_PER_GROUP, l, stat_tile)
        st_ref[0, rows, :] = stat_tile


def _alibi_neg_slopes(group):
    n = len(DIL_PATTERN) * DIL_HEADS_PER_GROUP
    k = np.arange(1, n + 1, dtype=np.float32)
    slopes = np.float32(2.0) ** (np.float32(-ALIBI_MAX_EXP) * k / np.float32(n))
    lo = group * DIL_HEADS_PER_GROUP
    return tuple(float(-v) for v in slopes[lo:lo + DIL_HEADS_PER_GROUP])


def _dil_attn(view, group, *, kb):
    _, dilation = DIL_PATTERN[group]
    b, length, width = view.shape
    nw = DIL_WINDOW_STEPS
    gw = DIL_HEADS_PER_GROUP * DIL_HEAD_DIM
    ncb = width // dilation // gw
    rows = nw * kb
    qc, kc, vc = 0, 1, 2

    def cur(cb):
        return pl.BlockSpec((1, rows, gw), lambda bi, r, n: (bi, n, r * ncb + cb))

    def prev(cb):
        return pl.BlockSpec((1, nw, gw),
                            lambda bi, r, n: (bi, jnp.maximum(n * kb - 1, 0), r * ncb + cb))

    o, st = pl.pallas_call(
        functools.partial(_dil_attn_kernel, dilation=dilation, kb=kb,
                          neg_slopes=_alibi_neg_slopes(group)),
        grid=(b, dilation, length // rows),
        in_specs=[cur(qc), cur(kc), cur(vc), prev(kc), prev(vc)],
        out_specs=[
            pl.BlockSpec((1, rows, gw), lambda bi, r, n: (bi, n, r)),
            pl.BlockSpec((1, rows, LANES), lambda bi, r, n: (bi, n, r)),
        ],
        out_shape=[
            jax.ShapeDtypeStruct((b, length, dilation * gw), BF16),
            jax.ShapeDtypeStruct((b, length, dilation * LANES), F32),
        ],
        scratch_shapes=[pltpu.VMEM((nw * (kb + 1), gw), BF16), pltpu.VMEM((nw * (kb + 1), gw), BF16),
                        pltpu.VMEM((DIL_HEADS_PER_GROUP, nw, 2 * nw), F32)],
        compiler_params=_cparams(3),
        name=f"dil_attn_g{group}",
    )(view, view, view, view, view)
    return o, st


def _natural_rows(blk, perm_t, dilation, tm):
    cols = blk.shape[1] // dilation
    if dilation == 1:
        return blk
    n = PERM_ROWS // dilation
    chunks = []
    for c in range(tm // PERM_ROWS):
        phase_major = jnp.concatenate(
            [blk[n * c:n * (c + 1), cols * r:cols * (r + 1)] for r in range(dilation)], axis=0)
        chunks.append(_dot(perm_t, phase_major))
    return jnp.concatenate(chunks, axis=0)


def _dil_out_kernel(o0_ref, o1_ref, o2_ref, s0_ref, s1_ref, s2_ref, z_ref, h_ref, p_ref,
                    pt1_ref, pt2_ref, e_ref, wout_ref, wup_ref, wgate_ref, g_ref,
                    out_ref, *, dilations, tm):
    perm_ts = (None, pt1_ref[...], pt2_ref[...])
    n_heads = DIL_HEADS_PER_GROUP
    row_max, row_sum = [], []
    for st_ref, perm_t, dil in zip((s0_ref, s1_ref, s2_ref), perm_ts, dilations):
        if dil == 1:
            st = st_ref[0]
        else:
            hi, mid, lo = (_natural_rows(piece.astype(BF16), perm_t, dil, tm)
                           for piece in _split3(st_ref[0]))
            st = hi + mid + lo
        row_max.append(st)
        row_sum.append(pltpu.roll(st, LANES - n_heads, 1))
    m = jnp.maximum(jnp.maximum(row_max[0], row_max[1]), row_max[2])
    e = [jnp.exp2(v - m) for v in row_max]
    inv = 1.0 / (e[0] * row_sum[0] + e[1] * row_sum[1] + e[2] * row_sum[2])
    live = lax.broadcasted_iota(jnp.int32, (tm, LANES), 1) < n_heads
    o = None
    for ev, o_ref, perm_t, dil in zip(e, (o0_ref, o1_ref, o2_ref), perm_ts, dilations):
        w = jnp.where(live, ev * inv, 0.0)
        w_hi = w.astype(BF16)
        w_lo = (w - w_hi.astype(F32)).astype(BF16)
        w_full = _dot(jnp.concatenate([w_hi, w_lo], axis=1), e_ref[...])
        term = w_full * _natural_rows(o_ref[0], perm_t, dil, tm)
        o = term if o is None else o + term
    z = z_ref[0].astype(F32)
    g = (o * (z * jax.nn.sigmoid(z))).astype(BF16)
    h1 = h_ref[0] + _dot(g, wout_ref[...])
    out_ref[0] = _rms_norm(_ple(h1, p_ref[0], wup_ref, wgate_ref), g_ref[...])


def _dil_out(outs, stats, proj0, h, p, w_out, w_up, w_gate, final_g, *, tm):
    b, s, d = h.shape
    gw = DIL_HEADS_PER_GROUP * DIL_HEAD_DIM
    pd = p.shape[-1]
    dilations = tuple(dil for _, dil in DIL_PATTERN)
    z_block = proj0.shape[-1] // gw - 1
    expand = np.zeros((2 * LANES, gw), np.float32)
    for hh in range(DIL_HEADS_PER_GROUP):
        expand[hh, DIL_HEAD_DIM * hh:DIL_HEAD_DIM * (hh + 1)] = 1.0
        expand[LANES + hh, DIL_HEAD_DIM * hh:DIL_HEAD_DIM * (hh + 1)] = 1.0
    perm_ts = [jnp.asarray(_phase_major_matrix(dil).T, BF16) for dil in dilations[1:]]
    const = lambda shape: pl.BlockSpec(shape, lambda bi, i: (0,) * len(shape))
    row = lambda w: pl.BlockSpec((1, tm, w), lambda bi, i: (bi, i, 0))
    view = lambda w: [pl.BlockSpec((1, tm // dil, dil * w), lambda bi, i: (bi, i, 0))
                      for dil in dilations]
    return pl.pallas_call(
        functools.partial(_dil_out_kernel, dilations=dilations, tm=tm),
        grid=(b, s // tm),
        in_specs=view(gw) + view(LANES) + [
            pl.BlockSpec((1, tm, gw), lambda bi, i: (bi, i, z_block)),
            row(d), row(pd),
            const((PERM_ROWS, PERM_ROWS)), const((PERM_ROWS, PERM_ROWS)),
            const((2 * LANES, gw)), const((gw, d)), const((pd, d)), const((d, d)), const((1, d)),
        ],
        out_specs=row(d),
        out_shape=jax.ShapeDtypeStruct((b, s, d), F32),
        compiler_params=_cparams(2),
        name="dil_out",
    )(*outs, *stats, proj0, h, p, *perm_ts, jnp.asarray(expand, BF16),
      w_out.astype(BF16), w_up.astype(BF16), w_gate.astype(BF16), final_g.reshape(1, d))


def kernel(x, p, fox_norm, fox_w_in, fox_b_f, fox_w_out, dil_norm, dil_w_in, dil_w_out,
           ple_w_up, ple_w_gate, final_norm):
    s = x.shape[1]
    tm = min(512, s)
    tq = min(512, s)
    tk = min(512, s)
    assert tm == tq == tk, "the pruning table pairs fox_in tiles with query tiles and key blocks"
    kaug, qt, vt, zt, augq, kst, qst = _fox_in(x, fox_norm[0], fox_w_in[0], fox_b_f[0], tm=tm, ch=tk)
    pps = 2
    ot = _fox_attn(kaug, qt, vt, augq, _fox_prune_table(kst, qst, 2 * pps), tq=tq, tk=tk,
                   pairs_per_step=pps)
    h = _fox_out(ot, zt, x, p[0], fox_w_out[0], ple_w_up[0], ple_w_gate[0], tm=tm)

    gw = DIL_HEADS_PER_GROUP * DIL_HEAD_DIM
    n_groups = len(DIL_PATTERN)
    w1 = dil_w_in[0]
    k_scale = DIL_HEAD_DIM ** -0.5 * LOG2E
    qkv = lambda g: [w1[:, (n_groups * part + g) * gw:(n_groups * part + g + 1) * gw]
                     * (k_scale if part == 1 else 1.0) for part in range(3)]
    w_natural = jnp.concatenate(qkv(0) + [w1[:, 3 * n_groups * gw:]], axis=1)
    w_phased = jnp.concatenate(qkv(1) + qkv(2), axis=1)
    (proj0,) = _dil_in(h, dil_norm[0], w_natural, (1,), tm=tm)
    views = [proj0] + list(_dil_in(h, dil_norm[0], w_phased,
                                   (DIL_PATTERN[1][1], DIL_PATTERN[2][1]), tm=min(PERM_ROWS, s)))
    outs, stats = [], []
    for group, (_, dilation) in enumerate(DIL_PATTERN):
        kb = min(4, s // dilation // DIL_WINDOW_STEPS)
        o, st = _dil_attn(views[group], group, kb=kb)
        outs.append(o)
        stats.append(st)
    return _dil_out(outs, stats, proj0, h, p[1], dil_w_out[0], ple_w_up[1], ple_w_gate[1],
                    final_norm, tm=tm)
```

```python
import functools

import numpy as np
import jax
import jax.numpy as jnp
from jax import lax
from jax.experimental import pallas as pl
from jax.experimental.pallas import tpu as pltpu

F32 = jnp.float32
BF16 = jnp.bfloat16

RMS_EPS = 1e-6
FOX_HEADS = 16
FOX_HEAD_DIM = 64
FOX_PAIRS = FOX_HEADS // 2
DIL_PATTERN = ((128, 1), (512, 4), (2048, 16))
DIL_HEADS_PER_GROUP = 8
DIL_HEAD_DIM = 128
DIL_WINDOW_STEPS = 128
ALIBI_MAX_EXP = 8.0
MASK_VALUE = -1e30
LOG2E = 1.4426950408889634

LANES = 128
AUG_SLOTS_PER_HEAD = 16
QUERY_LANES = 256
PRUNE_LOG2 = 50.0
NORM_SLACK = 1.02
VMEM_LIMIT_BYTES = 56 * 1024 * 1024

NT_DIMS = (((1,), (1,)), ((), ()))
TN_DIMS = (((0,), (0,)), ((), ()))


def _cparams(n_axes):
    return pltpu.CompilerParams(
        dimension_semantics=("arbitrary",) * n_axes,
        vmem_limit_bytes=VMEM_LIMIT_BYTES,
    )


def _rms_norm(x, g):
    ms = jnp.mean(x * x, axis=-1, keepdims=True)
    return x * lax.rsqrt(ms + RMS_EPS) * g


def _log_sigmoid(x):
    return jnp.minimum(x, 0.0) - jnp.log1p(jnp.exp(-jnp.abs(x)))


def _split3(x):
    hi = x.astype(BF16).astype(F32)
    r1 = x - hi
    mid = r1.astype(BF16).astype(F32)
    lo = (r1 - mid).astype(BF16).astype(F32)
    return hi, mid, lo


def _dot(a, b):
    return jnp.dot(a, b, preferred_element_type=F32)


def _fox_in_kernel(x_ref, g_ref, wk_ref, wt_ref, wft_ref, bt_ref,
                   triu_ref, pk_ref, pqt_ref, gk_ref, gq_ref,
                   kaug_ref, qt_ref, vt_ref, zt_ref, augq_ref, kst_ref, qst_ref,
                   carryt_ref, *, tm, ch):
    i = pl.program_id(1)

    @pl.when(i == 0)
    def _():
        carryt_ref[...] = jnp.zeros_like(carryt_ref)

    hn = _rms_norm(x_ref[0], g_ref[...]).astype(BF16)
    width = FOX_HEADS * FOX_HEAD_DIM

    ft = lax.dot_general(wft_ref[...], hn, NT_DIMS, preferred_element_type=F32)
    hi, mid, lo = _split3(_log_sigmoid(ft + bt_ref[...]))
    stack = jnp.concatenate([hi, mid, lo], axis=0).astype(BF16)
    cs = _dot(stack, triu_ref[...])
    ct = cs[0:16] + cs[16:32] + cs[32:48] + carryt_ref[:, 0:1]
    carryt_ref[...] = jnp.broadcast_to(ct[:, tm - 1:tm], carryt_ref.shape)
    c = jnp.concatenate([ct, jnp.zeros((LANES - FOX_HEADS, tm), F32)], axis=0).T

    k = _dot(hn, wk_ref[...])
    nhi, nmid, nlo = _split3(c * -LOG2E)
    lane = lax.broadcasted_iota(jnp.int32, (tm, LANES), 1)
    slots = jnp.where(lane < 16, nhi,
                      jnp.where(lane < 32, pltpu.roll(nmid, 16, 1),
                                jnp.where(lane < 48, pltpu.roll(nlo, 32, 1),
                                          jnp.where(lane == 48, 1.0, 0.0))))
    kaug = _dot(slots.astype(BF16), pk_ref[...])
    kb = k.astype(BF16)
    for p in range(FOX_PAIRS):
        kaug_ref[0, :, 2 * LANES * p:2 * LANES * p + LANES] = kb[:, LANES * p:LANES * (p + 1)]
        kaug_ref[0, :, 2 * LANES * p + LANES:2 * LANES * (p + 1)] = (
            kaug[:, LANES * p:LANES * (p + 1)].astype(BF16))
    kf = kb.astype(F32)
    kn2 = jnp.max(_dot((kf * kf).astype(BF16), gk_ref[...]), axis=0, keepdims=True)
    kst_ref[0, 0] = jnp.concatenate(
        [kn2, c[0:1, :], c[tm - 1:tm, :], jnp.zeros((5, LANES), F32)], axis=0)

    rows = 512
    for c0 in range(0, 3 * width, rows):
        res = lax.dot_general(wt_ref[c0:c0 + rows, :], hn, NT_DIMS,
                              preferred_element_type=F32).astype(BF16)
        which, off = divmod(c0, width)
        if which == 0:
            qt_ref[0, off:off + rows, :] = res
        elif which == 1:
            for j in range(tm // ch):
                vt_ref[0, j, off:off + rows, :] = res[:, j * ch:(j + 1) * ch]
        else:
            zt_ref[0, off:off + rows, :] = res

    hi, mid, lo = _split3(ct * LOG2E)
    ones = jnp.where(lax.broadcasted_iota(jnp.int32, (16, tm), 0) == 0, 1.0, 0.0)
    slots_t = jnp.concatenate([hi, mid, lo, ones], axis=0).astype(BF16)
    augq_ref[0] = _dot(pqt_ref[...], slots_t).astype(BF16)
    qf = qt_ref[0].astype(F32)
    qn2 = jnp.max(_dot(gq_ref[...], (qf * qf).astype(BF16)), axis=1, keepdims=True)
    qst_ref[0, 0] = jnp.broadcast_to(qn2, (FOX_HEADS, LANES))


def _placement_matrices():
    pk = np.zeros((LANES, FOX_PAIRS * LANES), np.float32)
    pqt = np.zeros((FOX_HEADS * AUG_SLOTS_PER_HEAD, 64), np.float32)
    for h in range(FOX_HEADS):
        p, odd = divmod(h, 2)
        base = 6 * odd
        for piece in range(3):
            pk[48, LANES * p + base + piece] = 1.0
            pk[16 * piece + h, LANES * p + base + 3 + piece] = 1.0
            pqt[AUG_SLOTS_PER_HEAD * h + base + piece, 16 * piece + h] = 1.0
            pqt[AUG_SLOTS_PER_HEAD * h + base + 3 + piece, 48] = 1.0
    return jnp.asarray(pk, BF16), jnp.asarray(pqt, BF16)


def _fox_in(x, norm_g, w_in, b_f, *, tm, ch):
    b, s, d = x.shape
    width = FOX_HEADS * FOX_HEAD_DIM
    wq = w_in[:, :width] * (FOX_HEAD_DIM ** -0.5 * LOG2E)
    wk = w_in[:, width:2 * width]
    wv = w_in[:, 2 * width:3 * width]
    wz = w_in[:, 3 * width:4 * width]
    wf = w_in[:, 4 * width:]
    wt = jnp.concatenate([wq, wv, wz], axis=1).T.astype(BF16)
    wft = wf.T.astype(BF16)
    bt = b_f.reshape(FOX_HEADS, 1)
    r = np.arange(tm)
    triu = jnp.asarray(r[:, None] <= r[None, :], BF16)
    pk, pqt = _placement_matrices()
    head_of = np.arange(width) // FOX_HEAD_DIM
    gk = jnp.asarray(head_of[:, None] == np.arange(LANES)[None, :], BF16)
    gq = jnp.asarray(np.arange(FOX_HEADS)[:, None] == head_of[None, :], BF16)

    const = lambda shape: pl.BlockSpec(shape, lambda bi, i: (0,) * len(shape))
    return pl.pallas_call(
        functools.partial(_fox_in_kernel, tm=tm, ch=ch),
        grid=(b, s // tm),
        in_specs=[
            pl.BlockSpec((1, tm, d), lambda bi, i: (bi, i, 0)),
            const((1, d)),
            const(wk.shape), const(wt.shape), const(wft.shape),
            const((FOX_HEADS, 1)),
            const((tm, tm)),
            const(pk.shape), const(pqt.shape), const(gk.shape), const(gq.shape),
        ],
        out_specs=[
            pl.BlockSpec((1, tm, 2 * width), lambda bi, i: (bi, i, 0)),
            pl.BlockSpec((1, width, tm), lambda bi, i: (bi, 0, i)),
            pl.BlockSpec((1, tm // ch, width, ch), lambda bi, i: (bi, i, 0, 0)),
            pl.BlockSpec((1, width, tm), lambda bi, i: (bi, 0, i)),
            pl.BlockSpec((1, FOX_HEADS * AUG_SLOTS_PER_HEAD, tm), lambda bi, i: (bi, 0, i)),
            pl.BlockSpec((1, 1, 8, LANES), lambda bi, i: (bi, i, 0, 0)),
            pl.BlockSpec((1, 1, FOX_HEADS, LANES), lambda bi, i: (bi, i, 0, 0)),
        ],
        out_shape=[
            jax.ShapeDtypeStruct((b, s, 2 * width), BF16),
            jax.ShapeDtypeStruct((b, width, s), BF16),
            jax.ShapeDtypeStruct((b, s // ch, width, ch), BF16),
            jax.ShapeDtypeStruct((b, width, s), BF16),
            jax.ShapeDtypeStruct((b, FOX_HEADS * AUG_SLOTS_PER_HEAD, s), BF16),
            jax.ShapeDtypeStruct((b, s // tm, 8, LANES), F32),
            jax.ShapeDtypeStruct((b, s // tm, FOX_HEADS, LANES), F32),
        ],
        scratch_shapes=[pltpu.VMEM((FOX_HEADS, LANES), F32)],
        compiler_params=_cparams(2),
        name="fox_in",
    )(x, norm_g.reshape(1, d), wk.astype(BF16), wt, wft, bt, triu, pk, pqt, gk, gq)


def _fox_attn_kernel(kstart_ref, kaug_ref, vt_ref, qt_ref, augq_ref, ot_ref,
                     s_scr, p_scr, *, tq, tk, pairs_per_step):
    qi = pl.program_id(2)
    hd = FOX_HEAD_DIM
    zeros_hd = jnp.zeros((hd, tq), BF16)
    zeros_tail = jnp.zeros((LANES - AUG_SLOTS_PER_HEAD, tq), BF16)
    ones_block = lambda n: jnp.where(lax.broadcasted_iota(jnp.int32, (16, n), 0) == 0, 1.0, 0.0).astype(BF16)
    ones_rows = ones_block(tk)
    n_blocks = (qi * tq) // tk + 1
    k_first = kstart_ref[pl.program_id(0), pl.program_id(1), qi]

    heads = [(u, h) for u in range(pairs_per_step) for h in range(2)]
    w_head = {}
    for u, h in heads:
        q_rows = qt_ref[0, LANES * u + hd * h:LANES * u + hd * (h + 1), :]
        aug_rows = augq_ref[0, 2 * AUG_SLOTS_PER_HEAD * u + AUG_SLOTS_PER_HEAD * h:
                            2 * AUG_SLOTS_PER_HEAD * u + AUG_SLOTS_PER_HEAD * (h + 1), :]
        q_part = [q_rows, zeros_hd] if h == 0 else [zeros_hd, q_rows]
        w_head[u, h] = jnp.concatenate(q_part + [aug_rows, zeros_tail], axis=0)
    chains = [(u, h, sub) for u, h in heads for sub in range(tq // QUERY_LANES)]

    def issue_scores(kj, slot):
        ks = pl.multiple_of(kj * tk, tk)
        block_max = []
        for c, (u, h, sub) in enumerate(chains):
            kb = kaug_ref[0, pl.ds(ks, tk), 2 * LANES * u:2 * LANES * (u + 1)]
            s = _dot(kb, w_head[u, h][:, QUERY_LANES * sub:QUERY_LANES * (sub + 1)])
            s_scr[slot, c] = s
            block_max.append(jnp.max(s, axis=0, keepdims=True))
        return tuple(block_max)

    tri_key = lax.broadcasted_iota(jnp.int32, (QUERY_LANES, QUERY_LANES), 0)
    tri_qry = lax.broadcasted_iota(jnp.int32, (QUERY_LANES, QUERY_LANES), 1)
    causal = jnp.where(tri_key <= tri_qry, 0.0, MASK_VALUE)

    def pv_block(kj, c):
        u, h, _ = chains[c]
        v_rows = vt_ref[0, kj, LANES * u + hd * h:LANES * u + hd * (h + 1), :]
        return _dot(jnp.concatenate([v_rows, ones_rows], axis=0), p_scr[c])

    def step(kj, slot, state, has_prev=True):
        carry, block_max = state
        next_max = issue_scores(kj + 1, 1 - slot)
        out = []
        for c in range(len(chains)):
            m_old, alpha_prev, acc = carry[c]
            if has_prev:
                acc = acc * alpha_prev + pv_block(kj - 1, c)
            m_new = jnp.maximum(m_old, block_max[c])
            p_scr[c] = jnp.exp2(s_scr[slot, c] - m_new).astype(BF16)
            out.append((m_new, jnp.exp2(m_old - m_new), acc))
        return tuple(out), next_max

    def diagonal(kj, slot, state):
        carry, _ = state
        for c, (u, h, sub) in enumerate(chains):
            m_old, alpha_prev, acc = carry[c]
            acc = acc * alpha_prev + pv_block(jnp.maximum(kj - 1, 0), c)
            lo = QUERY_LANES * sub
            vis = lo + QUERY_LANES
            mid = s_scr[slot, c, lo:vis, :] + causal
            m_new = jnp.maximum(m_old, jnp.max(mid, axis=0, keepdims=True))
            if lo:
                top = s_scr[slot, c, 0:lo, :]
                m_new = jnp.maximum(m_new, jnp.max(top, axis=0, keepdims=True))
                p = jnp.concatenate([jnp.exp2(top - m_new).astype(BF16),
                                     jnp.exp2(mid - m_new).astype(BF16)], axis=0)
            else:
                p = jnp.exp2(mid - m_new).astype(BF16)
            v_rows = vt_ref[0, kj, LANES * u + hd * h:LANES * u + hd * (h + 1), 0:vis]
            pv = _dot(jnp.concatenate([v_rows, ones_block(vis)], axis=0), p)
            acc = acc * jnp.exp2(m_old - m_new) + pv
            ot_ref[0, LANES * u + hd * h:LANES * u + hd * (h + 1),
                   QUERY_LANES * sub:QUERY_LANES * (sub + 1)] = acc[0:hd] / acc[hd:hd + 1]

    row = lambda value: jnp.full((1, QUERY_LANES), value, F32)
    carry = tuple((row(MASK_VALUE), row(1.0), jnp.zeros((hd + 16, QUERY_LANES), F32)) for _ in chains)
    state = (carry, tuple(row(0.0) for _ in chains))
    n_unmasked = n_blocks - 1 - k_first
    has_unmasked = jnp.minimum(n_unmasked, 1)

    def first(_, st):
        return step(k_first, 0, (st[0], issue_scores(k_first, 0)), has_prev=False)

    def only_diagonal(_, st):
        p_scr[...] = jnp.zeros_like(p_scr)
        return st[0], issue_scores(k_first, 0)

    def pair(i, st):
        kj = k_first + 1 + 2 * i
        return step(kj + 1, 0, step(kj, 1, st))

    state = lax.fori_loop(0, has_unmasked, first, state)
    state = lax.fori_loop(0, 1 - has_unmasked, only_diagonal, state)
    n_rest = jnp.maximum(n_unmasked - 1, 0)
    state = lax.fori_loop(0, n_rest // 2, pair, state)
    state = lax.fori_loop(0, lax.rem(n_rest, 2), lambda _, st: step(n_blocks - 2, 1, st), state)
    diagonal(n_blocks - 1, lax.rem(n_unmasked, 2), state)


def _fox_prune_table(kst, qst, heads_per_step):
    kn = jnp.sqrt(kst[:, :, 0, :FOX_HEADS]) * NORM_SLACK
    qn = jnp.sqrt(qst[:, :, :, 0]) * NORM_SLACK
    c_first = kst[:, :, 1, :FOX_HEADS] * LOG2E
    c_last = kst[:, :, 2, :FOX_HEADS] * LOG2E
    bound = (qn[:, :, None, :] * kn[:, None, :, :] + c_first[:, :, None, :] - c_last[:, None, :, :]
             + (qn * kn)[:, :, None, :])
    n = kst.shape[1]
    earlier = jnp.arange(n)[None, :, None] < jnp.arange(n)[:, None, None]
    skip = (bound < -PRUNE_LOG2) & earlier[None]
    block = jnp.arange(n, dtype=jnp.int32)[None, None, :, None]
    lead = jnp.min(jnp.where(skip, n, block), axis=2)
    lead = jnp.min(lead.reshape(lead.shape[0], n, FOX_HEADS // heads_per_step, heads_per_step), axis=3)
    return jnp.transpose(lead, (0, 2, 1)).astype(jnp.int32)


def _fox_attn(kaug, qt, vt, augq, kstart, *, tq, tk, pairs_per_step):
    b, s, _ = kaug.shape
    width = FOX_HEADS * FOX_HEAD_DIM
    nk = s // tk
    pps = pairs_per_step
    n_chains = 2 * pps * (tq // QUERY_LANES)
    grid_spec = pltpu.PrefetchScalarGridSpec(
        num_scalar_prefetch=1,
        grid=(b, FOX_PAIRS // pps, s // tq),
        in_specs=[
            pl.BlockSpec((1, s, 2 * LANES * pps), lambda bi, p, qi, ks: (bi, 0, p)),
            pl.BlockSpec((1, nk, LANES * pps, tk), lambda bi, p, qi, ks: (bi, 0, p, 0)),
            pl.BlockSpec((1, LANES * pps, tq), lambda bi, p, qi, ks: (bi, p, qi)),
            pl.BlockSpec((1, 2 * AUG_SLOTS_PER_HEAD * pps, tq), lambda bi, p, qi, ks: (bi, p, qi)),
        ],
        out_specs=pl.BlockSpec((1, LANES * pps, tq), lambda bi, p, qi, ks: (bi, p, qi)),
        scratch_shapes=[pltpu.VMEM((2, n_chains, tk, QUERY_LANES), F32),
                        pltpu.VMEM((n_chains, tk, QUERY_LANES), BF16)],
    )
    return pl.pallas_call(
        functools.partial(_fox_attn_kernel, tq=tq, tk=tk, pairs_per_step=pps),
        grid_spec=grid_spec,
        out_shape=jax.ShapeDtypeStruct((b, width, s), F32),
        compiler_params=_cparams(3),
        name="fox_attn",
    )(kstart, kaug, vt, qt, augq)


def _ple(h1, p, wup_ref, wgate_ref):
    gate = jax.nn.sigmoid(_dot(h1.astype(BF16), wgate_ref[...]))
    up = _dot(p.astype(BF16), wup_ref[...])
    return h1 + up * gate


def _fox_out_kernel(ot_ref, zt_ref, x_ref, p_ref, wout_ref, wup_ref, wgate_ref, h_ref):
    z = zt_ref[0].astype(F32)
    gt = (ot_ref[0] * (z * jax.nn.sigmoid(z))).astype(BF16)
    y = lax.dot_general(gt, wout_ref[...], TN_DIMS, preferred_element_type=F32)
    h_ref[0] = _ple(x_ref[0] + y, p_ref[0], wup_ref, wgate_ref)


def _fox_out(ot, zt, x, p, w_out, w_up, w_gate, *, tm):
    b, s, d = x.shape
    width = ot.shape[1]
    pd = p.shape[-1]
    const = lambda shape: pl.BlockSpec(shape, lambda bi, i: (0,) * len(shape))
    return pl.pallas_call(
        _fox_out_kernel,
        grid=(b, s // tm),
        in_specs=[
            pl.BlockSpec((1, width, tm), lambda bi, i: (bi, 0, i)),
            pl.BlockSpec((1, width, tm), lambda bi, i: (bi, 0, i)),
            pl.BlockSpec((1, tm, d), lambda bi, i: (bi, i, 0)),
            pl.BlockSpec((1, tm, pd), lambda bi, i: (bi, i, 0)),
            const((width, d)), const((pd, d)), const((d, d)),
        ],
        out_specs=pl.BlockSpec((1, tm, d), lambda bi, i: (bi, i, 0)),
        out_shape=jax.ShapeDtypeStruct((b, s, d), F32),
        compiler_params=_cparams(2),
        name="fox_out",
    )(ot, zt, x, p, w_out.astype(BF16), w_up.astype(BF16), w_gate.astype(BF16))


PERM_ROWS = 256


def _phase_major_matrix(dilation):
    n = PERM_ROWS // dilation
    p = np.zeros((PERM_ROWS, PERM_ROWS), np.float32)
    for r in range(dilation):
        for i in range(n):
            p[r * n + i, dilation * i + r] = 1.0
    return p


def _dil_in_kernel(h_ref, g_ref, w_ref, *rest, dilations, tm):
    n_perm = sum(d > 1 for d in dilations)
    perm_refs, out_refs = rest[:n_perm], rest[n_perm:]
    hn = _rms_norm(h_ref[0], g_ref[...]).astype(BF16)
    cols = w_ref.shape[1] // len(dilations)
    perm_refs = list(perm_refs)
    for g, (dil, o_ref) in enumerate(zip(dilations, out_refs)):
        w = w_ref[:, cols * g:cols * (g + 1)]
        if dil == 1:
            o_ref[0] = _dot(hn, w).astype(BF16)
            continue
        perm = perm_refs.pop(0)[...]
        n = PERM_ROWS // dil
        for c in range(tm // PERM_ROWS):
            rows = slice(PERM_ROWS * c, PERM_ROWS * (c + 1))
            hp = _dot(perm, hn[rows]).astype(BF16)
            res = _dot(hp, w).astype(BF16)
            for r in range(dil):
                o_ref[0, n * c:n * (c + 1), cols * r:cols * (r + 1)] = res[n * r:n * (r + 1)]


def _dil_in(h, norm_g, w, dilations, *, tm):
    b, s, d = h.shape
    cols = w.shape[1] // len(dilations)
    perms = [jnp.asarray(_phase_major_matrix(dil), BF16) for dil in dilations if dil > 1]
    const = lambda shape: pl.BlockSpec(shape, lambda bi, i: (0,) * len(shape))
    return pl.pallas_call(
        functools.partial(_dil_in_kernel, dilations=dilations, tm=tm),
        grid=(b, s // tm),
        in_specs=[pl.BlockSpec((1, tm, d), lambda bi, i: (bi, i, 0)), const((1, d)),
                  const(w.shape)] + [const(p.shape) for p in perms],
        out_specs=[pl.BlockSpec((1, tm // dil, dil * cols), lambda bi, i: (bi, i, 0))
                   for dil in dilations],
        out_shape=[jax.ShapeDtypeStruct((b, s // dil, dil * cols), BF16) for dil in dilations],
        compiler_params=_cparams(2),
        name="dil_in_" + "_".join(str(dil) for dil in dilations),
    )(h, norm_g.reshape(1, d), w.astype(BF16), *perms)


def _dil_attn_kernel(q_ref, kc_ref, vc_ref, kp_ref, vp_ref, o_ref, st_ref,
                     kcat, vcat, bias_scr, *, dilation, kb, neg_slopes):
    n = pl.program_id(2)
    nw = DIL_WINDOW_STEPS
    hd = DIL_HEAD_DIM

    @pl.when((pl.program_id(0) == 0) & (pl.program_id(1) == 0) & (n == 0))
    def _():
        row = lax.broadcasted_iota(jnp.int32, (nw, 2 * nw), 0)
        col = lax.broadcasted_iota(jnp.int32, (nw, 2 * nw), 1)
        dist = nw + row - col
        band = (dist >= 0) & (dist <= nw)
        dist_f = (dist * dilation).astype(F32)
        for h in range(DIL_HEADS_PER_GROUP):
            bias_scr[h] = jnp.where(band, neg_slopes[h] * dist_f * LOG2E, MASK_VALUE)

    kcat[0:nw] = kp_ref[0]
    kcat[nw:nw * (kb + 1)] = kc_ref[0]
    vcat[0:nw] = vp_ref[0]
    vcat[nw:nw * (kb + 1)] = vc_ref[0]
    lane = lax.broadcasted_iota(jnp.int32, (nw, LANES), 1)
    col = lax.broadcasted_iota(jnp.int32, (nw, 2 * nw), 1)
    no_history = jnp.where(col < jnp.where(n == 0, nw, 0), MASK_VALUE, 0.0)

    heads = [slice(hd * h, hd * (h + 1)) for h in range(DIL_HEADS_PER_GROUP)]
    scores = [[lax.dot_general(q_ref[0, nw * jb:nw * (jb + 1), hs],
                               kcat[nw * jb:nw * (jb + 2), hs], NT_DIMS,
                               preferred_element_type=F32) for hs in heads] for jb in range(kb)]
    for jb in range(kb):
        rows = slice(nw * jb, nw * (jb + 1))
        stat_tile = jnp.zeros((nw, LANES), F32)
        for h, hs in enumerate(heads):
            t = scores[jb][h] + bias_scr[h]
            if jb == 0:
                t = t + no_history
            m = jnp.max(t, axis=1, keepdims=True)
            p = jnp.exp2(t - m)
            l = jnp.sum(p, axis=1, keepdims=True)
            o_ref[0, rows, hs] = _dot(p.astype(BF16), vcat[nw * jb:nw * (jb + 2), hs]).astype(BF16)
            stat_tile = jnp.where(lane == h, m, stat_tile)
            stat_tile = jnp.where(lane == h + DIL_HEADS_PER_GROUP, l, stat_tile)
        st_ref[0, rows, :] = stat_tile


def _alibi_neg_slopes(group):
    n = len(DIL_PATTERN) * DIL_HEADS_PER_GROUP
    k = np.arange(1, n + 1, dtype=np.float32)
    slopes = np.float32(2.0) ** (np.float32(-ALIBI_MAX_EXP) * k / np.float32(n))
    lo = group * DIL_HEADS_PER_GROUP
    return tuple(float(-v) for v in slopes[lo:lo + DIL_HEADS_PER_GROUP])


def _dil_attn(view, group, *, kb):
    _, dilation = DIL_PATTERN[group]
    b, length, width = view.shape
    nw = DIL_WINDOW_STEPS
    gw = DIL_HEADS_PER_GROUP * DIL_HEAD_DIM
    ncb = width // dilation // gw
    rows = nw * kb
    qc, kc, vc = 0, 1, 2

    def cur(cb):
        return pl.BlockSpec((1, rows, gw), lambda bi, r, n: (bi, n, r * ncb + cb))

    def prev(cb):
        return pl.BlockSpec((1, nw, gw),
                            lambda bi, r, n: (bi, jnp.maximum(n * kb - 1, 0), r * ncb + cb))

    o, st = pl.pallas_call(
        functools.partial(_dil_attn_kernel, dilation=dilation, kb=kb,
                          neg_slopes=_alibi_neg_slopes(group)),
        grid=(b, dilation, length // rows),
        in_specs=[cur(qc), cur(kc), cur(vc), prev(kc), prev(vc)],
        out_specs=[
            pl.BlockSpec((1, rows, gw), lambda bi, r, n: (bi, n, r)),
            pl.BlockSpec((1, rows, LANES), lambda bi, r, n: (bi, n, r)),
        ],
        out_shape=[
            jax.ShapeDtypeStruct((b, length, dilation * gw), BF16),
            jax.ShapeDtypeStruct((b, length, dilation * LANES), F32),
        ],
        scratch_shapes=[pltpu.VMEM((nw * (kb + 1), gw), BF16), pltpu.VMEM((nw * (kb + 1), gw), BF16),
                        pltpu.VMEM((DIL_HEADS_PER_GROUP, nw, 2 * nw), F32)],
        compiler_params=_cparams(3),
        name=f"dil_attn_g{group}",
    )(view, view, view, view, view)
    return o, st


def _natural_rows(blk, perm_t, dilation, tm):
    cols = blk.shape[1] // dilation
    if dilation == 1:
        return blk
    n = PERM_ROWS // dilation
    chunks = []
    for c in range(tm // PERM_ROWS):
        phase_major = jnp.concatenate(
            [blk[n * c:n * (c + 1), cols * r:cols * (r + 1)] for r in range(dilation)], axis=0)
        chunks.append(_dot(perm_t, phase_major))
    return jnp.concatenate(chunks, axis=0)


def _dil_out_kernel(o0_ref, o1_ref, o2_ref, s0_ref, s1_ref, s2_ref, z_ref, h_ref, p_ref,
                    pt1_ref, pt2_ref, e_ref, wout_ref, wup_ref, wgate_ref, g_ref,
                    out_ref, *, dilations, tm):
    perm_ts = (None, pt1_ref[...], pt2_ref[...])
    n_heads = DIL_HEADS_PER_GROUP
    row_max, row_sum = [], []
    for st_ref, perm_t, dil in zip((s0_ref, s1_ref, s2_ref), perm_ts, dilations):
        if dil == 1:
            st = st_ref[0]
        else:
            hi, mid, lo = (_natural_rows(piece.astype(BF16), perm_t, dil, tm)
                           for piece in _split3(st_ref[0]))
            st = hi + mid + lo
        row_max.append(st)
        row_sum.append(pltpu.roll(st, LANES - n_heads, 1))
    m = jnp.maximum(jnp.maximum(row_max[0], row_max[1]), row_max[2])
    e = [jnp.exp2(v - m) for v in row_max]
    inv = 1.0 / (e[0] * row_sum[0] + e[1] * row_sum[1] + e[2] * row_sum[2])
    live = lax.broadcasted_iota(jnp.int32, (tm, LANES), 1) < n_heads
    o = None
    for ev, o_ref, perm_t, dil in zip(e, (o0_ref, o1_ref, o2_ref), perm_ts, dilations):
        w = jnp.where(live, ev * inv, 0.0)
        w_hi = w.astype(BF16)
        w_lo = (w - w_hi.astype(F32)).astype(BF16)
        w_full = _dot(jnp.concatenate([w_hi, w_lo], axis=1), e_ref[...])
        term = w_full * _natural_rows(o_ref[0], perm_t, dil, tm)
        o = term if o is None else o + term
    z = z_ref[0].astype(F32)
    g = (o * (z * jax.nn.sigmoid(z))).astype(BF16)
    h1 = h_ref[0] + _dot(g, wout_ref[...])
    out_ref[0] = _rms_norm(_ple(h1, p_ref[0], wup_ref, wgate_ref), g_ref[...])


def _dil_out(outs, stats, proj0, h, p, w_out, w_up, w_gate, final_g, *, tm):
    b, s, d = h.shape
    gw = DIL_HEADS_PER_GROUP * DIL_HEAD_DIM
    pd = p.shape[-1]
    dilations = tuple(dil for _, dil in DIL_PATTERN)
    z_block = proj0.shape[-1] // gw - 1
    expand = np.zeros((2 * LANES, gw), np.float32)
    for hh in range(DIL_HEADS_PER_GROUP):
        expand[hh, DIL_HEAD_DIM * hh:DIL_HEAD_DIM * (hh + 1)] = 1.0
        expand[LANES + hh, DIL_HEAD_DIM * hh:DIL_HEAD_DIM * (hh + 1)] = 1.0
    perm_ts = [jnp.asarray(_phase_major_matrix(dil).T, BF16) for dil in dilations[1:]]
    const = lambda shape: pl.BlockSpec(shape, lambda bi, i: (0,) * len(shape))
    row = lambda w: pl.BlockSpec((1, tm, w), lambda bi, i: (bi, i, 0))
    view = lambda w: [pl.BlockSpec((1, tm // dil, dil * w), lambda bi, i: (bi, i, 0))
                      for dil in dilations]
    return pl.pallas_call(
        functools.partial(_dil_out_kernel, dilations=dilations, tm=tm),
        grid=(b, s // tm),
        in_specs=view(gw) + view(LANES) + [
            pl.BlockSpec((1, tm, gw), lambda bi, i: (bi, i, z_block)),
            row(d), row(pd),
            const((PERM_ROWS, PERM_ROWS)), const((PERM_ROWS, PERM_ROWS)),
            const((2 * LANES, gw)), const((gw, d)), const((pd, d)), const((d, d)), const((1, d)),
        ],
        out_specs=row(d),
        out_shape=jax.ShapeDtypeStruct((b, s, d), F32),
        compiler_params=_cparams(2),
        name="dil_out",
    )(*outs, *stats, proj0, h, p, *perm_ts, jnp.asarray(expand, BF16),
      w_out.astype(BF16), w_up.astype(BF16), w_gate.astype(BF16), final_g.reshape(1, d))


def kernel(x, p, fox_norm, fox_w_in, fox_b_f, fox_w_out, dil_norm, dil_w_in, dil_w_out,
           ple_w_up, ple_w_gate, final_norm):
    s = x.shape[1]
    tm = min(512, s)
    tq = min(512, s)
    tk = min(512, s)
    assert tm == tq == tk, "the pruning table pairs fox_in tiles with query tiles and key blocks"
    kaug, qt, vt, zt, augq, kst, qst = _fox_in(x, fox_norm[0], fox_w_in[0], fox_b_f[0], tm=tm, ch=tk)
    pps = 2
    ot = _fox_attn(kaug, qt, vt, augq, _fox_prune_table(kst, qst, 2 * pps), tq=tq, tk=tk,
                   pairs_per_step=pps)
    h = _fox_out(ot, zt, x, p[0], fox_w_out[0], ple_w_up[0], ple_w_gate[0], tm=tm)

    gw = DIL_HEADS_PER_GROUP * DIL_HEAD_DIM
    n_groups = len(DIL_PATTERN)
    w1 = dil_w_in[0]
    k_scale = DIL_HEAD_DIM ** -0.5 * LOG2E
    qkv = lambda g: [w1[:, (n_groups * part + g) * gw:(n_groups * part + g + 1) * gw]
                     * (k_scale if part == 1 else 1.0) for part in range(3)]
    w_natural = jnp.concatenate(qkv(0) + [w1[:, 3 * n_groups * gw:]], axis=1)
    w_phased = jnp.concatenate(qkv(1) + qkv(2), axis=1)
    (proj0,) = _dil_in(h, dil_norm[0], w_natural, (1,), tm=tm)
    views = [proj0] + list(_dil_in(h, dil_norm[0], w_phased,
                                   (DIL_PATTERN[1][1], DIL_PATTERN[2][1]), tm=min(PERM_ROWS, s)))
    outs, stats = [], []
    for group, (_, dilation) in enumerate(DIL_PATTERN):
        kb = min(8, s // dilation // DIL_WINDOW_STEPS)
        o, st = _dil_attn(views[group], group, kb=kb)
        outs.append(o)
        stats.append(st)
    return _dil_out(outs, stats, proj0, h, p[1], dil_w_out[0], ple_w_up[1], ple_w_gate[1],
                    final_norm, tm=tm)
```

```python
import functools

import numpy as np
import jax
import jax.numpy as jnp
from jax import lax
from jax.experimental import pallas as pl
from jax.experimental.pallas import tpu as pltpu

F32 = jnp.float32
BF16 = jnp.bfloat16

RMS_EPS = 1e-6
FOX_HEADS = 16
FOX_HEAD_DIM = 64
FOX_PAIRS = FOX_HEADS // 2
DIL_PATTERN = ((128, 1), (512, 4), (2048, 16))
DIL_HEADS_PER_GROUP = 8
DIL_HEAD_DIM = 128
DIL_WINDOW_STEPS = 128
ALIBI_MAX_EXP = 8.0
MASK_VALUE = -1e30
LOG2E = 1.4426950408889634

LANES = 128
AUG_SLOTS_PER_HEAD = 16
QUERY_LANES = 256
PRUNE_LOG2 = 50.0
NORM_SLACK = 1.02
VMEM_LIMIT_BYTES = 56 * 1024 * 1024

NT_DIMS = (((1,), (1,)), ((), ()))
TN_DIMS = (((0,), (0,)), ((), ()))


def _cparams(n_axes):
    return pltpu.CompilerParams(
        dimension_semantics=("arbitrary",) * n_axes,
        vmem_limit_bytes=VMEM_LIMIT_BYTES,
    )


def _rms_norm(x, g):
    ms = jnp.mean(x * x, axis=-1, keepdims=True)
    return x * lax.rsqrt(ms + RMS_EPS) * g


def _log_sigmoid(x):
    return jnp.minimum(x, 0.0) - jnp.log1p(jnp.exp(-jnp.abs(x)))


def _split3(x):
    hi = x.astype(BF16).astype(F32)
    r1 = x - hi
    mid = r1.astype(BF16).astype(F32)
    lo = (r1 - mid).astype(BF16).astype(F32)
    return hi, mid, lo


def _dot(a, b):
    return jnp.dot(a, b, preferred_element_type=F32)


def _fox_in_kernel(x_ref, g_ref, wk_ref, wt_ref, wft_ref, bt_ref,
                   triu_ref, pk_ref, pqt_ref, gk_ref, gq_ref,
                   kaug_ref, qt_ref, vt_ref, zt_ref, augq_ref, kst_ref, qst_ref,
                   carryt_ref, *, tm, ch):
    i = pl.program_id(1)

    @pl.when(i == 0)
    def _():
        carryt_ref[...] = jnp.zeros_like(carryt_ref)

    hn = _rms_norm(x_ref[0], g_ref[...]).astype(BF16)
    width = FOX_HEADS * FOX_HEAD_DIM

    ft = lax.dot_general(wft_ref[...], hn, NT_DIMS, preferred_element_type=F32)
    hi, mid, lo = _split3(_log_sigmoid(ft + bt_ref[...]))
    stack = jnp.concatenate([hi, mid, lo], axis=0).astype(BF16)
    cs = _dot(stack, triu_ref[...])
    ct = cs[0:16] + cs[16:32] + cs[32:48] + carryt_ref[:, 0:1]
    carryt_ref[...] = jnp.broadcast_to(ct[:, tm - 1:tm], carryt_ref.shape)
    c = jnp.concatenate([ct, jnp.zeros((LANES - FOX_HEADS, tm), F32)], axis=0).T

    k = _dot(hn, wk_ref[...])
    nhi, nmid, nlo = _split3(c * -LOG2E)
    lane = lax.broadcasted_iota(jnp.int32, (tm, LANES), 1)
    slots = jnp.where(lane < 16, nhi,
                      jnp.where(lane < 32, pltpu.roll(nmid, 16, 1),
                                jnp.where(lane < 48, pltpu.roll(nlo, 32, 1),
                                          jnp.where(lane == 48, 1.0, 0.0))))
    kaug = _dot(slots.astype(BF16), pk_ref[...])
    kb = k.astype(BF16)
    for p in range(FOX_PAIRS):
        kaug_ref[0, :, 2 * LANES * p:2 * LANES * p + LANES] = kb[:, LANES * p:LANES * (p + 1)]
        kaug_ref[0, :, 2 * LANES * p + LANES:2 * LANES * (p + 1)] = (
            kaug[:, LANES * p:LANES * (p + 1)].astype(BF16))
    kf = kb.astype(F32)
    kn2 = jnp.max(_dot((kf * kf).astype(BF16), gk_ref[...]), axis=0, keepdims=True)
    kst_ref[0, 0] = jnp.concatenate(
        [kn2, c[0:1, :], c[tm - 1:tm, :], jnp.zeros((5, LANES), F32)], axis=0)

    rows = 512
    for c0 in range(0, 3 * width, rows):
        res = lax.dot_general(wt_ref[c0:c0 + rows, :], hn, NT_DIMS,
                              preferred_element_type=F32).astype(BF16)
        which, off = divmod(c0, width)
        if which == 0:
            qt_ref[0, off:off + rows, :] = res
        elif which == 1:
            for j in range(tm // ch):
                vt_ref[0, j, off:off + rows, :] = res[:, j * ch:(j + 1) * ch]
        else:
            zt_ref[0, off:off + rows, :] = res

    hi, mid, lo = _split3(ct * LOG2E)
    ones = jnp.where(lax.broadcasted_iota(jnp.int32, (16, tm), 0) == 0, 1.0, 0.0)
    slots_t = jnp.concatenate([hi, mid, lo, ones], axis=0).astype(BF16)
    augq_ref[0] = _dot(pqt_ref[...], slots_t).astype(BF16)
    qf = qt_ref[0].astype(F32)
    qn2 = jnp.max(_dot(gq_ref[...], (qf * qf).astype(BF16)), axis=1, keepdims=True)
    qst_ref[0, 0] = jnp.broadcast_to(qn2, (FOX_HEADS, LANES))


def _placement_matrices():
    pk = np.zeros((LANES, FOX_PAIRS * LANES), np.float32)
    pqt = np.zeros((FOX_HEADS * AUG_SLOTS_PER_HEAD, 64), np.float32)
    for h in range(FOX_HEADS):
        p, odd = divmod(h, 2)
        base = 6 * odd
        for piece in range(3):
            pk[48, LANES * p + base + piece] = 1.0
            pk[16 * piece + h, LANES * p + base + 3 + piece] = 1.0
            pqt[AUG_SLOTS_PER_HEAD * h + base + piece, 16 * piece + h] = 1.0
            pqt[AUG_SLOTS_PER_HEAD * h + base + 3 + piece, 48] = 1.0
    return jnp.asarray(pk, BF16), jnp.asarray(pqt, BF16)


def _fox_in(x, norm_g, w_in, b_f, *, tm, ch):
    b, s, d = x.shape
    width = FOX_HEADS * FOX_HEAD_DIM
    wq = w_in[:, :width] * (FOX_HEAD_DIM ** -0.5 * LOG2E)
    wk = w_in[:, width:2 * width]
    wv = w_in[:, 2 * width:3 * width]
    wz = w_in[:, 3 * width:4 * width]
    wf = w_in[:, 4 * width:]
    wt = jnp.concatenate([wq, wv, wz], axis=1).T.astype(BF16)
    wft = wf.T.astype(BF16)
    bt = b_f.reshape(FOX_HEADS, 1)
    r = np.arange(tm)
    triu = jnp.asarray(r[:, None] <= r[None, :], BF16)
    pk, pqt = _placement_matrices()
    head_of = np.arange(width) // FOX_HEAD_DIM
    gk = jnp.asarray(head_of[:, None] == np.arange(LANES)[None, :], BF16)
    gq = jnp.asarray(np.arange(FOX_HEADS)[:, None] == head_of[None, :], BF16)

    const = lambda shape: pl.BlockSpec(shape, lambda bi, i: (0,) * len(shape))
    return pl.pallas_call(
        functools.partial(_fox_in_kernel, tm=tm, ch=ch),
        grid=(b, s // tm),
        in_specs=[
            pl.BlockSpec((1, tm, d), lambda bi, i: (bi, i, 0)),
            const((1, d)),
            const(wk.shape), const(wt.shape), const(wft.shape),
            const((FOX_HEADS, 1)),
            const((tm, tm)),
            const(pk.shape), const(pqt.shape), const(gk.shape), const(gq.shape),
        ],
        out_specs=[
            pl.BlockSpec((1, tm, 2 * width), lambda bi, i: (bi, i, 0)),
            pl.BlockSpec((1, width, tm), lambda bi, i: (bi, 0, i)),
            pl.BlockSpec((1, tm // ch, width, ch), lambda bi, i: (bi, i, 0, 0)),
            pl.BlockSpec((1, width, tm), lambda bi, i: (bi, 0, i)),
            pl.BlockSpec((1, FOX_HEADS * AUG_SLOTS_PER_HEAD, tm), lambda bi, i: (bi, 0, i)),
            pl.BlockSpec((1, 1, 8, LANES), lambda bi, i: (bi, i, 0, 0)),
            pl.BlockSpec((1, 1, FOX_HEADS, LANES), lambda bi, i: (bi, i, 0, 0)),
        ],
        out_shape=[
            jax.ShapeDtypeStruct((b, s, 2 * width), BF16),
            jax.ShapeDtypeStruct((b, width, s), BF16),
            jax.ShapeDtypeStruct((b, s // ch, width, ch), BF16),
            jax.ShapeDtypeStruct((b, width, s), BF16),
            jax.ShapeDtypeStruct((b, FOX_HEADS * AUG_SLOTS_PER_HEAD, s), BF16),
            jax.ShapeDtypeStruct((b, s // tm, 8, LANES), F32),
            jax.ShapeDtypeStruct((b, s // tm, FOX_HEADS, LANES), F32),
        ],
        scratch_shapes=[pltpu.VMEM((FOX_HEADS, LANES), F32)],
        compiler_params=_cparams(2),
        name="fox_in",
    )(x, norm_g.reshape(1, d), wk.astype(BF16), wt, wft, bt, triu, pk, pqt, gk, gq)


def _fox_attn_kernel(kstart_ref, kaug_ref, vt_ref, qt_ref, augq_ref, ot_ref,
                     s_scr, p_scr, *, tq, tk, pairs_per_step):
    qi = pl.program_id(2)
    hd = FOX_HEAD_DIM
    zeros_hd = jnp.zeros((hd, tq), BF16)
    zeros_tail = jnp.zeros((LANES - AUG_SLOTS_PER_HEAD, tq), BF16)
    ones_block = lambda n: jnp.where(lax.broadcasted_iota(jnp.int32, (16, n), 0) == 0, 1.0, 0.0).astype(BF16)
    ones_rows = ones_block(tk)
    n_blocks = (qi * tq) // tk + 1
    k_first = kstart_ref[pl.program_id(0), pl.program_id(1), qi]

    heads = [(u, h) for u in range(pairs_per_step) for h in range(2)]
    w_head = {}
    for u, h in heads:
        q_rows = qt_ref[0, LANES * u + hd * h:LANES * u + hd * (h + 1), :]
        aug_rows = augq_ref[0, 2 * AUG_SLOTS_PER_HEAD * u + AUG_SLOTS_PER_HEAD * h:
                            2 * AUG_SLOTS_PER_HEAD * u + AUG_SLOTS_PER_HEAD * (h + 1), :]
        q_part = [q_rows, zeros_hd] if h == 0 else [zeros_hd, q_rows]
        w_head[u, h] = jnp.concatenate(q_part + [aug_rows, zeros_tail], axis=0)
    chains = [(u, h, sub) for u, h in heads for sub in range(tq // QUERY_LANES)]

    def issue_scores(kj, slot):
        ks = pl.multiple_of(kj * tk, tk)
        block_max = []
        for c, (u, h, sub) in enumerate(chains):
            kb = kaug_ref[0, pl.ds(ks, tk), 2 * LANES * u:2 * LANES * (u + 1)]
            s = _dot(kb, w_head[u, h][:, QUERY_LANES * sub:QUERY_LANES * (sub + 1)])
            s_scr[slot, c] = s
            block_max.append(jnp.max(s, axis=0, keepdims=True))
        return tuple(block_max)

    tri_key = lax.broadcasted_iota(jnp.int32, (QUERY_LANES, QUERY_LANES), 0)
    tri_qry = lax.broadcasted_iota(jnp.int32, (QUERY_LANES, QUERY_LANES), 1)
    causal = jnp.where(tri_key <= tri_qry, 0.0, MASK_VALUE)

    def pv_block(kj, c):
        u, h, _ = chains[c]
        v_rows = vt_ref[0, kj, LANES * u + hd * h:LANES * u + hd * (h + 1), :]
        return _dot(jnp.concatenate([v_rows, ones_rows], axis=0), p_scr[c])

    def step(kj, slot, state, has_prev=True):
        carry, block_max = state
        next_max = issue_scores(kj + 1, 1 - slot)
        out = []
        for c in range(len(chains)):
            m_old, alpha_prev, acc = carry[c]
            if has_prev:
                acc = acc * alpha_prev + pv_block(kj - 1, c)
            m_new = jnp.maximum(m_old, block_max[c])
            p_scr[c] = jnp.exp2(s_scr[slot, c] - m_new).astype(BF16)
            out.append((m_new, jnp.exp2(m_old - m_new), acc))
        return tuple(out), next_max

    def diagonal(kj, slot, state):
        carry, _ = state
        for c, (u, h, sub) in enumerate(chains):
            m_old, alpha_prev, acc = carry[c]
            acc = acc * alpha_prev + pv_block(jnp.maximum(kj - 1, 0), c)
            lo = QUERY_LANES * sub
            vis = lo + QUERY_LANES
            mid = s_scr[slot, c, lo:vis, :] + causal
            m_new = jnp.maximum(m_old, jnp.max(mid, axis=0, keepdims=True))
            if lo:
                top = s_scr[slot, c, 0:lo, :]
                m_new = jnp.maximum(m_new, jnp.max(top, axis=0, keepdims=True))
                p = jnp.concatenate([jnp.exp2(top - m_new).astype(BF16),
                                     jnp.exp2(mid - m_new).astype(BF16)], axis=0)
            else:
                p = jnp.exp2(mid - m_new).astype(BF16)
            v_rows = vt_ref[0, kj, LANES * u + hd * h:LANES * u + hd * (h + 1), 0:vis]
            pv = _dot(jnp.concatenate([v_rows, ones_block(vis)], axis=0), p)
            acc = acc * jnp.exp2(m_old - m_new) + pv
            ot_ref[0, LANES * u + hd * h:LANES * u + hd * (h + 1),
                   QUERY_LANES * sub:QUERY_LANES * (sub + 1)] = acc[0:hd] / acc[hd:hd + 1]

    row = lambda value: jnp.full((1, QUERY_LANES), value, F32)
    carry = tuple((row(MASK_VALUE), row(1.0), jnp.zeros((hd + 16, QUERY_LANES), F32)) for _ in chains)
    state = (carry, tuple(row(0.0) for _ in chains))
    n_unmasked = n_blocks - 1 - k_first
    has_unmasked = jnp.minimum(n_unmasked, 1)

    def first(_, st):
        return step(k_first, 0, (st[0], issue_scores(k_first, 0)), has_prev=False)

    def only_diagonal(_, st):
        p_scr[...] = jnp.zeros_like(p_scr)
        return st[0], issue_scores(k_first, 0)

    def pair(i, st):
        kj = k_first + 1 + 2 * i
        return step(kj + 1, 0, step(kj, 1, st))

    state = lax.fori_loop(0, has_unmasked, first, state)
    state = lax.fori_loop(0, 1 - has_unmasked, only_diagonal, state)
    n_rest = jnp.maximum(n_unmasked - 1, 0)
    state = lax.fori_loop(0, n_rest // 2, pair, state)
    state = lax.fori_loop(0, lax.rem(n_rest, 2), lambda _, st: step(n_blocks - 2, 1, st), state)
    diagonal(n_blocks - 1, lax.rem(n_unmasked, 2), state)


def _fox_prune_table(kst, qst, heads_per_step):
    kn = jnp.sqrt(kst[:, :, 0, :FOX_HEADS]) * NORM_SLACK
    qn = jnp.sqrt(qst[:, :, :, 0]) * NORM_SLACK
    c_first = kst[:, :, 1, :FOX_HEADS] * LOG2E
    c_last = kst[:, :, 2, :FOX_HEADS] * LOG2E
    bound = (qn[:, :, None, :] * kn[:, None, :, :] + c_first[:, :, None, :] - c_last[:, None, :, :]
             + (qn * kn)[:, :, None, :])
    n = kst.shape[1]
    earlier = jnp.arange(n)[None, :, None] < jnp.arange(n)[:, None, None]
    skip = (bound < -PRUNE_LOG2) & earlier[None]
    block = jnp.arange(n, dtype=jnp.int32)[None, None, :, None]
    lead = jnp.min(jnp.where(skip, n, block), axis=2)
    lead = jnp.min(lead.reshape(lead.shape[0], n, FOX_HEADS // heads_per_step, heads_per_step), axis=3)
    return jnp.transpose(lead, (0, 2, 1)).astype(jnp.int32)


def _fox_attn(kaug, qt, vt, augq, kstart, *, tq, tk, pairs_per_step):
    b, s, _ = kaug.shape
    width = FOX_HEADS * FOX_HEAD_DIM
    nk = s // tk
    pps = pairs_per_step
    n_chains = 2 * pps * (tq // QUERY_LANES)
    grid_spec = pltpu.PrefetchScalarGridSpec(
        num_scalar_prefetch=1,
        grid=(b, FOX_PAIRS // pps, s // tq),
        in_specs=[
            pl.BlockSpec((1, s, 2 * LANES * pps), lambda bi, p, qi, ks: (bi, 0, p)),
            pl.BlockSpec((1, nk, LANES * pps, tk), lambda bi, p, qi, ks: (bi, 0, p, 0)),
            pl.BlockSpec((1, LANES * pps, tq), lambda bi, p, qi, ks: (bi, p, qi)),
            pl.BlockSpec((1, 2 * AUG_SLOTS_PER_HEAD * pps, tq), lambda bi, p, qi, ks: (bi, p, qi)),
        ],
        out_specs=pl.BlockSpec((1, LANES * pps, tq), lambda bi, p, qi, ks: (bi, p, qi)),
        scratch_shapes=[pltpu.VMEM((2, n_chains, tk, QUERY_LANES), F32),
                        pltpu.VMEM((n_chains, tk, QUERY_LANES), BF16)],
    )
    return pl.pallas_call(
        functools.partial(_fox_attn_kernel, tq=tq, tk=tk, pairs_per_step=pps),
        grid_spec=grid_spec,
        out_shape=jax.ShapeDtypeStruct((b, width, s), F32),
        compiler_params=_cparams(3),
        name="fox_attn",
    )(kstart, kaug, vt, qt, augq)


def _ple(h1, p, wup_ref, wgate_ref):
    gate = jax.nn.sigmoid(_dot(h1.astype(BF16), wgate_ref[...]))
    up = _dot(p.astype(BF16), wup_ref[...])
    return h1 + up * gate


def _fox_out_kernel(ot_ref, zt_ref, x_ref, p_ref, wout_ref, wup_ref, wgate_ref, h_ref):
    z = zt_ref[0].astype(F32)
    gt = (ot_ref[0] * (z * jax.nn.sigmoid(z))).astype(BF16)
    y = lax.dot_general(gt, wout_ref[...], TN_DIMS, preferred_element_type=F32)
    h_ref[0] = _ple(x_ref[0] + y, p_ref[0], wup_ref, wgate_ref)


def _fox_out(ot, zt, x, p, w_out, w_up, w_gate, *, tm):
    b, s, d = x.shape
    width = ot.shape[1]
    pd = p.shape[-1]
    const = lambda shape: pl.BlockSpec(shape, lambda bi, i: (0,) * len(shape))
    return pl.pallas_call(
        _fox_out_kernel,
        grid=(b, s // tm),
        in_specs=[
            pl.BlockSpec((1, width, tm), lambda bi, i: (bi, 0, i)),
            pl.BlockSpec((1, width, tm), lambda bi, i: (bi, 0, i)),
            pl.BlockSpec((1, tm, d), lambda bi, i: (bi, i, 0)),
            pl.BlockSpec((1, tm, pd), lambda bi, i: (bi, i, 0)),
            const((width, d)), const((pd, d)), const((d, d)),
        ],
        out_specs=pl.BlockSpec((1, tm, d), lambda bi, i: (bi, i, 0)),
        out_shape=jax.ShapeDtypeStruct((b, s, d), F32),
        compiler_params=_cparams(2),
        name="fox_out",
    )(ot, zt, x, p, w_out.astype(BF16), w_up.astype(BF16), w_gate.astype(BF16))


PERM_ROWS = 256
DIL_SUB_BLOCKS = 8


def _phase_major_matrix(dilation):
    n = PERM_ROWS // dilation
    p = np.zeros((PERM_ROWS, PERM_ROWS), np.float32)
    for r in range(dilation):
        for i in range(n):
            p[r * n + i, dilation * i + r] = 1.0
    return p


def _dil_in_kernel(h_ref, g_ref, w_ref, *rest, dilations, tm, part_cols):
    n_perm = sum(d > 1 for d in dilations)
    perm_refs, out_refs = rest[:n_perm], rest[n_perm:]
    hn = _rms_norm(h_ref[0], g_ref[...]).astype(BF16)
    cols = w_ref.shape[1] // len(dilations)
    perm_refs = list(perm_refs)
    for g, (dil, o_ref) in enumerate(zip(dilations, out_refs)):
        w = w_ref[:, cols * g:cols * (g + 1)]
        if dil == 1:
            o_ref[0] = _dot(hn, w).astype(BF16)
            continue
        perm = perm_refs.pop(0)[...]
        n = PERM_ROWS // dil
        for c in range(tm // PERM_ROWS):
            rows = slice(PERM_ROWS * c, PERM_ROWS * (c + 1))
            hp = _dot(perm, hn[rows]).astype(BF16)
            res = _dot(hp, w).astype(BF16)
            for r in range(dil):
                for part in range(cols // part_cols):
                    dst = part_cols * (part * dil + r)
                    o_ref[0, n * c:n * (c + 1), dst:dst + part_cols] = (
                        res[n * r:n * (r + 1), part_cols * part:part_cols * (part + 1)])


def _dil_in(h, norm_g, w, dilations, *, tm, part_cols):
    b, s, d = h.shape
    cols = w.shape[1] // len(dilations)
    perms = [jnp.asarray(_phase_major_matrix(dil), BF16) for dil in dilations if dil > 1]
    const = lambda shape: pl.BlockSpec(shape, lambda bi, i: (0,) * len(shape))
    return pl.pallas_call(
        functools.partial(_dil_in_kernel, dilations=dilations, tm=tm, part_cols=part_cols),
        grid=(b, s // tm),
        in_specs=[pl.BlockSpec((1, tm, d), lambda bi, i: (bi, i, 0)), const((1, d)),
                  const(w.shape)] + [const(p.shape) for p in perms],
        out_specs=[pl.BlockSpec((1, tm // dil, dil * cols), lambda bi, i: (bi, i, 0))
                   for dil in dilations],
        out_shape=[jax.ShapeDtypeStruct((b, s // dil, dil * cols), BF16) for dil in dilations],
        compiler_params=_cparams(2),
        name="dil_in_" + "_".join(str(dil) for dil in dilations),
    )(h, norm_g.reshape(1, d), w.astype(BF16), *perms)


def _dil_attn_kernel(q_ref, kc_ref, vc_ref, kp_ref, vp_ref, o_ref, st_ref,
                     kcat, vcat, bias_scr, *, dilation, kb, phases, neg_slopes):
    n = pl.program_id(2)
    nw = DIL_WINDOW_STEPS
    hd = DIL_HEAD_DIM
    gw = DIL_HEADS_PER_GROUP * hd

    @pl.when((pl.program_id(0) == 0) & (pl.program_id(1) == 0) & (n == 0))
    def _():
        row = lax.broadcasted_iota(jnp.int32, (nw, 2 * nw), 0)
        col = lax.broadcasted_iota(jnp.int32, (nw, 2 * nw), 1)
        dist = nw + row - col
        band = (dist >= 0) & (dist <= nw)
        dist_f = (dist * dilation).astype(F32)
        for h in range(DIL_HEADS_PER_GROUP):
            bias_scr[h] = jnp.where(band, neg_slopes[h] * dist_f * LOG2E, MASK_VALUE)

    for ph in range(phases):
        cols = slice(gw * ph, gw * (ph + 1))
        kcat[ph, 0:nw] = kp_ref[0, :, cols]
        kcat[ph, nw:nw * (kb + 1)] = kc_ref[0, :, cols]
        vcat[ph, 0:nw] = vp_ref[0, :, cols]
        vcat[ph, nw:nw * (kb + 1)] = vc_ref[0, :, cols]
    lane = lax.broadcasted_iota(jnp.int32, (nw, LANES), 1)
    col = lax.broadcasted_iota(jnp.int32, (nw, 2 * nw), 1)
    no_history = jnp.where(col < jnp.where(n == 0, nw, 0), MASK_VALUE, 0.0)

    heads = [slice(hd * h, hd * (h + 1)) for h in range(DIL_HEADS_PER_GROUP)]
    problems = [(ph, jb) for ph in range(phases) for jb in range(kb)]
    scores = [[lax.dot_general(q_ref[0, nw * jb:nw * (jb + 1), gw * ph + hd * h:gw * ph + hd * (h + 1)],
                               kcat[ph, nw * jb:nw * (jb + 2), hs], NT_DIMS,
                               preferred_element_type=F32) for h, hs in enumerate(heads)]
              for ph, jb in problems]
    for i, (ph, jb) in enumerate(problems):
        rows = slice(nw * jb, nw * (jb + 1))
        stat_tile = jnp.zeros((nw, LANES), F32)
        for h, hs in enumerate(heads):
            t = scores[i][h] + bias_scr[h]
            if jb == 0:
                t = t + no_history
            m = jnp.max(t, axis=1, keepdims=True)
            p = jnp.exp2(t - m)
            l = jnp.sum(p, axis=1, keepdims=True)
            o_ref[0, rows, gw * ph + hd * h:gw * ph + hd * (h + 1)] = _dot(
                p.astype(BF16), vcat[ph, nw * jb:nw * (jb + 2), hs]).astype(BF16)
            stat_tile = jnp.where(lane == h, m, stat_tile)
            stat_tile = jnp.where(lane == h + DIL_HEADS_PER_GROUP, l, stat_tile)
        st_ref[0, rows, LANES * ph:LANES * (ph + 1)] = stat_tile


def _alibi_neg_slopes(group):
    n = len(DIL_PATTERN) * DIL_HEADS_PER_GROUP
    k = np.arange(1, n + 1, dtype=np.float32)
    slopes = np.float32(2.0) ** (np.float32(-ALIBI_MAX_EXP) * k / np.float32(n))
    lo = group * DIL_HEADS_PER_GROUP
    return tuple(float(-v) for v in slopes[lo:lo + DIL_HEADS_PER_GROUP])


def _dil_attn(view, group, *, kb, phases):
    _, dilation = DIL_PATTERN[group]
    b, length, _ = view.shape
    nw = DIL_WINDOW_STEPS
    gw = DIL_HEADS_PER_GROUP * DIL_HEAD_DIM
    rows = nw * kb
    steps = dilation // phases

    def cur(part):
        return pl.BlockSpec((1, rows, phases * gw), lambda bi, r, n: (bi, n, part * steps + r))

    def prev(part):
        return pl.BlockSpec((1, nw, phases * gw),
                            lambda bi, r, n: (bi, jnp.maximum(n * kb - 1, 0), part * steps + r))

    o, st = pl.pallas_call(
        functools.partial(_dil_attn_kernel, dilation=dilation, kb=kb, phases=phases,
                          neg_slopes=_alibi_neg_slopes(group)),
        grid=(b, steps, length // rows),
        in_specs=[cur(0), cur(1), cur(2), prev(1), prev(2)],
        out_specs=[
            pl.BlockSpec((1, rows, phases * gw), lambda bi, r, n: (bi, n, r)),
            pl.BlockSpec((1, rows, phases * LANES), lambda bi, r, n: (bi, n, r)),
        ],
        out_shape=[
            jax.ShapeDtypeStruct((b, length, dilation * gw), BF16),
            jax.ShapeDtypeStruct((b, length, dilation * LANES), F32),
        ],
        scratch_shapes=[pltpu.VMEM((phases, nw * (kb + 1), gw), BF16),
                        pltpu.VMEM((phases, nw * (kb + 1), gw), BF16),
                        pltpu.VMEM((DIL_HEADS_PER_GROUP, nw, 2 * nw), F32)],
        compiler_params=_cparams(3),
        name=f"dil_attn_g{group}",
    )(view, view, view, view, view)
    return o, st


def _natural_rows(blk, perm_t, dilation, tm):
    cols = blk.shape[1] // dilation
    if dilation == 1:
        return blk
    n = PERM_ROWS // dilation
    chunks = []
    for c in range(tm // PERM_ROWS):
        phase_major = jnp.concatenate(
            [blk[n * c:n * (c + 1), cols * r:cols * (r + 1)] for r in range(dilation)], axis=0)
        chunks.append(_dot(perm_t, phase_major))
    return jnp.concatenate(chunks, axis=0)


def _dil_out_kernel(o0_ref, o1_ref, o2_ref, s0_ref, s1_ref, s2_ref, z_ref, h_ref, p_ref,
                    pt1_ref, pt2_ref, e_ref, wout_ref, wup_ref, wgate_ref, g_ref,
                    out_ref, *, dilations, tm):
    perm_ts = (None, pt1_ref[...], pt2_ref[...])
    n_heads = DIL_HEADS_PER_GROUP
    row_max, row_sum = [], []
    for st_ref, perm_t, dil in zip((s0_ref, s1_ref, s2_ref), perm_ts, dilations):
        if dil == 1:
            st = st_ref[0]
        else:
            hi, mid, lo = (_natural_rows(piece.astype(BF16), perm_t, dil, tm)
                           for piece in _split3(st_ref[0]))
            st = hi + mid + lo
        row_max.append(st)
        row_sum.append(pltpu.roll(st, LANES - n_heads, 1))
    m = jnp.maximum(jnp.maximum(row_max[0], row_max[1]), row_max[2])
    e = [jnp.exp2(v - m) for v in row_max]
    inv = 1.0 / (e[0] * row_sum[0] + e[1] * row_sum[1] + e[2] * row_sum[2])
    live = lax.broadcasted_iota(jnp.int32, (tm, LANES), 1) < n_heads
    o = None
    for ev, o_ref, perm_t, dil in zip(e, (o0_ref, o1_ref, o2_ref), perm_ts, dilations):
        w = jnp.where(live, ev * inv, 0.0)
        w_hi = w.astype(BF16)
        w_lo = (w - w_hi.astype(F32)).astype(BF16)
        w_full = _dot(jnp.concatenate([w_hi, w_lo], axis=1), e_ref[...])
        term = w_full * _natural_rows(o_ref[0], perm_t, dil, tm)
        o = term if o is None else o + term
    z = z_ref[0].astype(F32)
    g = (o * (z * jax.nn.sigmoid(z))).astype(BF16)
    h1 = h_ref[0] + _dot(g, wout_ref[...])
    out_ref[0] = _rms_norm(_ple(h1, p_ref[0], wup_ref, wgate_ref), g_ref[...])


def _dil_out(outs, stats, proj0, h, p, w_out, w_up, w_gate, final_g, *, tm):
    b, s, d = h.shape
    gw = DIL_HEADS_PER_GROUP * DIL_HEAD_DIM
    pd = p.shape[-1]
    dilations = tuple(dil for _, dil in DIL_PATTERN)
    z_block = proj0.shape[-1] // gw - 1
    expand = np.zeros((2 * LANES, gw), np.float32)
    for hh in range(DIL_HEADS_PER_GROUP):
        expand[hh, DIL_HEAD_DIM * hh:DIL_HEAD_DIM * (hh + 1)] = 1.0
        expand[LANES + hh, DIL_HEAD_DIM * hh:DIL_HEAD_DIM * (hh + 1)] = 1.0
    perm_ts = [jnp.asarray(_phase_major_matrix(dil).T, BF16) for dil in dilations[1:]]
    const = lambda shape: pl.BlockSpec(shape, lambda bi, i: (0,) * len(shape))
    row = lambda w: pl.BlockSpec((1, tm, w), lambda bi, i: (bi, i, 0))
    view = lambda w: [pl.BlockSpec((1, tm // dil, dil * w), lambda bi, i: (bi, i, 0))
                      for dil in dilations]
    return pl.pallas_call(
        functools.partial(_dil_out_kernel, dilations=dilations, tm=tm),
        grid=(b, s // tm),
        in_specs=view(gw) + view(LANES) + [
            pl.BlockSpec((1, tm, gw), lambda bi, i: (bi, i, z_block)),
            row(d), row(pd),
            const((PERM_ROWS, PERM_ROWS)), const((PERM_ROWS, PERM_ROWS)),
            const((2 * LANES, gw)), const((gw, d)), const((pd, d)), const((d, d)), const((1, d)),
        ],
        out_specs=row(d),
        out_shape=jax.ShapeDtypeStruct((b, s, d), F32),
        compiler_params=_cparams(2),
        name="dil_out",
    )(*outs, *stats, proj0, h, p, *perm_ts, jnp.asarray(expand, BF16),
      w_out.astype(BF16), w_up.astype(BF16), w_gate.astype(BF16), final_g.reshape(1, d))


def kernel(x, p, fox_norm, fox_w_in, fox_b_f, fox_w_out, dil_norm, dil_w_in, dil_w_out,
           ple_w_up, ple_w_gate, final_norm):
    s = x.shape[1]
    tm = min(512, s)
    tq = min(512, s)
    tk = min(512, s)
    assert tm == tq == tk, "the pruning table pairs fox_in tiles with query tiles and key blocks"
    kaug, qt, vt, zt, augq, kst, qst = _fox_in(x, fox_norm[0], fox_w_in[0], fox_b_f[0], tm=tm, ch=tk)
    pps = 2
    ot = _fox_attn(kaug, qt, vt, augq, _fox_prune_table(kst, qst, 2 * pps), tq=tq, tk=tk,
                   pairs_per_step=pps)
    h = _fox_out(ot, zt, x, p[0], fox_w_out[0], ple_w_up[0], ple_w_gate[0], tm=tm)

    gw = DIL_HEADS_PER_GROUP * DIL_HEAD_DIM
    n_groups = len(DIL_PATTERN)
    w1 = dil_w_in[0]
    k_scale = DIL_HEAD_DIM ** -0.5 * LOG2E
    qkv = lambda g: [w1[:, (n_groups * part + g) * gw:(n_groups * part + g + 1) * gw]
                     * (k_scale if part == 1 else 1.0) for part in range(3)]
    w_natural = jnp.concatenate(qkv(0) + [w1[:, 3 * n_groups * gw:]], axis=1)
    w_phased = jnp.concatenate(qkv(1) + qkv(2), axis=1)
    (proj0,) = _dil_in(h, dil_norm[0], w_natural, (1,), tm=tm, part_cols=gw)
    views = [proj0] + list(_dil_in(h, dil_norm[0], w_phased,
                                   (DIL_PATTERN[1][1], DIL_PATTERN[2][1]), tm=min(PERM_ROWS, s),
                                   part_cols=gw))
    outs, stats = [], []
    for group, (_, dilation) in enumerate(DIL_PATTERN):
        kb = min(DIL_SUB_BLOCKS, s // dilation // DIL_WINDOW_STEPS)
        phases = min(dilation, DIL_SUB_BLOCKS // kb)
        o, st = _dil_attn(views[group], group, kb=kb, phases=phases)
        outs.append(o)
        stats.append(st)
    return _dil_out(outs, stats, proj0, h, p[1], dil_w_out[0], ple_w_up[1], ple_w_gate[1],
                    final_norm, tm=tm)
```

```python
import functools

import numpy as np
import jax
import jax.numpy as jnp
from jax import lax
from jax.experimental import pallas as pl
from jax.experimental.pallas import tpu as pltpu

F32 = jnp.float32
BF16 = jnp.bfloat16

RMS_EPS = 1e-6
FOX_HEADS = 16
FOX_HEAD_DIM = 64
FOX_PAIRS = FOX_HEADS // 2
DIL_PATTERN = ((128, 1), (512, 4), (2048, 16))
DIL_HEADS_PER_GROUP = 8
DIL_HEAD_DIM = 128
DIL_WINDOW_STEPS = 128
ALIBI_MAX_EXP = 8.0
MASK_VALUE = -1e30
LOG2E = 1.4426950408889634

LANES = 128
AUG_SLOTS_PER_HEAD = 16
QUERY_LANES = 256
PRUNE_LOG2 = 50.0
NORM_SLACK = 1.02
VMEM_LIMIT_BYTES = 56 * 1024 * 1024

NT_DIMS = (((1,), (1,)), ((), ()))
TN_DIMS = (((0,), (0,)), ((), ()))


def _cparams(n_axes):
    return pltpu.CompilerParams(
        dimension_semantics=("arbitrary",) * n_axes,
        vmem_limit_bytes=VMEM_LIMIT_BYTES,
    )


def _rms_norm(x, g):
    ms = jnp.mean(x * x, axis=-1, keepdims=True)
    return x * lax.rsqrt(ms + RMS_EPS) * g


def _log_sigmoid(x):
    return jnp.minimum(x, 0.0) - jnp.log1p(jnp.exp(-jnp.abs(x)))


def _split3(x):
    hi = x.astype(BF16).astype(F32)
    r1 = x - hi
    mid = r1.astype(BF16).astype(F32)
    lo = (r1 - mid).astype(BF16).astype(F32)
    return hi, mid, lo


def _dot(a, b):
    return jnp.dot(a, b, preferred_element_type=F32)


def _fox_in_kernel(x_ref, g_ref, wk_ref, wt_ref, wft_ref, bt_ref,
                   triu_ref, pk_ref, pqt_ref, gk_ref, gq_ref,
                   kaug_ref, qt_ref, vt_ref, zt_ref, augq_ref, kst_ref, qst_ref,
                   carryt_ref, *, tm, ch):
    i = pl.program_id(1)

    @pl.when(i == 0)
    def _():
        carryt_ref[...] = jnp.zeros_like(carryt_ref)

    hn = _rms_norm(x_ref[0], g_ref[...]).astype(BF16)
    width = FOX_HEADS * FOX_HEAD_DIM

    ft = lax.dot_general(wft_ref[...], hn, NT_DIMS, preferred_element_type=F32)
    hi, mid, lo = _split3(_log_sigmoid(ft + bt_ref[...]))
    stack = jnp.concatenate([hi, mid, lo], axis=0).astype(BF16)
    cs = _dot(stack, triu_ref[...])
    ct = cs[0:16] + cs[16:32] + cs[32:48] + carryt_ref[:, 0:1]
    carryt_ref[...] = jnp.broadcast_to(ct[:, tm - 1:tm], carryt_ref.shape)
    c = jnp.concatenate([ct, jnp.zeros((LANES - FOX_HEADS, tm), F32)], axis=0).T

    k = _dot(hn, wk_ref[...])
    nhi, nmid, nlo = _split3(c * -LOG2E)
    lane = lax.broadcasted_iota(jnp.int32, (tm, LANES), 1)
    slots = jnp.where(lane < 16, nhi,
                      jnp.where(lane < 32, pltpu.roll(nmid, 16, 1),
                                jnp.where(lane < 48, pltpu.roll(nlo, 32, 1),
                                          jnp.where(lane == 48, 1.0, 0.0))))
    kaug = _dot(slots.astype(BF16), pk_ref[...])
    kb = k.astype(BF16)
    for p in range(FOX_PAIRS):
        kaug_ref[0, :, 2 * LANES * p:2 * LANES * p + LANES] = kb[:, LANES * p:LANES * (p + 1)]
        kaug_ref[0, :, 2 * LANES * p + LANES:2 * LANES * (p + 1)] = (
            kaug[:, LANES * p:LANES * (p + 1)].astype(BF16))
    kf = kb.astype(F32)
    kn2 = jnp.max(_dot((kf * kf).astype(BF16), gk_ref[...]), axis=0, keepdims=True)
    kst_ref[0, 0] = jnp.concatenate(
        [kn2, c[0:1, :], c[tm - 1:tm, :], jnp.zeros((5, LANES), F32)], axis=0)

    rows = 512
    for c0 in range(0, 3 * width, rows):
        res = lax.dot_general(wt_ref[c0:c0 + rows, :], hn, NT_DIMS,
                              preferred_element_type=F32).astype(BF16)
        which, off = divmod(c0, width)
        if which == 0:
            qt_ref[0, off:off + rows, :] = res
        elif which == 1:
            for j in range(tm // ch):
                vt_ref[0, j, off:off + rows, :] = res[:, j * ch:(j + 1) * ch]
        else:
            zt_ref[0, off:off + rows, :] = res

    hi, mid, lo = _split3(ct * LOG2E)
    ones = jnp.where(lax.broadcasted_iota(jnp.int32, (16, tm), 0) == 0, 1.0, 0.0)
    slots_t = jnp.concatenate([hi, mid, lo, ones], axis=0).astype(BF16)
    augq_ref[0] = _dot(pqt_ref[...], slots_t).astype(BF16)
    qf = qt_ref[0].astype(F32)
    qn2 = jnp.max(_dot(gq_ref[...], (qf * qf).astype(BF16)), axis=1, keepdims=True)
    qst_ref[0, 0] = jnp.broadcast_to(qn2, (FOX_HEADS, LANES))


def _placement_matrices():
    pk = np.zeros((LANES, FOX_PAIRS * LANES), np.float32)
    pqt = np.zeros((FOX_HEADS * AUG_SLOTS_PER_HEAD, 64), np.float32)
    for h in range(FOX_HEADS):
        p, odd = divmod(h, 2)
        base = 6 * odd
        for piece in range(3):
            pk[48, LANES * p + base + piece] = 1.0
            pk[16 * piece + h, LANES * p + base + 3 + piece] = 1.0
            pqt[AUG_SLOTS_PER_HEAD * h + base + piece, 16 * piece + h] = 1.0
            pqt[AUG_SLOTS_PER_HEAD * h + base + 3 + piece, 48] = 1.0
    return jnp.asarray(pk, BF16), jnp.asarray(pqt, BF16)


def _fox_in(x, norm_g, w_in, b_f, *, tm, ch):
    b, s, d = x.shape
    width = FOX_HEADS * FOX_HEAD_DIM
    wq = w_in[:, :width] * (FOX_HEAD_DIM ** -0.5 * LOG2E)
    wk = w_in[:, width:2 * width]
    wv = w_in[:, 2 * width:3 * width]
    wz = w_in[:, 3 * width:4 * width]
    wf = w_in[:, 4 * width:]
    wt = jnp.concatenate([wq, wv, wz], axis=1).T.astype(BF16)
    wft = wf.T.astype(BF16)
    bt = b_f.reshape(FOX_HEADS, 1)
    r = np.arange(tm)
    triu = jnp.asarray(r[:, None] <= r[None, :], BF16)
    pk, pqt = _placement_matrices()
    head_of = np.arange(width) // FOX_HEAD_DIM
    gk = jnp.asarray(head_of[:, None] == np.arange(LANES)[None, :], BF16)
    gq = jnp.asarray(np.arange(FOX_HEADS)[:, None] == head_of[None, :], BF16)

    const = lambda shape: pl.BlockSpec(shape, lambda bi, i: (0,) * len(shape))
    return pl.pallas_call(
        functools.partial(_fox_in_kernel, tm=tm, ch=ch),
        grid=(b, s // tm),
        in_specs=[
            pl.BlockSpec((1, tm, d), lambda bi, i: (bi, i, 0)),
            const((1, d)),
            const(wk.shape), const(wt.shape), const(wft.shape),
            const((FOX_HEADS, 1)),
            const((tm, tm)),
            const(pk.shape), const(pqt.shape), const(gk.shape), const(gq.shape),
        ],
        out_specs=[
            pl.BlockSpec((1, tm, 2 * width), lambda bi, i: (bi, i, 0)),
            pl.BlockSpec((1, width, tm), lambda bi, i: (bi, 0, i)),
            pl.BlockSpec((1, tm // ch, width, ch), lambda bi, i: (bi, i, 0, 0)),
            pl.BlockSpec((1, width, tm), lambda bi, i: (bi, 0, i)),
            pl.BlockSpec((1, FOX_HEADS * AUG_SLOTS_PER_HEAD, tm), lambda bi, i: (bi, 0, i)),
            pl.BlockSpec((1, 1, 8, LANES), lambda bi, i: (bi, i, 0, 0)),
            pl.BlockSpec((1, 1, FOX_HEADS, LANES), lambda bi, i: (bi, i, 0, 0)),
        ],
        out_shape=[
            jax.ShapeDtypeStruct((b, s, 2 * width), BF16),
            jax.ShapeDtypeStruct((b, width, s), BF16),
            jax.ShapeDtypeStruct((b, s // ch, width, ch), BF16),
            jax.ShapeDtypeStruct((b, width, s), BF16),
            jax.ShapeDtypeStruct((b, FOX_HEADS * AUG_SLOTS_PER_HEAD, s), BF16),
            jax.ShapeDtypeStruct((b, s // tm, 8, LANES), F32),
            jax.ShapeDtypeStruct((b, s // tm, FOX_HEADS, LANES), F32),
        ],
        scratch_shapes=[pltpu.VMEM((FOX_HEADS, LANES), F32)],
        compiler_params=_cparams(2),
        name="fox_in",
    )(x, norm_g.reshape(1, d), wk.astype(BF16), wt, wft, bt, triu, pk, pqt, gk, gq)


def _fox_attn_kernel(kstart_ref, kaug_ref, vt_ref, qt_ref, augq_ref, ot_ref,
                     s_scr, p_scr, *, tq, tk, pairs_per_step):
    qi = pl.program_id(2)
    hd = FOX_HEAD_DIM
    zeros_hd = jnp.zeros((hd, tq), BF16)
    zeros_tail = jnp.zeros((LANES - AUG_SLOTS_PER_HEAD, tq), BF16)
    ones_block = lambda n: jnp.where(lax.broadcasted_iota(jnp.int32, (16, n), 0) == 0, 1.0, 0.0).astype(BF16)
    ones_rows = ones_block(tk)
    n_blocks = (qi * tq) // tk + 1
    k_first = kstart_ref[pl.program_id(0), pl.program_id(1), qi]

    heads = [(u, h) for u in range(pairs_per_step) for h in range(2)]
    w_head = {}
    for u, h in heads:
        q_rows = qt_ref[0, LANES * u + hd * h:LANES * u + hd * (h + 1), :]
        aug_rows = augq_ref[0, 2 * AUG_SLOTS_PER_HEAD * u + AUG_SLOTS_PER_HEAD * h:
                            2 * AUG_SLOTS_PER_HEAD * u + AUG_SLOTS_PER_HEAD * (h + 1), :]
        q_part = [q_rows, zeros_hd] if h == 0 else [zeros_hd, q_rows]
        w_head[u, h] = jnp.concatenate(q_part + [aug_rows, zeros_tail], axis=0)
    chains = [(u, h, sub) for u, h in heads for sub in range(tq // QUERY_LANES)]

    def issue_scores(kj, slot):
        ks = pl.multiple_of(kj * tk, tk)
        block_max = []
        for c, (u, h, sub) in enumerate(chains):
            kb = kaug_ref[0, pl.ds(ks, tk), 2 * LANES * u:2 * LANES * (u + 1)]
            s = _dot(kb, w_head[u, h][:, QUERY_LANES * sub:QUERY_LANES * (sub + 1)])
            s_scr[slot, c] = s
            block_max.append(jnp.max(s, axis=0, keepdims=True))
        return tuple(block_max)

    tri_key = lax.broadcasted_iota(jnp.int32, (QUERY_LANES, QUERY_LANES), 0)
    tri_qry = lax.broadcasted_iota(jnp.int32, (QUERY_LANES, QUERY_LANES), 1)
    causal = jnp.where(tri_key <= tri_qry, 0.0, MASK_VALUE)

    def pv_block(kj, c):
        u, h, _ = chains[c]
        v_rows = vt_ref[0, kj, LANES * u + hd * h:LANES * u + hd * (h + 1), :]
        return _dot(jnp.concatenate([v_rows, ones_rows], axis=0), p_scr[c])

    def step(kj, slot, state, has_prev=True):
        carry, block_max = state
        next_max = issue_scores(kj + 1, 1 - slot)
        out = []
        for c in range(len(chains)):
            m_old, alpha_prev, acc = carry[c]
            if has_prev:
                acc = acc * alpha_prev + pv_block(kj - 1, c)
            m_new = jnp.maximum(m_old, block_max[c])
            p_scr[c] = jnp.exp2(s_scr[slot, c] - m_new).astype(BF16)
            out.append((m_new, jnp.exp2(m_old - m_new), acc))
        return tuple(out), next_max

    def diagonal(kj, slot, state):
        carry, _ = state
        for c, (u, h, sub) in enumerate(chains):
            m_old, alpha_prev, acc = carry[c]
            acc = acc * alpha_prev + pv_block(jnp.maximum(kj - 1, 0), c)
            lo = QUERY_LANES * sub
            vis = lo + QUERY_LANES
            mid = s_scr[slot, c, lo:vis, :] + causal
            m_new = jnp.maximum(m_old, jnp.max(mid, axis=0, keepdims=True))
            if lo:
                top = s_scr[slot, c, 0:lo, :]
                m_new = jnp.maximum(m_new, jnp.max(top, axis=0, keepdims=True))
                p = jnp.concatenate([jnp.exp2(top - m_new).astype(BF16),
                                     jnp.exp2(mid - m_new).astype(BF16)], axis=0)
            else:
                p = jnp.exp2(mid - m_new).astype(BF16)
            v_rows = vt_ref[0, kj, LANES * u + hd * h:LANES * u + hd * (h + 1), 0:vis]
            pv = _dot(jnp.concatenate([v_rows, ones_block(vis)], axis=0), p)
            acc = acc * jnp.exp2(m_old - m_new) + pv
            ot_ref[0, LANES * u + hd * h:LANES * u + hd * (h + 1),
                   QUERY_LANES * sub:QUERY_LANES * (sub + 1)] = acc[0:hd] / acc[hd:hd + 1]

    row = lambda value: jnp.full((1, QUERY_LANES), value, F32)
    carry = tuple((row(MASK_VALUE), row(1.0), jnp.zeros((hd + 16, QUERY_LANES), F32)) for _ in chains)
    state = (carry, tuple(row(0.0) for _ in chains))
    n_unmasked = n_blocks - 1 - k_first
    has_unmasked = jnp.minimum(n_unmasked, 1)

    def first(_, st):
        return step(k_first, 0, (st[0], issue_scores(k_first, 0)), has_prev=False)

    def only_diagonal(_, st):
        p_scr[...] = jnp.zeros_like(p_scr)
        return st[0], issue_scores(k_first, 0)

    def pair(i, st):
        kj = k_first + 1 + 2 * i
        return step(kj + 1, 0, step(kj, 1, st))

    state = lax.fori_loop(0, has_unmasked, first, state)
    state = lax.fori_loop(0, 1 - has_unmasked, only_diagonal, state)
    n_rest = jnp.maximum(n_unmasked - 1, 0)
    state = lax.fori_loop(0, n_rest // 2, pair, state)
    state = lax.fori_loop(0, lax.rem(n_rest, 2), lambda _, st: step(n_blocks - 2, 1, st), state)
    diagonal(n_blocks - 1, lax.rem(n_unmasked, 2), state)


def _fox_prune_table(kst, qst, heads_per_step):
    kn = jnp.sqrt(kst[:, :, 0, :FOX_HEADS]) * NORM_SLACK
    qn = jnp.sqrt(qst[:, :, :, 0]) * NORM_SLACK
    c_first = kst[:, :, 1, :FOX_HEADS] * LOG2E
    c_last = kst[:, :, 2, :FOX_HEADS] * LOG2E
    bound = (qn[:, :, None, :] * kn[:, None, :, :] + c_first[:, :, None, :] - c_last[:, None, :, :]
             + (qn * kn)[:, :, None, :])
    n = kst.shape[1]
    earlier = jnp.arange(n)[None, :, None] < jnp.arange(n)[:, None, None]
    skip = (bound < -PRUNE_LOG2) & earlier[None]
    block = jnp.arange(n, dtype=jnp.int32)[None, None, :, None]
    lead = jnp.min(jnp.where(skip, n, block), axis=2)
    lead = jnp.min(lead.reshape(lead.shape[0], n, FOX_HEADS // heads_per_step, heads_per_step), axis=3)
    return jnp.transpose(lead, (0, 2, 1)).astype(jnp.int32)


def _fox_attn(kaug, qt, vt, augq, kstart, *, tq, tk, pairs_per_step):
    b, s, _ = kaug.shape
    width = FOX_HEADS * FOX_HEAD_DIM
    nk = s // tk
    pps = pairs_per_step
    n_chains = 2 * pps * (tq // QUERY_LANES)
    grid_spec = pltpu.PrefetchScalarGridSpec(
        num_scalar_prefetch=1,
        grid=(b, FOX_PAIRS // pps, s // tq),
        in_specs=[
            pl.BlockSpec((1, s, 2 * LANES * pps), lambda bi, p, qi, ks: (bi, 0, p)),
            pl.BlockSpec((1, nk, LANES * pps, tk), lambda bi, p, qi, ks: (bi, 0, p, 0)),
            pl.BlockSpec((1, LANES * pps, tq), lambda bi, p, qi, ks: (bi, p, qi)),
            pl.BlockSpec((1, 2 * AUG_SLOTS_PER_HEAD * pps, tq), lambda bi, p, qi, ks: (bi, p, qi)),
        ],
        out_specs=pl.BlockSpec((1, LANES * pps, tq), lambda bi, p, qi, ks: (bi, p, qi)),
        scratch_shapes=[pltpu.VMEM((2, n_chains, tk, QUERY_LANES), F32),
                        pltpu.VMEM((n_chains, tk, QUERY_LANES), BF16)],
    )
    return pl.pallas_call(
        functools.partial(_fox_attn_kernel, tq=tq, tk=tk, pairs_per_step=pps),
        grid_spec=grid_spec,
        out_shape=jax.ShapeDtypeStruct((b, width, s), F32),
        compiler_params=_cparams(3),
        name="fox_attn",
    )(kstart, kaug, vt, qt, augq)


def _ple(h1, p, wup_ref, wgate_ref):
    gate = jax.nn.sigmoid(_dot(h1.astype(BF16), wgate_ref[...]))
    up = _dot(p.astype(BF16), wup_ref[...])
    return h1 + up * gate


def _fox_out_kernel(ot_ref, zt_ref, x_ref, p_ref, wout_ref, wup_ref, wgate_ref, h_ref):
    z = zt_ref[0].astype(F32)
    gt = (ot_ref[0] * (z * jax.nn.sigmoid(z))).astype(BF16)
    y = lax.dot_general(gt, wout_ref[...], TN_DIMS, preferred_element_type=F32)
    h_ref[0] = _ple(x_ref[0] + y, p_ref[0], wup_ref, wgate_ref)


def _fox_out(ot, zt, x, p, w_out, w_up, w_gate, *, tm):
    b, s, d = x.shape
    width = ot.shape[1]
    pd = p.shape[-1]
    const = lambda shape: pl.BlockSpec(shape, lambda bi, i: (0,) * len(shape))
    return pl.pallas_call(
        _fox_out_kernel,
        grid=(b, s // tm),
        in_specs=[
            pl.BlockSpec((1, width, tm), lambda bi, i: (bi, 0, i)),
            pl.BlockSpec((1, width, tm), lambda bi, i: (bi, 0, i)),
            pl.BlockSpec((1, tm, d), lambda bi, i: (bi, i, 0)),
            pl.BlockSpec((1, tm, pd), lambda bi, i: (bi, i, 0)),
            const((width, d)), const((pd, d)), const((d, d)),
        ],
        out_specs=pl.BlockSpec((1, tm, d), lambda bi, i: (bi, i, 0)),
        out_shape=jax.ShapeDtypeStruct((b, s, d), F32),
        compiler_params=_cparams(2),
        name="fox_out",
    )(ot, zt, x, p, w_out.astype(BF16), w_up.astype(BF16), w_gate.astype(BF16))


PERM_ROWS = 256
DIL_SUB_BLOCKS = 8


def _phase_major_matrix(dilation):
    n = PERM_ROWS // dilation
    p = np.zeros((PERM_ROWS, PERM_ROWS), np.float32)
    for r in range(dilation):
        for i in range(n):
            p[r * n + i, dilation * i + r] = 1.0
    return p


def _dil_in_kernel(h_ref, g_ref, *rest, dilations, parts, tm):
    n_w = sum(parts)
    n_perm = sum(d > 1 for d in dilations)
    w_refs, perm_refs, out_refs = rest[:n_w], list(rest[n_w:n_w + n_perm]), rest[n_w + n_perm:]
    hn = _rms_norm(h_ref[0], g_ref[...]).astype(BF16)
    first = 0
    for dil, n_parts, o_ref in zip(dilations, parts, out_refs):
        group_w = w_refs[first:first + n_parts]
        first += n_parts
        pc = group_w[0].shape[1]
        if dil == 1:
            for part, w_ref in enumerate(group_w):
                o_ref[0, :, pc * part:pc * (part + 1)] = _dot(hn, w_ref[...]).astype(BF16)
            continue
        perm = perm_refs.pop(0)[...]
        n = PERM_ROWS // dil
        for c in range(tm // PERM_ROWS):
            rows = slice(PERM_ROWS * c, PERM_ROWS * (c + 1))
            hp = _dot(perm, hn[rows]).astype(BF16)
            for part, w_ref in enumerate(group_w):
                res = _dot(hp, w_ref[...]).astype(BF16)
                for r in range(dil):
                    dst = pc * (part * dil + r)
                    o_ref[0, n * c:n * (c + 1), dst:dst + pc] = res[n * r:n * (r + 1)]


def _dil_in(h, norm_g, w, groups, dilations, *, tm, part_cols):
    b, s, d = h.shape
    parts = tuple(len(blocks) for blocks in groups)
    perms = [jnp.asarray(_phase_major_matrix(dil), BF16) for dil in dilations if dil > 1]
    const = lambda shape: pl.BlockSpec(shape, lambda bi, i: (0,) * len(shape))
    w_specs = [pl.BlockSpec((d, part_cols), lambda bi, i, cb=cb: (0, cb))
               for blocks in groups for cb in blocks]
    widths = [dil * n_parts * part_cols for dil, n_parts in zip(dilations, parts)]
    return pl.pallas_call(
        functools.partial(_dil_in_kernel, dilations=dilations, parts=parts, tm=tm),
        grid=(b, s // tm),
        in_specs=[pl.BlockSpec((1, tm, d), lambda bi, i: (bi, i, 0)), const((1, d))]
                 + w_specs + [const(p.shape) for p in perms],
        out_specs=[pl.BlockSpec((1, tm // dil, width), lambda bi, i: (bi, i, 0))
                   for dil, width in zip(dilations, widths)],
        out_shape=[jax.ShapeDtypeStruct((b, s // dil, width), BF16)
                   for dil, width in zip(dilations, widths)],
        compiler_params=_cparams(2),
        name="dil_in_" + "_".join(str(dil) for dil in dilations),
    )(h, norm_g.reshape(1, d), *([w] * sum(parts)), *perms)


def _dil_attn_kernel(q_ref, kc_ref, vc_ref, kp_ref, vp_ref, o_ref, st_ref,
                     kcat, vcat, bias_scr, *, dilation, kb, phases, neg_slopes):
    n = pl.program_id(2)
    nw = DIL_WINDOW_STEPS
    hd = DIL_HEAD_DIM
    gw = DIL_HEADS_PER_GROUP * hd

    @pl.when((pl.program_id(0) == 0) & (pl.program_id(1) == 0) & (n == 0))
    def _():
        row = lax.broadcasted_iota(jnp.int32, (nw, 2 * nw), 0)
        col = lax.broadcasted_iota(jnp.int32, (nw, 2 * nw), 1)
        dist = nw + row - col
        band = (dist >= 0) & (dist <= nw)
        dist_f = (dist * dilation).astype(F32)
        for h in range(DIL_HEADS_PER_GROUP):
            bias_scr[h] = jnp.where(band, neg_slopes[h] * dist_f * LOG2E, MASK_VALUE)

    for ph in range(phases):
        cols = slice(gw * ph, gw * (ph + 1))
        kcat[ph, 0:nw] = kp_ref[0, :, cols]
        kcat[ph, nw:nw * (kb + 1)] = kc_ref[0, :, cols]
        vcat[ph, 0:nw] = vp_ref[0, :, cols]
        vcat[ph, nw:nw * (kb + 1)] = vc_ref[0, :, cols]
    lane = lax.broadcasted_iota(jnp.int32, (nw, LANES), 1)
    col = lax.broadcasted_iota(jnp.int32, (nw, 2 * nw), 1)
    no_history = jnp.where(col < jnp.where(n == 0, nw, 0), MASK_VALUE, 0.0)

    heads = [slice(hd * h, hd * (h + 1)) for h in range(DIL_HEADS_PER_GROUP)]
    problems = [(ph, jb) for ph in range(phases) for jb in range(kb)]
    scores = [[lax.dot_general(q_ref[0, nw * jb:nw * (jb + 1), gw * ph + hd * h:gw * ph + hd * (h + 1)],
                               kcat[ph, nw * jb:nw * (jb + 2), hs], NT_DIMS,
                               preferred_element_type=F32) for h, hs in enumerate(heads)]
              for ph, jb in problems]
    for i, (ph, jb) in enumerate(problems):
        rows = slice(nw * jb, nw * (jb + 1))
        stat_tile = jnp.zeros((nw, LANES), F32)
        for h, hs in enumerate(heads):
            t = scores[i][h] + bias_scr[h]
            if jb == 0:
                t = t + no_history
            m = jnp.max(t, axis=1, keepdims=True)
            p = jnp.exp2(t - m)
            l = jnp.sum(p, axis=1, keepdims=True)
            o_ref[0, rows, gw * ph + hd * h:gw * ph + hd * (h + 1)] = _dot(
                p.astype(BF16), vcat[ph, nw * jb:nw * (jb + 2), hs]).astype(BF16)
            stat_tile = jnp.where(lane == h, m, stat_tile)
            stat_tile = jnp.where(lane == h + DIL_HEADS_PER_GROUP, l, stat_tile)
        st_ref[0, rows, LANES * ph:LANES * (ph + 1)] = stat_tile


def _alibi_neg_slopes(group):
    n = len(DIL_PATTERN) * DIL_HEADS_PER_GROUP
    k = np.arange(1, n + 1, dtype=np.float32)
    slopes = np.float32(2.0) ** (np.float32(-ALIBI_MAX_EXP) * k / np.float32(n))
    lo = group * DIL_HEADS_PER_GROUP
    return tuple(float(-v) for v in slopes[lo:lo + DIL_HEADS_PER_GROUP])


def _dil_attn(view, group, *, kb, phases):
    _, dilation = DIL_PATTERN[group]
    b, length, _ = view.shape
    nw = DIL_WINDOW_STEPS
    gw = DIL_HEADS_PER_GROUP * DIL_HEAD_DIM
    rows = nw * kb
    steps = dilation // phases

    def cur(part):
        return pl.BlockSpec((1, rows, phases * gw), lambda bi, r, n: (bi, n, part * steps + r))

    def prev(part):
        return pl.BlockSpec((1, nw, phases * gw),
                            lambda bi, r, n: (bi, jnp.maximum(n * kb - 1, 0), part * steps + r))

    o, st = pl.pallas_call(
        functools.partial(_dil_attn_kernel, dilation=dilation, kb=kb, phases=phases,
                          neg_slopes=_alibi_neg_slopes(group)),
        grid=(b, steps, length // rows),
        in_specs=[cur(0), cur(1), cur(2), prev(1), prev(2)],
        out_specs=[
            pl.BlockSpec((1, rows, phases * gw), lambda bi, r, n: (bi, n, r)),
            pl.BlockSpec((1, rows, phases * LANES), lambda bi, r, n: (bi, n, r)),
        ],
        out_shape=[
            jax.ShapeDtypeStruct((b, length, dilation * gw), BF16),
            jax.ShapeDtypeStruct((b, length, dilation * LANES), F32),
        ],
        scratch_shapes=[pltpu.VMEM((phases, nw * (kb + 1), gw), BF16),
                        pltpu.VMEM((phases, nw * (kb + 1), gw), BF16),
                        pltpu.VMEM((DIL_HEADS_PER_GROUP, nw, 2 * nw), F32)],
        compiler_params=_cparams(3),
        name=f"dil_attn_g{group}",
    )(view, view, view, view, view)
    return o, st


def _natural_rows(blk, perm_t, dilation, tm):
    cols = blk.shape[1] // dilation
    if dilation == 1:
        return blk
    n = PERM_ROWS // dilation
    chunks = []
    for c in range(tm // PERM_ROWS):
        phase_major = jnp.concatenate(
            [blk[n * c:n * (c + 1), cols * r:cols * (r + 1)] for r in range(dilation)], axis=0)
        chunks.append(_dot(perm_t, phase_major))
    return jnp.concatenate(chunks, axis=0)


def _dil_out_kernel(o0_ref, o1_ref, o2_ref, s0_ref, s1_ref, s2_ref, z_ref, h_ref, p_ref,
                    pt1_ref, pt2_ref, e_ref, wout_ref, wup_ref, wgate_ref, g_ref,
                    out_ref, *, dilations, tm):
    perm_ts = (None, pt1_ref[...], pt2_ref[...])
    n_heads = DIL_HEADS_PER_GROUP
    row_max, row_sum = [], []
    for st_ref, perm_t, dil in zip((s0_ref, s1_ref, s2_ref), perm_ts, dilations):
        if dil == 1:
            st = st_ref[0]
        else:
            hi, mid, lo = (_natural_rows(piece.astype(BF16), perm_t, dil, tm)
                           for piece in _split3(st_ref[0]))
            st = hi + mid + lo
        row_max.append(st)
        row_sum.append(pltpu.roll(st, LANES - n_heads, 1))
    m = jnp.maximum(jnp.maximum(row_max[0], row_max[1]), row_max[2])
    e = [jnp.exp2(v - m) for v in row_max]
    inv = 1.0 / (e[0] * row_sum[0] + e[1] * row_sum[1] + e[2] * row_sum[2])
    live = lax.broadcasted_iota(jnp.int32, (tm, LANES), 1) < n_heads
    o = None
    for ev, o_ref, perm_t, dil in zip(e, (o0_ref, o1_ref, o2_ref), perm_ts, dilations):
        w = jnp.where(live, ev * inv, 0.0)
        w_hi = w.astype(BF16)
        w_lo = (w - w_hi.astype(F32)).astype(BF16)
        w_full = _dot(jnp.concatenate([w_hi, w_lo], axis=1), e_ref[...])
        term = w_full * _natural_rows(o_ref[0], perm_t, dil, tm)
        o = term if o is None else o + term
    z = z_ref[0].astype(F32)
    g = (o * (z * jax.nn.sigmoid(z))).astype(BF16)
    h1 = h_ref[0] + _dot(g, wout_ref[...])
    out_ref[0] = _rms_norm(_ple(h1, p_ref[0], wup_ref, wgate_ref), g_ref[...])


def _dil_out(outs, stats, proj0, h, p, w_out, w_up, w_gate, final_g, *, tm):
    b, s, d = h.shape
    gw = DIL_HEADS_PER_GROUP * DIL_HEAD_DIM
    pd = p.shape[-1]
    dilations = tuple(dil for _, dil in DIL_PATTERN)
    z_block = proj0.shape[-1] // gw - 1
    expand = np.zeros((2 * LANES, gw), np.float32)
    for hh in range(DIL_HEADS_PER_GROUP):
        expand[hh, DIL_HEAD_DIM * hh:DIL_HEAD_DIM * (hh + 1)] = 1.0
        expand[LANES + hh, DIL_HEAD_DIM * hh:DIL_HEAD_DIM * (hh + 1)] = 1.0
    perm_ts = [jnp.asarray(_phase_major_matrix(dil).T, BF16) for dil in dilations[1:]]
    const = lambda shape: pl.BlockSpec(shape, lambda bi, i: (0,) * len(shape))
    row = lambda w: pl.BlockSpec((1, tm, w), lambda bi, i: (bi, i, 0))
    view = lambda w: [pl.BlockSpec((1, tm // dil, dil * w), lambda bi, i: (bi, i, 0))
                      for dil in dilations]
    return pl.pallas_call(
        functools.partial(_dil_out_kernel, dilations=dilations, tm=tm),
        grid=(b, s // tm),
        in_specs=view(gw) + view(LANES) + [
            pl.BlockSpec((1, tm, gw), lambda bi, i: (bi, i, z_block)),
            row(d), row(pd),
            const((PERM_ROWS, PERM_ROWS)), const((PERM_ROWS, PERM_ROWS)),
            const((2 * LANES, gw)), const((gw, d)), const((pd, d)), const((d, d)), const((1, d)),
        ],
        out_specs=row(d),
        out_shape=jax.ShapeDtypeStruct((b, s, d), F32),
        compiler_params=_cparams(2),
        name="dil_out",
    )(*outs, *stats, proj0, h, p, *perm_ts, jnp.asarray(expand, BF16),
      w_out.astype(BF16), w_up.astype(BF16), w_gate.astype(BF16), final_g.reshape(1, d))


def kernel(x, p, fox_norm, fox_w_in, fox_b_f, fox_w_out, dil_norm, dil_w_in, dil_w_out,
           ple_w_up, ple_w_gate, final_norm):
    s = x.shape[1]
    tm = min(512, s)
    tq = min(512, s)
    tk = min(512, s)
    assert tm == tq == tk, "the pruning table pairs fox_in tiles with query tiles and key blocks"
    kaug, qt, vt, zt, augq, kst, qst = _fox_in(x, fox_norm[0], fox_w_in[0], fox_b_f[0], tm=tm, ch=tk)
    pps = 2
    ot = _fox_attn(kaug, qt, vt, augq, _fox_prune_table(kst, qst, 2 * pps), tq=tq, tk=tk,
                   pairs_per_step=pps)
    h = _fox_out(ot, zt, x, p[0], fox_w_out[0], ple_w_up[0], ple_w_gate[0], tm=tm)

    gw = DIL_HEADS_PER_GROUP * DIL_HEAD_DIM
    n_groups = len(DIL_PATTERN)
    k_scale = DIL_HEAD_DIM ** -0.5 * LOG2E
    col = jnp.arange(dil_w_in.shape[-1]) // (n_groups * gw)
    w1 = (dil_w_in[0] * jnp.where(col == 1, k_scale, 1.0)[None, :]).astype(BF16)
    qkv = lambda g: [n_groups * part + g for part in range(3)]
    (proj0,) = _dil_in(h, dil_norm[0], w1, [qkv(0) + [3 * n_groups]], (1,), tm=tm, part_cols=gw)
    views = [proj0] + list(_dil_in(h, dil_norm[0], w1, [qkv(1), qkv(2)],
                                   (DIL_PATTERN[1][1], DIL_PATTERN[2][1]), tm=min(PERM_ROWS, s),
                                   part_cols=gw))
    outs, stats = [], []
    for group, (_, dilation) in enumerate(DIL_PATTERN):
        kb = min(DIL_SUB_BLOCKS, s // dilation // DIL_WINDOW_STEPS)
        phases = min(dilation, DIL_SUB_BLOCKS // kb)
        o, st = _dil_attn(views[group], group, kb=kb, phases=phases)
        outs.append(o)
        stats.append(st)
    return _dil_out(outs, stats, proj0, h, p[1], dil_w_out[0], ple_w_up[1], ple_w_gate[1],
                    final_norm, tm=tm)
```

```python
import functools

import numpy as np
import jax
import jax.numpy as jnp
from jax import lax
from jax.experimental import pallas as pl
from jax.experimental.pallas import tpu as pltpu

F32 = jnp.float32
BF16 = jnp.bfloat16

RMS_EPS = 1e-6
FOX_HEADS = 16
FOX_HEAD_DIM = 64
FOX_PAIRS = FOX_HEADS // 2
DIL_PATTERN = ((128, 1), (512, 4), (2048, 16))
DIL_HEADS_PER_GROUP = 8
DIL_HEAD_DIM = 128
DIL_WINDOW_STEPS = 128
ALIBI_MAX_EXP = 8.0
MASK_VALUE = -1e30
LOG2E = 1.4426950408889634

LANES = 128
AUG_SLOTS_PER_HEAD = 16
QUERY_LANES = 256
PRUNE_LOG2 = 50.0
NORM_SLACK = 1.02
VMEM_LIMIT_BYTES = 56 * 1024 * 1024

NT_DIMS = (((1,), (1,)), ((), ()))
TN_DIMS = (((0,), (0,)), ((), ()))


def _cparams(n_axes):
    return pltpu.CompilerParams(
        dimension_semantics=("arbitrary",) * n_axes,
        vmem_limit_bytes=VMEM_LIMIT_BYTES,
    )


def _rms_norm(x, g):
    ms = jnp.mean(x * x, axis=-1, keepdims=True)
    return x * lax.rsqrt(ms + RMS_EPS) * g


def _log_sigmoid(x):
    return jnp.minimum(x, 0.0) - jnp.log1p(jnp.exp(-jnp.abs(x)))


def _split3(x):
    hi = x.astype(BF16).astype(F32)
    r1 = x - hi
    mid = r1.astype(BF16).astype(F32)
    lo = (r1 - mid).astype(BF16).astype(F32)
    return hi, mid, lo


def _dot(a, b):
    return jnp.dot(a, b, preferred_element_type=F32)


def _fox_in_kernel(x_ref, g_ref, wk_ref, wt_ref, wft_ref, bt_ref,
                   triu_ref, pk_ref, pqt_ref, gk_ref, gq_ref,
                   kaug_ref, qt_ref, vt_ref, zt_ref, augq_ref, kst_ref, qst_ref,
                   carryt_ref, *, tm, ch):
    i = pl.program_id(1)

    @pl.when(i == 0)
    def _():
        carryt_ref[...] = jnp.zeros_like(carryt_ref)

    hn = _rms_norm(x_ref[0], g_ref[...]).astype(BF16)
    width = FOX_HEADS * FOX_HEAD_DIM

    ft = lax.dot_general(wft_ref[...], hn, NT_DIMS, preferred_element_type=F32)
    hi, mid, lo = _split3(_log_sigmoid(ft + bt_ref[...]))
    stack = jnp.concatenate([hi, mid, lo], axis=0).astype(BF16)
    cs = _dot(stack, triu_ref[...])
    ct = cs[0:16] + cs[16:32] + cs[32:48] + carryt_ref[:, 0:1]
    carryt_ref[...] = jnp.broadcast_to(ct[:, tm - 1:tm], carryt_ref.shape)
    c = jnp.concatenate([ct, jnp.zeros((LANES - FOX_HEADS, tm), F32)], axis=0).T

    k = _dot(hn, wk_ref[...])
    nhi, nmid, nlo = _split3(c * -LOG2E)
    lane = lax.broadcasted_iota(jnp.int32, (tm, LANES), 1)
    slots = jnp.where(lane < 16, nhi,
                      jnp.where(lane < 32, pltpu.roll(nmid, 16, 1),
                                jnp.where(lane < 48, pltpu.roll(nlo, 32, 1),
                                          jnp.where(lane == 48, 1.0, 0.0))))
    kaug = _dot(slots.astype(BF16), pk_ref[...])
    kb = k.astype(BF16)
    for p in range(FOX_PAIRS):
        kaug_ref[0, :, 2 * LANES * p:2 * LANES * p + LANES] = kb[:, LANES * p:LANES * (p + 1)]
        kaug_ref[0, :, 2 * LANES * p + LANES:2 * LANES * (p + 1)] = (
            kaug[:, LANES * p:LANES * (p + 1)].astype(BF16))
    kf = kb.astype(F32)
    kn2 = jnp.max(_dot((kf * kf).astype(BF16), gk_ref[...]), axis=0, keepdims=True)
    kst_ref[0, 0] = jnp.concatenate(
        [kn2, c[0:1, :], c[tm - 1:tm, :], jnp.zeros((5, LANES), F32)], axis=0)

    rows = 512
    for c0 in range(0, 3 * width, rows):
        res = lax.dot_general(wt_ref[c0:c0 + rows, :], hn, NT_DIMS,
                              preferred_element_type=F32).astype(BF16)
        which, off = divmod(c0, width)
        if which == 0:
            qt_ref[0, off:off + rows, :] = res
        elif which == 1:
            for j in range(tm // ch):
                vt_ref[0, j, off:off + rows, :] = res[:, j * ch:(j + 1) * ch]
        else:
            zt_ref[0, off:off + rows, :] = res

    hi, mid, lo = _split3(ct * LOG2E)
    ones = jnp.where(lax.broadcasted_iota(jnp.int32, (16, tm), 0) == 0, 1.0, 0.0)
    slots_t = jnp.concatenate([hi, mid, lo, ones], axis=0).astype(BF16)
    augq_ref[0] = _dot(pqt_ref[...], slots_t).astype(BF16)
    qf = qt_ref[0].astype(F32)
    qn2 = jnp.max(_dot(gq_ref[...], (qf * qf).astype(BF16)), axis=1, keepdims=True)
    qst_ref[0, 0] = jnp.broadcast_to(qn2, (FOX_HEADS, LANES))


def _placement_matrices():
    pk = np.zeros((LANES, FOX_PAIRS * LANES), np.float32)
    pqt = np.zeros((FOX_HEADS * AUG_SLOTS_PER_HEAD, 64), np.float32)
    for h in range(FOX_HEADS):
        p, odd = divmod(h, 2)
        base = 6 * odd
        for piece in range(3):
            pk[48, LANES * p + base + piece] = 1.0
            pk[16 * piece + h, LANES * p + base + 3 + piece] = 1.0
            pqt[AUG_SLOTS_PER_HEAD * h + base + piece, 16 * piece + h] = 1.0
            pqt[AUG_SLOTS_PER_HEAD * h + base + 3 + piece, 48] = 1.0
    return jnp.asarray(pk, BF16), jnp.asarray(pqt, BF16)


def _fox_in(x, norm_g, w_in, b_f, *, tm, ch):
    b, s, d = x.shape
    width = FOX_HEADS * FOX_HEAD_DIM
    wq = w_in[:, :width] * (FOX_HEAD_DIM ** -0.5 * LOG2E)
    wk = w_in[:, width:2 * width]
    wv = w_in[:, 2 * width:3 * width]
    wz = w_in[:, 3 * width:4 * width]
    wf = w_in[:, 4 * width:]
    wt = jnp.concatenate([wq, wv, wz], axis=1).astype(BF16).T
    wft = wf.T.astype(BF16)
    bt = b_f.reshape(FOX_HEADS, 1)
    r = np.arange(tm)
    triu = jnp.asarray(r[:, None] <= r[None, :], BF16)
    pk, pqt = _placement_matrices()
    head_of = np.arange(width) // FOX_HEAD_DIM
    gk = jnp.asarray(head_of[:, None] == np.arange(LANES)[None, :], BF16)
    gq = jnp.asarray(np.arange(FOX_HEADS)[:, None] == head_of[None, :], BF16)

    const = lambda shape: pl.BlockSpec(shape, lambda bi, i: (0,) * len(shape))
    return pl.pallas_call(
        functools.partial(_fox_in_kernel, tm=tm, ch=ch),
        grid=(b, s // tm),
        in_specs=[
            pl.BlockSpec((1, tm, d), lambda bi, i: (bi, i, 0)),
            const((1, d)),
            const(wk.shape), const(wt.shape), const(wft.shape),
            const((FOX_HEADS, 1)),
            const((tm, tm)),
            const(pk.shape), const(pqt.shape), const(gk.shape), const(gq.shape),
        ],
        out_specs=[
            pl.BlockSpec((1, tm, 2 * width), lambda bi, i: (bi, i, 0)),
            pl.BlockSpec((1, width, tm), lambda bi, i: (bi, 0, i)),
            pl.BlockSpec((1, tm // ch, width, ch), lambda bi, i: (bi, i, 0, 0)),
            pl.BlockSpec((1, width, tm), lambda bi, i: (bi, 0, i)),
            pl.BlockSpec((1, FOX_HEADS * AUG_SLOTS_PER_HEAD, tm), lambda bi, i: (bi, 0, i)),
            pl.BlockSpec((1, 1, 8, LANES), lambda bi, i: (bi, i, 0, 0)),
            pl.BlockSpec((1, 1, FOX_HEADS, LANES), lambda bi, i: (bi, i, 0, 0)),
        ],
        out_shape=[
            jax.ShapeDtypeStruct((b, s, 2 * width), BF16),
            jax.ShapeDtypeStruct((b, width, s), BF16),
            jax.ShapeDtypeStruct((b, s // ch, width, ch), BF16),
            jax.ShapeDtypeStruct((b, width, s), BF16),
            jax.ShapeDtypeStruct((b, FOX_HEADS * AUG_SLOTS_PER_HEAD, s), BF16),
            jax.ShapeDtypeStruct((b, s // tm, 8, LANES), F32),
            jax.ShapeDtypeStruct((b, s // tm, FOX_HEADS, LANES), F32),
        ],
        scratch_shapes=[pltpu.VMEM((FOX_HEADS, LANES), F32)],
        compiler_params=_cparams(2),
        name="fox_in",
    )(x, norm_g.reshape(1, d), wk.astype(BF16), wt, wft, bt, triu, pk, pqt, gk, gq)


def _fox_attn_kernel(kstart_ref, kaug_ref, vt_ref, qt_ref, augq_ref, ot_ref,
                     s_scr, p_scr, *, tq, tk, pairs_per_step):
    qi = pl.program_id(2)
    hd = FOX_HEAD_DIM
    zeros_hd = jnp.zeros((hd, tq), BF16)
    zeros_tail = jnp.zeros((LANES - AUG_SLOTS_PER_HEAD, tq), BF16)
    ones_block = lambda n: jnp.where(lax.broadcasted_iota(jnp.int32, (16, n), 0) == 0, 1.0, 0.0).astype(BF16)
    ones_rows = ones_block(tk)
    n_blocks = (qi * tq) // tk + 1
    k_first = kstart_ref[pl.program_id(0), pl.program_id(1), qi]

    heads = [(u, h) for u in range(pairs_per_step) for h in range(2)]
    w_head = {}
    for u, h in heads:
        q_rows = qt_ref[0, LANES * u + hd * h:LANES * u + hd * (h + 1), :]
        aug_rows = augq_ref[0, 2 * AUG_SLOTS_PER_HEAD * u + AUG_SLOTS_PER_HEAD * h:
                            2 * AUG_SLOTS_PER_HEAD * u + AUG_SLOTS_PER_HEAD * (h + 1), :]
        q_part = [q_rows, zeros_hd] if h == 0 else [zeros_hd, q_rows]
        w_head[u, h] = jnp.concatenate(q_part + [aug_rows, zeros_tail], axis=0)
    chains = [(u, h, sub) for u, h in heads for sub in range(tq // QUERY_LANES)]

    def issue_scores(kj, slot):
        ks = pl.multiple_of(kj * tk, tk)
        block_max = []
        for c, (u, h, sub) in enumerate(chains):
            kb = kaug_ref[0, pl.ds(ks, tk), 2 * LANES * u:2 * LANES * (u + 1)]
            s = _dot(kb, w_head[u, h][:, QUERY_LANES * sub:QUERY_LANES * (sub + 1)])
            s_scr[slot, c] = s
            block_max.append(jnp.max(s, axis=0, keepdims=True))
        return tuple(block_max)

    tri_key = lax.broadcasted_iota(jnp.int32, (QUERY_LANES, QUERY_LANES), 0)
    tri_qry = lax.broadcasted_iota(jnp.int32, (QUERY_LANES, QUERY_LANES), 1)
    causal = jnp.where(tri_key <= tri_qry, 0.0, MASK_VALUE)

    def pv_block(kj, c):
        u, h, _ = chains[c]
        v_rows = vt_ref[0, kj, LANES * u + hd * h:LANES * u + hd * (h + 1), :]
        return _dot(jnp.concatenate([v_rows, ones_rows], axis=0), p_scr[c])

    def step(kj, slot, state, has_prev=True):
        carry, block_max = state
        next_max = issue_scores(kj + 1, 1 - slot)
        out = []
        for c in range(len(chains)):
            m_old, alpha_prev, acc = carry[c]
            if has_prev:
                acc = acc * alpha_prev + pv_block(kj - 1, c)
            m_new = jnp.maximum(m_old, block_max[c])
            p_scr[c] = jnp.exp2(s_scr[slot, c] - m_new).astype(BF16)
            out.append((m_new, jnp.exp2(m_old - m_new), acc))
        return tuple(out), next_max

    def diagonal(kj, slot, state):
        carry, _ = state
        for c, (u, h, sub) in enumerate(chains):
            m_old, alpha_prev, acc = carry[c]
            acc = acc * alpha_prev + pv_block(jnp.maximum(kj - 1, 0), c)
            lo = QUERY_LANES * sub
            vis = lo + QUERY_LANES
            mid = s_scr[slot, c, lo:vis, :] + causal
            m_new = jnp.maximum(m_old, jnp.max(mid, axis=0, keepdims=True))
            if lo:
                top = s_scr[slot, c, 0:lo, :]
                m_new = jnp.maximum(m_new, jnp.max(top, axis=0, keepdims=True))
                p = jnp.concatenate([jnp.exp2(top - m_new).astype(BF16),
                                     jnp.exp2(mid - m_new).astype(BF16)], axis=0)
            else:
                p = jnp.exp2(mid - m_new).astype(BF16)
            v_rows = vt_ref[0, kj, LANES * u + hd * h:LANES * u + hd * (h + 1), 0:vis]
            pv = _dot(jnp.concatenate([v_rows, ones_block(vis)], axis=0), p)
            acc = acc * jnp.exp2(m_old - m_new) + pv
            ot_ref[0, LANES * u + hd * h:LANES * u + hd * (h + 1),
                   QUERY_LANES * sub:QUERY_LANES * (sub + 1)] = (acc[0:hd] / acc[hd:hd + 1]).astype(BF16)

    row = lambda value: jnp.full((1, QUERY_LANES), value, F32)
    carry = tuple((row(MASK_VALUE), row(1.0), jnp.zeros((hd + 16, QUERY_LANES), F32)) for _ in chains)
    state = (carry, tuple(row(0.0) for _ in chains))
    n_unmasked = n_blocks - 1 - k_first
    has_unmasked = jnp.minimum(n_unmasked, 1)

    def first(_, st):
        return step(k_first, 0, (st[0], issue_scores(k_first, 0)), has_prev=False)

    def only_diagonal(_, st):
        p_scr[...] = jnp.zeros_like(p_scr)
        return st[0], issue_scores(k_first, 0)

    def pair(i, st):
        kj = k_first + 1 + 2 * i
        return step(kj + 1, 0, step(kj, 1, st))

    state = lax.fori_loop(0, has_unmasked, first, state)
    state = lax.fori_loop(0, 1 - has_unmasked, only_diagonal, state)
    n_rest = jnp.maximum(n_unmasked - 1, 0)
    state = lax.fori_loop(0, n_rest // 2, pair, state)
    state = lax.fori_loop(0, lax.rem(n_rest, 2), lambda _, st: step(n_blocks - 2, 1, st), state)
    diagonal(n_blocks - 1, lax.rem(n_unmasked, 2), state)


def _fox_prune_table(kst, qst, heads_per_step):
    kn = jnp.sqrt(kst[:, :, 0, :FOX_HEADS]) * NORM_SLACK
    qn = jnp.sqrt(qst[:, :, :, 0]) * NORM_SLACK
    c_first = kst[:, :, 1, :FOX_HEADS] * LOG2E
    c_last = kst[:, :, 2, :FOX_HEADS] * LOG2E
    bound = (qn[:, :, None, :] * kn[:, None, :, :] + c_first[:, :, None, :] - c_last[:, None, :, :]
             + (qn * kn)[:, :, None, :])
    n = kst.shape[1]
    earlier = jnp.arange(n)[None, :, None] < jnp.arange(n)[:, None, None]
    skip = (bound < -PRUNE_LOG2) & earlier[None]
    block = jnp.arange(n, dtype=jnp.int32)[None, None, :, None]
    lead = jnp.min(jnp.where(skip, n, block), axis=2)
    lead = jnp.min(lead.reshape(lead.shape[0], n, FOX_HEADS // heads_per_step, heads_per_step), axis=3)
    return jnp.transpose(lead, (0, 2, 1)).astype(jnp.int32)


def _fox_attn(kaug, qt, vt, augq, kstart, *, tq, tk, pairs_per_step):
    b, s, _ = kaug.shape
    width = FOX_HEADS * FOX_HEAD_DIM
    nk = s // tk
    pps = pairs_per_step
    n_chains = 2 * pps * (tq // QUERY_LANES)
    grid_spec = pltpu.PrefetchScalarGridSpec(
        num_scalar_prefetch=1,
        grid=(b, FOX_PAIRS // pps, s // tq),
        in_specs=[
            pl.BlockSpec((1, s, 2 * LANES * pps), lambda bi, p, qi, ks: (bi, 0, p)),
            pl.BlockSpec((1, nk, LANES * pps, tk), lambda bi, p, qi, ks: (bi, 0, p, 0)),
            pl.BlockSpec((1, LANES * pps, tq), lambda bi, p, qi, ks: (bi, p, qi)),
            pl.BlockSpec((1, 2 * AUG_SLOTS_PER_HEAD * pps, tq), lambda bi, p, qi, ks: (bi, p, qi)),
        ],
        out_specs=pl.BlockSpec((1, LANES * pps, tq), lambda bi, p, qi, ks: (bi, p, qi)),
        scratch_shapes=[pltpu.VMEM((2, n_chains, tk, QUERY_LANES), F32),
                        pltpu.VMEM((n_chains, tk, QUERY_LANES), BF16)],
    )
    return pl.pallas_call(
        functools.partial(_fox_attn_kernel, tq=tq, tk=tk, pairs_per_step=pps),
        grid_spec=grid_spec,
        out_shape=jax.ShapeDtypeStruct((b, width, s), BF16),
        compiler_params=_cparams(3),
        name="fox_attn",
    )(kstart, kaug, vt, qt, augq)


def _ple(h1, p, wup_ref, wgate_ref):
    gate = jax.nn.sigmoid(_dot(h1.astype(BF16), wgate_ref[...]))
    up = _dot(p.astype(BF16), wup_ref[...])
    return h1 + up * gate


def _fox_out_kernel(ot_ref, zt_ref, x_ref, p_ref, wout_ref, wup_ref, wgate_ref, h_ref):
    z = zt_ref[0].astype(F32)
    gt = (ot_ref[0].astype(F32) * (z * jax.nn.sigmoid(z))).astype(BF16)
    y = lax.dot_general(gt, wout_ref[...], TN_DIMS, preferred_element_type=F32)
    h_ref[0] = _ple(x_ref[0] + y, p_ref[0], wup_ref, wgate_ref)


def _fox_out(ot, zt, x, p, w_out, w_up, w_gate, *, tm):
    b, s, d = x.shape
    width = ot.shape[1]
    pd = p.shape[-1]
    const = lambda shape: pl.BlockSpec(shape, lambda bi, i: (0,) * len(shape))
    return pl.pallas_call(
        _fox_out_kernel,
        grid=(b, s // tm),
        in_specs=[
            pl.BlockSpec((1, width, tm), lambda bi, i: (bi, 0, i)),
            pl.BlockSpec((1, width, tm), lambda bi, i: (bi, 0, i)),
            pl.BlockSpec((1, tm, d), lambda bi, i: (bi, i, 0)),
            pl.BlockSpec((1, tm, pd), lambda bi, i: (bi, i, 0)),
            const((width, d)), const((pd, d)), const((d, d)),
        ],
        out_specs=pl.BlockSpec((1, tm, d), lambda bi, i: (bi, i, 0)),
        out_shape=jax.ShapeDtypeStruct((b, s, d), F32),
        compiler_params=_cparams(2),
        name="fox_out",
    )(ot, zt, x, p, w_out.astype(BF16), w_up.astype(BF16), w_gate.astype(BF16))


PERM_ROWS = 256
DIL_SUB_BLOCKS = 8


def _phase_major_matrix(dilation):
    n = PERM_ROWS // dilation
    p = np.zeros((PERM_ROWS, PERM_ROWS), np.float32)
    for r in range(dilation):
        for i in range(n):
            p[r * n + i, dilation * i + r] = 1.0
    return p


def _dil_in_kernel(h_ref, g_ref, *rest, dilations, parts, tm):
    n_w = sum(parts)
    n_perm = sum(d > 1 for d in dilations)
    w_refs, perm_refs, out_refs = rest[:n_w], list(rest[n_w:n_w + n_perm]), rest[n_w + n_perm:]
    hn = _rms_norm(h_ref[0], g_ref[...]).astype(BF16)
    first = 0
    for dil, n_parts, o_ref in zip(dilations, parts, out_refs):
        group_w = w_refs[first:first + n_parts]
        first += n_parts
        pc = group_w[0].shape[1]
        if dil == 1:
            for part, w_ref in enumerate(group_w):
                o_ref[0, :, pc * part:pc * (part + 1)] = _dot(hn, w_ref[...]).astype(BF16)
            continue
        perm = perm_refs.pop(0)[...]
        n = PERM_ROWS // dil
        for c in range(tm // PERM_ROWS):
            rows = slice(PERM_ROWS * c, PERM_ROWS * (c + 1))
            hp = _dot(perm, hn[rows]).astype(BF16)
            for part, w_ref in enumerate(group_w):
                res = _dot(hp, w_ref[...]).astype(BF16)
                for r in range(dil):
                    dst = pc * (part * dil + r)
                    o_ref[0, n * c:n * (c + 1), dst:dst + pc] = res[n * r:n * (r + 1)]


def _dil_in(h, norm_g, w, groups, dilations, *, tm, part_cols):
    b, s, d = h.shape
    parts = tuple(len(blocks) for blocks in groups)
    perms = [jnp.asarray(_phase_major_matrix(dil), BF16) for dil in dilations if dil > 1]
    const = lambda shape: pl.BlockSpec(shape, lambda bi, i: (0,) * len(shape))
    w_specs = [pl.BlockSpec((d, part_cols), lambda bi, i, cb=cb: (0, cb))
               for blocks in groups for cb in blocks]
    widths = [dil * n_parts * part_cols for dil, n_parts in zip(dilations, parts)]
    return pl.pallas_call(
        functools.partial(_dil_in_kernel, dilations=dilations, parts=parts, tm=tm),
        grid=(b, s // tm),
        in_specs=[pl.BlockSpec((1, tm, d), lambda bi, i: (bi, i, 0)), const((1, d))]
                 + w_specs + [const(p.shape) for p in perms],
        out_specs=[pl.BlockSpec((1, tm // dil, width), lambda bi, i: (bi, i, 0))
                   for dil, width in zip(dilations, widths)],
        out_shape=[jax.ShapeDtypeStruct((b, s // dil, width), BF16)
                   for dil, width in zip(dilations, widths)],
        compiler_params=_cparams(2),
        name="dil_in_" + "_".join(str(dil) for dil in dilations),
    )(h, norm_g.reshape(1, d), *([w] * sum(parts)), *perms)


def _dil_attn_kernel(q_ref, kc_ref, vc_ref, kp_ref, vp_ref, o_ref, st_ref,
                     kcat, vcat, bias_scr, *, dilation, kb, phases, neg_slopes):
    n = pl.program_id(2)
    nw = DIL_WINDOW_STEPS
    hd = DIL_HEAD_DIM
    gw = DIL_HEADS_PER_GROUP * hd

    @pl.when((pl.program_id(0) == 0) & (pl.program_id(1) == 0) & (n == 0))
    def _():
        row = lax.broadcasted_iota(jnp.int32, (nw, 2 * nw), 0)
        col = lax.broadcasted_iota(jnp.int32, (nw, 2 * nw), 1)
        dist = nw + row - col
        band = (dist >= 0) & (dist <= nw)
        dist_f = (dist * dilation).astype(F32)
        for h in range(DIL_HEADS_PER_GROUP):
            bias_scr[h] = jnp.where(band, neg_slopes[h] * dist_f * LOG2E, MASK_VALUE)

    for ph in range(phases):
        cols = slice(gw * ph, gw * (ph + 1))
        kcat[ph, 0:nw] = kp_ref[0, :, cols]
        kcat[ph, nw:nw * (kb + 1)] = kc_ref[0, :, cols]
        vcat[ph, 0:nw] = vp_ref[0, :, cols]
        vcat[ph, nw:nw * (kb + 1)] = vc_ref[0, :, cols]
    lane = lax.broadcasted_iota(jnp.int32, (nw, LANES), 1)
    col = lax.broadcasted_iota(jnp.int32, (nw, 2 * nw), 1)
    no_history = jnp.where(col < jnp.where(n == 0, nw, 0), MASK_VALUE, 0.0)

    heads = [slice(hd * h, hd * (h + 1)) for h in range(DIL_HEADS_PER_GROUP)]
    problems = [(ph, jb) for ph in range(phases) for jb in range(kb)]
    scores = [[lax.dot_general(q_ref[0, nw * jb:nw * (jb + 1), gw * ph + hd * h:gw * ph + hd * (h + 1)],
                               kcat[ph, nw * jb:nw * (jb + 2), hs], NT_DIMS,
                               preferred_element_type=F32) for h, hs in enumerate(heads)]
              for ph, jb in problems]
    for i, (ph, jb) in enumerate(problems):
        rows = slice(nw * jb, nw * (jb + 1))
        stat_tile = jnp.zeros((nw, LANES), F32)
        for h, hs in enumerate(heads):
            t = scores[i][h] + bias_scr[h]
            if jb == 0:
                t = t + no_history
            m = jnp.max(t, axis=1, keepdims=True)
            p = jnp.exp2(t - m)
            l = jnp.sum(p, axis=1, keepdims=True)
            o_ref[0, rows, gw * ph + hd * h:gw * ph + hd * (h + 1)] = _dot(
                p.astype(BF16), vcat[ph, nw * jb:nw * (jb + 2), hs]).astype(BF16)
            stat_tile = jnp.where(lane == h, m, stat_tile)
            stat_tile = jnp.where(lane == h + DIL_HEADS_PER_GROUP, l, stat_tile)
        st_ref[0, rows, LANES * ph:LANES * (ph + 1)] = stat_tile


def _alibi_neg_slopes(group):
    n = len(DIL_PATTERN) * DIL_HEADS_PER_GROUP
    k = np.arange(1, n + 1, dtype=np.float32)
    slopes = np.float32(2.0) ** (np.float32(-ALIBI_MAX_EXP) * k / np.float32(n))
    lo = group * DIL_HEADS_PER_GROUP
    return tuple(float(-v) for v in slopes[lo:lo + DIL_HEADS_PER_GROUP])


def _dil_attn(view, group, *, kb, phases):
    _, dilation = DIL_PATTERN[group]
    b, length, _ = view.shape
    nw = DIL_WINDOW_STEPS
    gw = DIL_HEADS_PER_GROUP * DIL_HEAD_DIM
    rows = nw * kb
    steps = dilation // phases

    def cur(part):
        return pl.BlockSpec((1, rows, phases * gw), lambda bi, r, n: (bi, n, part * steps + r))

    def prev(part):
        return pl.BlockSpec((1, nw, phases * gw),
                            lambda bi, r, n: (bi, jnp.maximum(n * kb - 1, 0), part * steps + r))

    o, st = pl.pallas_call(
        functools.partial(_dil_attn_kernel, dilation=dilation, kb=kb, phases=phases,
                          neg_slopes=_alibi_neg_slopes(group)),
        grid=(b, steps, length // rows),
        in_specs=[cur(0), cur(1), cur(2), prev(1), prev(2)],
        out_specs=[
            pl.BlockSpec((1, rows, phases * gw), lambda bi, r, n: (bi, n, r)),
            pl.BlockSpec((1, rows, phases * LANES), lambda bi, r, n: (bi, n, r)),
        ],
        out_shape=[
            jax.ShapeDtypeStruct((b, length, dilation * gw), BF16),
            jax.ShapeDtypeStruct((b, length, dilation * LANES), F32),
        ],
        scratch_shapes=[pltpu.VMEM((phases, nw * (kb + 1), gw), BF16),
                        pltpu.VMEM((phases, nw * (kb + 1), gw), BF16),
                        pltpu.VMEM((DIL_HEADS_PER_GROUP, nw, 2 * nw), F32)],
        compiler_params=_cparams(3),
        name=f"dil_attn_g{group}",
    )(view, view, view, view, view)
    return o, st


def _natural_rows(blk, perm_t, dilation, tm):
    cols = blk.shape[1] // dilation
    if dilation == 1:
        return blk
    n = PERM_ROWS // dilation
    chunks = []
    for c in range(tm // PERM_ROWS):
        phase_major = jnp.concatenate(
            [blk[n * c:n * (c + 1), cols * r:cols * (r + 1)] for r in range(dilation)], axis=0)
        chunks.append(_dot(perm_t, phase_major))
    return jnp.concatenate(chunks, axis=0)


def _dil_out_kernel(o0_ref, o1_ref, o2_ref, s0_ref, s1_ref, s2_ref, z_ref, h_ref, p_ref,
                    pt1_ref, pt2_ref, e_ref, wout_ref, wup_ref, wgate_ref, g_ref,
                    out_ref, *, dilations, tm):
    perm_ts = (None, pt1_ref[...], pt2_ref[...])
    n_heads = DIL_HEADS_PER_GROUP
    row_max, row_sum = [], []
    for st_ref, perm_t, dil in zip((s0_ref, s1_ref, s2_ref), perm_ts, dilations):
        if dil == 1:
            st = st_ref[0]
        else:
            hi, mid, lo = (_natural_rows(piece.astype(BF16), perm_t, dil, tm)
                           for piece in _split3(st_ref[0]))
            st = hi + mid + lo
        row_max.append(st)
        row_sum.append(pltpu.roll(st, LANES - n_heads, 1))
    m = jnp.maximum(jnp.maximum(row_max[0], row_max[1]), row_max[2])
    e = [jnp.exp2(v - m) for v in row_max]
    inv = 1.0 / (e[0] * row_sum[0] + e[1] * row_sum[1] + e[2] * row_sum[2])
    live = lax.broadcasted_iota(jnp.int32, (tm, LANES), 1) < n_heads
    o = None
    for ev, o_ref, perm_t, dil in zip(e, (o0_ref, o1_ref, o2_ref), perm_ts, dilations):
        w = jnp.where(live, ev * inv, 0.0)
        w_hi = w.astype(BF16)
        w_lo = (w - w_hi.astype(F32)).astype(BF16)
        w_full = _dot(jnp.concatenate([w_hi, w_lo], axis=1), e_ref[...])
        term = w_full * _natural_rows(o_ref[0], perm_t, dil, tm)
        o = term if o is None else o + term
    z = z_ref[0].astype(F32)
    g = (o * (z * jax.nn.sigmoid(z))).astype(BF16)
    h1 = h_ref[0] + _dot(g, wout_ref[...])
    out_ref[0] = _rms_norm(_ple(h1, p_ref[0], wup_ref, wgate_ref), g_ref[...])


def _dil_out(outs, stats, proj0, h, p, w_out, w_up, w_gate, final_g, *, tm):
    b, s, d = h.shape
    gw = DIL_HEADS_PER_GROUP * DIL_HEAD_DIM
    pd = p.shape[-1]
    dilations = tuple(dil for _, dil in DIL_PATTERN)
    z_block = proj0.shape[-1] // gw - 1
    expand = np.zeros((2 * LANES, gw), np.float32)
    for hh in range(DIL_HEADS_PER_GROUP):
        expand[hh, DIL_HEAD_DIM * hh:DIL_HEAD_DIM * (hh + 1)] = 1.0
        expand[LANES + hh, DIL_HEAD_DIM * hh:DIL_HEAD_DIM * (hh + 1)] = 1.0
    perm_ts = [jnp.asarray(_phase_major_matrix(dil).T, BF16) for dil in dilations[1:]]
    const = lambda shape: pl.BlockSpec(shape, lambda bi, i: (0,) * len(shape))
    row = lambda w: pl.BlockSpec((1, tm, w), lambda bi, i: (bi, i, 0))
    view = lambda w: [pl.BlockSpec((1, tm // dil, dil * w), lambda bi, i: (bi, i, 0))
                      for dil in dilations]
    return pl.pallas_call(
        functools.partial(_dil_out_kernel, dilations=dilations, tm=tm),
        grid=(b, s // tm),
        in_specs=view(gw) + view(LANES) + [
            pl.BlockSpec((1, tm, gw), lambda bi, i: (bi, i, z_block)),
            row(d), row(pd),
            const((PERM_ROWS, PERM_ROWS)), const((PERM_ROWS, PERM_ROWS)),
            const((2 * LANES, gw)), const((gw, d)), const((pd, d)), const((d, d)), const((1, d)),
        ],
        out_specs=row(d),
        out_shape=jax.ShapeDtypeStruct((b, s, d), F32),
        compiler_params=_cparams(2),
        name="dil_out",
    )(*outs, *stats, proj0, h, p, *perm_ts, jnp.asarray(expand, BF16),
      w_out.astype(BF16), w_up.astype(BF16), w_gate.astype(BF16), final_g.reshape(1, d))


def kernel(x, p, fox_norm, fox_w_in, fox_b_f, fox_w_out, dil_norm, dil_w_in, dil_w_out,
           ple_w_up, ple_w_gate, final_norm):
    s = x.shape[1]
    tm = min(512, s)
    tq = min(512, s)
    tk = min(512, s)
    assert tm == tq == tk, "the pruning table pairs fox_in tiles with query tiles and key blocks"
    kaug, qt, vt, zt, augq, kst, qst = _fox_in(x, fox_norm[0], fox_w_in[0], fox_b_f[0], tm=tm, ch=tk)
    pps = 2
    ot = _fox_attn(kaug, qt, vt, augq, _fox_prune_table(kst, qst, 2 * pps), tq=tq, tk=tk,
                   pairs_per_step=pps)
    h = _fox_out(ot, zt, x, p[0], fox_w_out[0], ple_w_up[0], ple_w_gate[0], tm=tm)

    gw = DIL_HEADS_PER_GROUP * DIL_HEAD_DIM
    n_groups = len(DIL_PATTERN)
    k_scale = DIL_HEAD_DIM ** -0.5 * LOG2E
    col = jnp.arange(dil_w_in.shape[-1]) // (n_groups * gw)
    w1 = (dil_w_in[0] * jnp.where(col == 1, k_scale, 1.0)[None, :]).astype(BF16)
    qkv = lambda g: [n_groups * part + g for part in range(3)]
    (proj0,) = _dil_in(h, dil_norm[0], w1, [qkv(0) + [3 * n_groups]], (1,), tm=tm, part_cols=gw)
    views = [proj0] + list(_dil_in(h, dil_norm[0], w1, [qkv(1), qkv(2)],
                                   (DIL_PATTERN[1][1], DIL_PATTERN[2][1]), tm=min(PERM_ROWS, s),
                                   part_cols=gw))
    outs, stats = [], []
    for group, (_, dilation) in enumerate(DIL_PATTERN):
        kb = min(DIL_SUB_BLOCKS, s // dilation // DIL_WINDOW_STEPS)
        phases = min(dilation, DIL_SUB_BLOCKS // kb)
        o, st = _dil_attn(views[group], group, kb=kb, phases=phases)
        outs.append(o)
        stats.append(st)
    return _dil_out(outs, stats, proj0, h, p[1], dil_w_out[0], ple_w_up[1], ple_w_gate[1],
                    final_norm, tm=tm)
```

```python
import functools

import numpy as np
import jax
import jax.numpy as jnp
from jax import lax
from jax.experimental import pallas as pl
from jax.experimental.pallas import tpu as pltpu

F32 = jnp.float32
BF16 = jnp.bfloat16

RMS_EPS = 1e-6
FOX_HEADS = 16
FOX_HEAD_DIM = 64
FOX_PAIRS = FOX_HEADS // 2
DIL_PATTERN = ((128, 1), (512, 4), (2048, 16))
DIL_HEADS_PER_GROUP = 8
DIL_HEAD_DIM = 128
DIL_WINDOW_STEPS = 128
ALIBI_MAX_EXP = 8.0
MASK_VALUE = -1e30
LOG2E = 1.4426950408889634

LANES = 128
AUG_SLOTS_PER_HEAD = 16
QUERY_LANES = 256
PRUNE_LOG2 = 50.0
NORM_SLACK = 1.02
VMEM_LIMIT_BYTES = 56 * 1024 * 1024

NT_DIMS = (((1,), (1,)), ((), ()))
TN_DIMS = (((0,), (0,)), ((), ()))


def _cparams(n_axes):
    return pltpu.CompilerParams(
        dimension_semantics=("arbitrary",) * n_axes,
        vmem_limit_bytes=VMEM_LIMIT_BYTES,
    )


def _rms_norm(x, g):
    ms = jnp.mean(x * x, axis=-1, keepdims=True)
    return x * lax.rsqrt(ms + RMS_EPS) * g


def _log_sigmoid(x):
    return jnp.minimum(x, 0.0) - jnp.log1p(jnp.exp(-jnp.abs(x)))


def _split3(x):
    hi = x.astype(BF16).astype(F32)
    r1 = x - hi
    mid = r1.astype(BF16).astype(F32)
    lo = (r1 - mid).astype(BF16).astype(F32)
    return hi, mid, lo


def _dot(a, b):
    return jnp.dot(a, b, preferred_element_type=F32)


def _fox_in_kernel(x_ref, g_ref, wk_ref, wt_ref, wft_ref, bt_ref,
                   triu_ref, pk_ref, pqt_ref, gk_ref, gq_ref,
                   kaug_ref, qt_ref, vt_ref, zt_ref, augq_ref, kst_ref, qst_ref,
                   carryt_ref, *, tm, ch):
    i = pl.program_id(1)

    @pl.when(i == 0)
    def _():
        carryt_ref[...] = jnp.zeros_like(carryt_ref)

    hn = _rms_norm(x_ref[0], g_ref[...]).astype(BF16)
    width = FOX_HEADS * FOX_HEAD_DIM

    ft = lax.dot_general(wft_ref[...], hn, NT_DIMS, preferred_element_type=F32)
    hi, mid, lo = _split3(_log_sigmoid(ft + bt_ref[...]))
    stack = jnp.concatenate([hi, mid, lo], axis=0).astype(BF16)
    cs = _dot(stack, triu_ref[...])
    ct = cs[0:16] + cs[16:32] + cs[32:48] + carryt_ref[:, 0:1]
    carryt_ref[...] = jnp.broadcast_to(ct[:, tm - 1:tm], carryt_ref.shape)
    c = jnp.concatenate([ct, jnp.zeros((LANES - FOX_HEADS, tm), F32)], axis=0).T

    k = _dot(hn, wk_ref[...])
    nhi, nmid, nlo = _split3(c * -LOG2E)
    lane = lax.broadcasted_iota(jnp.int32, (tm, LANES), 1)
    slots = jnp.where(lane < 16, nhi,
                      jnp.where(lane < 32, pltpu.roll(nmid, 16, 1),
                                jnp.where(lane < 48, pltpu.roll(nlo, 32, 1),
                                          jnp.where(lane == 48, 1.0, 0.0))))
    kaug = _dot(slots.astype(BF16), pk_ref[...])
    kb = k.astype(BF16)
    for p in range(FOX_PAIRS):
        kaug_ref[0, :, 2 * LANES * p:2 * LANES * p + LANES] = kb[:, LANES * p:LANES * (p + 1)]
        kaug_ref[0, :, 2 * LANES * p + LANES:2 * LANES * (p + 1)] = (
            kaug[:, LANES * p:LANES * (p + 1)].astype(BF16))
    kf = kb.astype(F32)
    kn2 = jnp.max(_dot((kf * kf).astype(BF16), gk_ref[...]), axis=0, keepdims=True)
    kst_ref[0, 0] = jnp.concatenate(
        [kn2, c[0:1, :], c[tm - 1:tm, :], jnp.zeros((5, LANES), F32)], axis=0)

    rows = 512
    for c0 in range(0, 3 * width, rows):
        res = lax.dot_general(wt_ref[c0:c0 + rows, :], hn, NT_DIMS,
                              preferred_element_type=F32).astype(BF16)
        which, off = divmod(c0, width)
        if which == 0:
            qt_ref[0, off:off + rows, :] = res
        elif which == 1:
            for j in range(tm // ch):
                vt_ref[0, j, off:off + rows, :] = res[:, j * ch:(j + 1) * ch]
        else:
            zt_ref[0, off:off + rows, :] = res

    hi, mid, lo = _split3(ct * LOG2E)
    ones = jnp.where(lax.broadcasted_iota(jnp.int32, (16, tm), 0) == 0, 1.0, 0.0)
    slots_t = jnp.concatenate([hi, mid, lo, ones], axis=0).astype(BF16)
    augq_ref[0] = _dot(pqt_ref[...], slots_t).astype(BF16)
    qf = qt_ref[0].astype(F32)
    qn2 = jnp.max(_dot(gq_ref[...], (qf * qf).astype(BF16)), axis=1, keepdims=True)
    qst_ref[0, 0] = jnp.broadcast_to(qn2, (FOX_HEADS, LANES))


def _placement_matrices():
    pk = np.zeros((LANES, FOX_PAIRS * LANES), np.float32)
    pqt = np.zeros((FOX_HEADS * AUG_SLOTS_PER_HEAD, 64), np.float32)
    for h in range(FOX_HEADS):
        p, odd = divmod(h, 2)
        base = 6 * odd
        for piece in range(3):
            pk[48, LANES * p + base + piece] = 1.0
            pk[16 * piece + h, LANES * p + base + 3 + piece] = 1.0
            pqt[AUG_SLOTS_PER_HEAD * h + base + piece, 16 * piece + h] = 1.0
            pqt[AUG_SLOTS_PER_HEAD * h + base + 3 + piece, 48] = 1.0
    return jnp.asarray(pk, BF16), jnp.asarray(pqt, BF16)


def _fox_in(x, norm_g, w_in, b_f, *, tm, ch):
    b, s, d = x.shape
    width = FOX_HEADS * FOX_HEAD_DIM
    wq = w_in[:, :width] * (FOX_HEAD_DIM ** -0.5 * LOG2E)
    wk = w_in[:, width:2 * width]
    wv = w_in[:, 2 * width:3 * width]
    wz = w_in[:, 3 * width:4 * width]
    wf = w_in[:, 4 * width:]
    wt = jnp.concatenate([wq, wv, wz], axis=1).astype(BF16).T
    wft = wf.T.astype(BF16)
    bt = b_f.reshape(FOX_HEADS, 1)
    r = np.arange(tm)
    triu = jnp.asarray(r[:, None] <= r[None, :], BF16)
    pk, pqt = _placement_matrices()
    head_of = np.arange(width) // FOX_HEAD_DIM
    gk = jnp.asarray(head_of[:, None] == np.arange(LANES)[None, :], BF16)
    gq = jnp.asarray(np.arange(FOX_HEADS)[:, None] == head_of[None, :], BF16)

    const = lambda shape: pl.BlockSpec(shape, lambda bi, i: (0,) * len(shape))
    return pl.pallas_call(
        functools.partial(_fox_in_kernel, tm=tm, ch=ch),
        grid=(b, s // tm),
        in_specs=[
            pl.BlockSpec((1, tm, d), lambda bi, i: (bi, i, 0)),
            const((1, d)),
            const(wk.shape), const(wt.shape), const(wft.shape),
            const((FOX_HEADS, 1)),
            const((tm, tm)),
            const(pk.shape), const(pqt.shape), const(gk.shape), const(gq.shape),
        ],
        out_specs=[
            pl.BlockSpec((1, tm, 2 * width), lambda bi, i: (bi, i, 0)),
            pl.BlockSpec((1, width, tm), lambda bi, i: (bi, 0, i)),
            pl.BlockSpec((1, tm // ch, width, ch), lambda bi, i: (bi, i, 0, 0)),
            pl.BlockSpec((1, width, tm), lambda bi, i: (bi, 0, i)),
            pl.BlockSpec((1, FOX_HEADS * AUG_SLOTS_PER_HEAD, tm), lambda bi, i: (bi, 0, i)),
            pl.BlockSpec((1, 1, 8, LANES), lambda bi, i: (bi, i, 0, 0)),
            pl.BlockSpec((1, 1, FOX_HEADS, LANES), lambda bi, i: (bi, i, 0, 0)),
        ],
        out_shape=[
            jax.ShapeDtypeStruct((b, s, 2 * width), BF16),
            jax.ShapeDtypeStruct((b, width, s), BF16),
            jax.ShapeDtypeStruct((b, s // ch, width, ch), BF16),
            jax.ShapeDtypeStruct((b, width, s), BF16),
            jax.ShapeDtypeStruct((b, FOX_HEADS * AUG_SLOTS_PER_HEAD, s), BF16),
            jax.ShapeDtypeStruct((b, s // tm, 8, LANES), F32),
            jax.ShapeDtypeStruct((b, s // tm, FOX_HEADS, LANES), F32),
        ],
        scratch_shapes=[pltpu.VMEM((FOX_HEADS, LANES), F32)],
        compiler_params=_cparams(2),
        name="fox_in",
    )(x, norm_g.reshape(1, d), wk.astype(BF16), wt, wft, bt, triu, pk, pqt, gk, gq)


def _fox_attn_kernel(kstart_ref, kaug_ref, vt_ref, qt_ref, augq_ref, ot_ref,
                     s_scr, p_scr, *, tq, tk, pairs_per_step):
    qi = pl.program_id(2)
    hd = FOX_HEAD_DIM
    zeros_hd = jnp.zeros((hd, tq), BF16)
    zeros_tail = jnp.zeros((LANES - AUG_SLOTS_PER_HEAD, tq), BF16)
    ones_block = lambda n: jnp.where(lax.broadcasted_iota(jnp.int32, (16, n), 0) == 0, 1.0, 0.0).astype(BF16)
    ones_rows = ones_block(tk)
    n_blocks = (qi * tq) // tk + 1
    k_first = kstart_ref[pl.program_id(0), pl.program_id(1), qi]

    heads = [(u, h) for u in range(pairs_per_step) for h in range(2)]
    w_head = {}
    for u, h in heads:
        q_rows = qt_ref[0, LANES * u + hd * h:LANES * u + hd * (h + 1), :]
        aug_rows = augq_ref[0, 2 * AUG_SLOTS_PER_HEAD * u + AUG_SLOTS_PER_HEAD * h:
                            2 * AUG_SLOTS_PER_HEAD * u + AUG_SLOTS_PER_HEAD * (h + 1), :]
        q_part = [q_rows, zeros_hd] if h == 0 else [zeros_hd, q_rows]
        w_head[u, h] = jnp.concatenate(q_part + [aug_rows, zeros_tail], axis=0)
    chains = [(u, h, sub) for u, h in heads for sub in range(tq // QUERY_LANES)]

    def issue_scores(kj, slot):
        ks = pl.multiple_of(kj * tk, tk)
        block_max = []
        for c, (u, h, sub) in enumerate(chains):
            kb = kaug_ref[0, pl.ds(ks, tk), 2 * LANES * u:2 * LANES * (u + 1)]
            s = _dot(kb, w_head[u, h][:, QUERY_LANES * sub:QUERY_LANES * (sub + 1)])
            s_scr[slot, c] = s
            block_max.append(jnp.max(s, axis=0, keepdims=True))
        return tuple(block_max)

    tri_key = lax.broadcasted_iota(jnp.int32, (QUERY_LANES, QUERY_LANES), 0)
    tri_qry = lax.broadcasted_iota(jnp.int32, (QUERY_LANES, QUERY_LANES), 1)
    causal = jnp.where(tri_key <= tri_qry, 0.0, MASK_VALUE)

    def pv_block(kj, c):
        u, h, _ = chains[c]
        v_rows = vt_ref[0, kj, LANES * u + hd * h:LANES * u + hd * (h + 1), :]
        return _dot(jnp.concatenate([v_rows, ones_rows], axis=0), p_scr[c])

    def step(kj, slot, state, has_prev=True):
        carry, block_max = state
        next_max = issue_scores(kj + 1, 1 - slot)
        out = []
        for c in range(len(chains)):
            m_old, alpha_prev, acc = carry[c]
            if has_prev:
                acc = acc * alpha_prev + pv_block(kj - 1, c)
            m_new = jnp.maximum(m_old, block_max[c])
            p_scr[c] = jnp.exp2(s_scr[slot, c] - m_new).astype(BF16)
            out.append((m_new, jnp.exp2(m_old - m_new), acc))
        return tuple(out), next_max

    def diagonal(kj, slot, state):
        carry, _ = state
        for c, (u, h, sub) in enumerate(chains):
            m_old, alpha_prev, acc = carry[c]
            acc = acc * alpha_prev + pv_block(jnp.maximum(kj - 1, 0), c)
            lo = QUERY_LANES * sub
            vis = lo + QUERY_LANES
            mid = s_scr[slot, c, lo:vis, :] + causal
            m_new = jnp.maximum(m_old, jnp.max(mid, axis=0, keepdims=True))
            if lo:
                top = s_scr[slot, c, 0:lo, :]
                m_new = jnp.maximum(m_new, jnp.max(top, axis=0, keepdims=True))
                p = jnp.concatenate([jnp.exp2(top - m_new).astype(BF16),
                                     jnp.exp2(mid - m_new).astype(BF16)], axis=0)
            else:
                p = jnp.exp2(mid - m_new).astype(BF16)
            v_rows = vt_ref[0, kj, LANES * u + hd * h:LANES * u + hd * (h + 1), 0:vis]
            pv = _dot(jnp.concatenate([v_rows, ones_block(vis)], axis=0), p)
            acc = acc * jnp.exp2(m_old - m_new) + pv
            ot_ref[0, LANES * u + hd * h:LANES * u + hd * (h + 1),
                   QUERY_LANES * sub:QUERY_LANES * (sub + 1)] = (acc[0:hd] / acc[hd:hd + 1]).astype(BF16)

    row = lambda value: jnp.full((1, QUERY_LANES), value, F32)
    carry = tuple((row(MASK_VALUE), row(1.0), jnp.zeros((hd + 16, QUERY_LANES), F32)) for _ in chains)
    state = (carry, tuple(row(0.0) for _ in chains))
    n_unmasked = n_blocks - 1 - k_first
    has_unmasked = jnp.minimum(n_unmasked, 1)

    def first(_, st):
        return step(k_first, 0, (st[0], issue_scores(k_first, 0)), has_prev=False)

    def only_diagonal(_, st):
        p_scr[...] = jnp.zeros_like(p_scr)
        return st[0], issue_scores(k_first, 0)

    def pair(i, st):
        kj = k_first + 1 + 2 * i
        return step(kj + 1, 0, step(kj, 1, st))

    state = lax.fori_loop(0, has_unmasked, first, state)
    state = lax.fori_loop(0, 1 - has_unmasked, only_diagonal, state)
    n_rest = jnp.maximum(n_unmasked - 1, 0)
    state = lax.fori_loop(0, n_rest // 2, pair, state)
    state = lax.fori_loop(0, lax.rem(n_rest, 2), lambda _, st: step(n_blocks - 2, 1, st), state)
    diagonal(n_blocks - 1, lax.rem(n_unmasked, 2), state)


def _fox_prune_table(kst, qst, heads_per_step):
    kn = jnp.sqrt(kst[:, :, 0, :FOX_HEADS]) * NORM_SLACK
    qn = jnp.sqrt(qst[:, :, :, 0]) * NORM_SLACK
    c_first = kst[:, :, 1, :FOX_HEADS] * LOG2E
    c_last = kst[:, :, 2, :FOX_HEADS] * LOG2E
    bound = (qn[:, :, None, :] * kn[:, None, :, :] + c_first[:, :, None, :] - c_last[:, None, :, :]
             + (qn * kn)[:, :, None, :])
    n = kst.shape[1]
    earlier = jnp.arange(n)[None, :, None] < jnp.arange(n)[:, None, None]
    skip = (bound < -PRUNE_LOG2) & earlier[None]
    block = jnp.arange(n, dtype=jnp.int32)[None, None, :, None]
    lead = jnp.min(jnp.where(skip, n, block), axis=2)
    lead = jnp.min(lead.reshape(lead.shape[0], n, FOX_HEADS // heads_per_step, heads_per_step), axis=3)
    return jnp.transpose(lead, (0, 2, 1)).astype(jnp.int32)


def _fox_attn(kaug, qt, vt, augq, kstart, *, tq, tk, pairs_per_step):
    b, s, _ = kaug.shape
    width = FOX_HEADS * FOX_HEAD_DIM
    nk = s // tk
    pps = pairs_per_step
    n_chains = 2 * pps * (tq // QUERY_LANES)
    grid_spec = pltpu.PrefetchScalarGridSpec(
        num_scalar_prefetch=1,
        grid=(b, FOX_PAIRS // pps, s // tq),
        in_specs=[
            pl.BlockSpec((1, s, 2 * LANES * pps), lambda bi, p, qi, ks: (bi, 0, p)),
            pl.BlockSpec((1, nk, LANES * pps, tk), lambda bi, p, qi, ks: (bi, 0, p, 0)),
            pl.BlockSpec((1, LANES * pps, tq), lambda bi, p, qi, ks: (bi, p, qi)),
            pl.BlockSpec((1, 2 * AUG_SLOTS_PER_HEAD * pps, tq), lambda bi, p, qi, ks: (bi, p, qi)),
        ],
        out_specs=pl.BlockSpec((1, LANES * pps, tq), lambda bi, p, qi, ks: (bi, p, qi)),
        scratch_shapes=[pltpu.VMEM((2, n_chains, tk, QUERY_LANES), F32),
                        pltpu.VMEM((n_chains, tk, QUERY_LANES), BF16)],
    )
    return pl.pallas_call(
        functools.partial(_fox_attn_kernel, tq=tq, tk=tk, pairs_per_step=pps),
        grid_spec=grid_spec,
        out_shape=jax.ShapeDtypeStruct((b, width, s), BF16),
        compiler_params=_cparams(3),
        name="fox_attn",
    )(kstart, kaug, vt, qt, augq)


def _ple(h1, p, wup_ref, wgate_ref):
    gate = jax.nn.sigmoid(_dot(h1.astype(BF16), wgate_ref[...]))
    up = _dot(p.astype(BF16), wup_ref[...])
    return h1 + up * gate


def _fox_out_kernel(ot_ref, zt_ref, x_ref, p_ref, wout_ref, wup_ref, wgate_ref, h_ref):
    z = zt_ref[0].astype(F32)
    gt = (ot_ref[0].astype(F32) * (z * jax.nn.sigmoid(z))).astype(BF16)
    y = lax.dot_general(gt, wout_ref[...], TN_DIMS, preferred_element_type=F32)
    h_ref[0] = _ple(x_ref[0] + y, p_ref[0, 0], wup_ref, wgate_ref)


def _fox_out(ot, zt, x, p, layer, w_out, w_up, w_gate, *, tm):
    b, s, d = x.shape
    width = ot.shape[1]
    pd = p.shape[-1]
    const = lambda shape: pl.BlockSpec(shape, lambda bi, i: (0,) * len(shape))
    return pl.pallas_call(
        _fox_out_kernel,
        grid=(b, s // tm),
        in_specs=[
            pl.BlockSpec((1, width, tm), lambda bi, i: (bi, 0, i)),
            pl.BlockSpec((1, width, tm), lambda bi, i: (bi, 0, i)),
            pl.BlockSpec((1, tm, d), lambda bi, i: (bi, i, 0)),
            pl.BlockSpec((1, 1, tm, pd), lambda bi, i: (layer, bi, i, 0)),
            const((width, d)), const((pd, d)), const((d, d)),
        ],
        out_specs=pl.BlockSpec((1, tm, d), lambda bi, i: (bi, i, 0)),
        out_shape=jax.ShapeDtypeStruct((b, s, d), F32),
        compiler_params=_cparams(2),
        name="fox_out",
    )(ot, zt, x, p, w_out.astype(BF16), w_up.astype(BF16), w_gate.astype(BF16))


PERM_ROWS = 256
DIL_SUB_BLOCKS = 8


def _phase_major_matrix(dilation):
    n = PERM_ROWS // dilation
    p = np.zeros((PERM_ROWS, PERM_ROWS), np.float32)
    for r in range(dilation):
        for i in range(n):
            p[r * n + i, dilation * i + r] = 1.0
    return p


def _dil_in_kernel(h_ref, g_ref, *rest, dilations, parts, tm):
    n_w = sum(parts)
    n_perm = sum(d > 1 for d in dilations)
    w_refs, perm_refs, out_refs = rest[:n_w], list(rest[n_w:n_w + n_perm]), rest[n_w + n_perm:]
    hn = _rms_norm(h_ref[0], g_ref[...]).astype(BF16)
    first = 0
    for dil, n_parts, o_ref in zip(dilations, parts, out_refs):
        group_w = w_refs[first:first + n_parts]
        first += n_parts
        pc = group_w[0].shape[1]
        if dil == 1:
            for part, w_ref in enumerate(group_w):
                o_ref[0, :, pc * part:pc * (part + 1)] = _dot(hn, w_ref[...]).astype(BF16)
            continue
        perm = perm_refs.pop(0)[...]
        n = PERM_ROWS // dil
        for c in range(tm // PERM_ROWS):
            rows = slice(PERM_ROWS * c, PERM_ROWS * (c + 1))
            hp = _dot(perm, hn[rows]).astype(BF16)
            for part, w_ref in enumerate(group_w):
                res = _dot(hp, w_ref[...]).astype(BF16)
                for r in range(dil):
                    dst = pc * (part * dil + r)
                    o_ref[0, n * c:n * (c + 1), dst:dst + pc] = res[n * r:n * (r + 1)]


def _dil_in(h, norm_g, w, groups, dilations, *, tm, part_cols):
    b, s, d = h.shape
    parts = tuple(len(blocks) for blocks in groups)
    perms = [jnp.asarray(_phase_major_matrix(dil), BF16) for dil in dilations if dil > 1]
    const = lambda shape: pl.BlockSpec(shape, lambda bi, i: (0,) * len(shape))
    w_specs = [pl.BlockSpec((d, part_cols), lambda bi, i, cb=cb: (0, cb))
               for blocks in groups for cb in blocks]
    widths = [dil * n_parts * part_cols for dil, n_parts in zip(dilations, parts)]
    return pl.pallas_call(
        functools.partial(_dil_in_kernel, dilations=dilations, parts=parts, tm=tm),
        grid=(b, s // tm),
        in_specs=[pl.BlockSpec((1, tm, d), lambda bi, i: (bi, i, 0)), const((1, d))]
                 + w_specs + [const(p.shape) for p in perms],
        out_specs=[pl.BlockSpec((1, tm // dil, width), lambda bi, i: (bi, i, 0))
                   for dil, width in zip(dilations, widths)],
        out_shape=[jax.ShapeDtypeStruct((b, s // dil, width), BF16)
                   for dil, width in zip(dilations, widths)],
        compiler_params=_cparams(2),
        name="dil_in_" + "_".join(str(dil) for dil in dilations),
    )(h, norm_g.reshape(1, d), *([w] * sum(parts)), *perms)


def _dil_attn_kernel(q_ref, kc_ref, vc_ref, kp_ref, vp_ref, o_ref, st_ref,
                     kcat, vcat, bias_scr, *, dilation, kb, phases, neg_slopes):
    n = pl.program_id(2)
    nw = DIL_WINDOW_STEPS
    hd = DIL_HEAD_DIM
    gw = DIL_HEADS_PER_GROUP * hd

    @pl.when((pl.program_id(0) == 0) & (pl.program_id(1) == 0) & (n == 0))
    def _():
        row = lax.broadcasted_iota(jnp.int32, (nw, 2 * nw), 0)
        col = lax.broadcasted_iota(jnp.int32, (nw, 2 * nw), 1)
        dist = nw + row - col
        band = (dist >= 0) & (dist <= nw)
        dist_f = (dist * dilation).astype(F32)
        for h in range(DIL_HEADS_PER_GROUP):
            bias_scr[h] = jnp.where(band, neg_slopes[h] * dist_f * LOG2E, MASK_VALUE)

    for ph in range(phases):
        cols = slice(gw * ph, gw * (ph + 1))
        kcat[ph, 0:nw] = kp_ref[0, :, cols]
        kcat[ph, nw:nw * (kb + 1)] = kc_ref[0, :, cols]
        vcat[ph, 0:nw] = vp_ref[0, :, cols]
        vcat[ph, nw:nw * (kb + 1)] = vc_ref[0, :, cols]
    lane = lax.broadcasted_iota(jnp.int32, (nw, LANES), 1)
    col = lax.broadcasted_iota(jnp.int32, (nw, 2 * nw), 1)
    no_history = jnp.where(col < jnp.where(n == 0, nw, 0), MASK_VALUE, 0.0)

    heads = [slice(hd * h, hd * (h + 1)) for h in range(DIL_HEADS_PER_GROUP)]
    problems = [(ph, jb) for ph in range(phases) for jb in range(kb)]
    scores = [[lax.dot_general(q_ref[0, nw * jb:nw * (jb + 1), gw * ph + hd * h:gw * ph + hd * (h + 1)],
                               kcat[ph, nw * jb:nw * (jb + 2), hs], NT_DIMS,
                               preferred_element_type=F32) for h, hs in enumerate(heads)]
              for ph, jb in problems]
    for i, (ph, jb) in enumerate(problems):
        rows = slice(nw * jb, nw * (jb + 1))
        stat_tile = jnp.zeros((nw, LANES), F32)
        for h, hs in enumerate(heads):
            t = scores[i][h] + bias_scr[h]
            if jb == 0:
                t = t + no_history
            m = jnp.max(t, axis=1, keepdims=True)
            p = jnp.exp2(t - m)
            l = jnp.sum(p, axis=1, keepdims=True)
            o_ref[0, rows, gw * ph + hd * h:gw * ph + hd * (h + 1)] = _dot(
                p.astype(BF16), vcat[ph, nw * jb:nw * (jb + 2), hs]).astype(BF16)
            stat_tile = jnp.where(lane == h, m, stat_tile)
            stat_tile = jnp.where(lane == h + DIL_HEADS_PER_GROUP, l, stat_tile)
        st_ref[0, rows, LANES * ph:LANES * (ph + 1)] = stat_tile


def _alibi_neg_slopes(group):
    n = len(DIL_PATTERN) * DIL_HEADS_PER_GROUP
    k = np.arange(1, n + 1, dtype=np.float32)
    slopes = np.float32(2.0) ** (np.float32(-ALIBI_MAX_EXP) * k / np.float32(n))
    lo = group * DIL_HEADS_PER_GROUP
    return tuple(float(-v) for v in slopes[lo:lo + DIL_HEADS_PER_GROUP])


def _dil_attn(view, group, *, kb, phases):
    _, dilation = DIL_PATTERN[group]
    b, length, _ = view.shape
    nw = DIL_WINDOW_STEPS
    gw = DIL_HEADS_PER_GROUP * DIL_HEAD_DIM
    rows = nw * kb
    steps = dilation // phases

    def cur(part):
        return pl.BlockSpec((1, rows, phases * gw), lambda bi, r, n: (bi, n, part * steps + r))

    def prev(part):
        return pl.BlockSpec((1, nw, phases * gw),
                            lambda bi, r, n: (bi, jnp.maximum(n * kb - 1, 0), part * steps + r))

    o, st = pl.pallas_call(
        functools.partial(_dil_attn_kernel, dilation=dilation, kb=kb, phases=phases,
                          neg_slopes=_alibi_neg_slopes(group)),
        grid=(b, steps, length // rows),
        in_specs=[cur(0), cur(1), cur(2), prev(1), prev(2)],
        out_specs=[
            pl.BlockSpec((1, rows, phases * gw), lambda bi, r, n: (bi, n, r)),
            pl.BlockSpec((1, rows, phases * LANES), lambda bi, r, n: (bi, n, r)),
        ],
        out_shape=[
            jax.ShapeDtypeStruct((b, length, dilation * gw), BF16),
            jax.ShapeDtypeStruct((b, length, dilation * LANES), F32),
        ],
        scratch_shapes=[pltpu.VMEM((phases, nw * (kb + 1), gw), BF16),
                        pltpu.VMEM((phases, nw * (kb + 1), gw), BF16),
                        pltpu.VMEM((DIL_HEADS_PER_GROUP, nw, 2 * nw), F32)],
        compiler_params=_cparams(3),
        name=f"dil_attn_g{group}",
    )(view, view, view, view, view)
    return o, st


def _natural_rows(blk, perm_t, dilation, tm):
    cols = blk.shape[1] // dilation
    if dilation == 1:
        return blk
    n = PERM_ROWS // dilation
    chunks = []
    for c in range(tm // PERM_ROWS):
        phase_major = jnp.concatenate(
            [blk[n * c:n * (c + 1), cols * r:cols * (r + 1)] for r in range(dilation)], axis=0)
        chunks.append(_dot(perm_t, phase_major))
    return jnp.concatenate(chunks, axis=0)


def _dil_out_kernel(o0_ref, o1_ref, o2_ref, s0_ref, s1_ref, s2_ref, z_ref, h_ref, p_ref,
                    pt1_ref, pt2_ref, e_ref, wout_ref, wup_ref, wgate_ref, g_ref,
                    out_ref, *, dilations, tm):
    perm_ts = (None, pt1_ref[...], pt2_ref[...])
    n_heads = DIL_HEADS_PER_GROUP
    row_max, row_sum = [], []
    for st_ref, perm_t, dil in zip((s0_ref, s1_ref, s2_ref), perm_ts, dilations):
        if dil == 1:
            st = st_ref[0]
        else:
            hi, mid, lo = (_natural_rows(piece.astype(BF16), perm_t, dil, tm)
                           for piece in _split3(st_ref[0]))
            st = hi + mid + lo
        row_max.append(st)
        row_sum.append(pltpu.roll(st, LANES - n_heads, 1))
    m = jnp.maximum(jnp.maximum(row_max[0], row_max[1]), row_max[2])
    e = [jnp.exp2(v - m) for v in row_max]
    inv = 1.0 / (e[0] * row_sum[0] + e[1] * row_sum[1] + e[2] * row_sum[2])
    live = lax.broadcasted_iota(jnp.int32, (tm, LANES), 1) < n_heads
    o = None
    for ev, o_ref, perm_t, dil in zip(e, (o0_ref, o1_ref, o2_ref), perm_ts, dilations):
        w = jnp.where(live, ev * inv, 0.0)
        w_hi = w.astype(BF16)
        w_lo = (w - w_hi.astype(F32)).astype(BF16)
        w_full = _dot(jnp.concatenate([w_hi, w_lo], axis=1), e_ref[...])
        term = w_full * _natural_rows(o_ref[0], perm_t, dil, tm)
        o = term if o is None else o + term
    z = z_ref[0].astype(F32)
    g = (o * (z * jax.nn.sigmoid(z))).astype(BF16)
    h1 = h_ref[0] + _dot(g, wout_ref[...])
    out_ref[0] = _rms_norm(_ple(h1, p_ref[0, 0], wup_ref, wgate_ref), g_ref[...])


def _dil_out(outs, stats, proj0, h, p, layer, w_out, w_up, w_gate, final_g, *, tm):
    b, s, d = h.shape
    gw = DIL_HEADS_PER_GROUP * DIL_HEAD_DIM
    pd = p.shape[-1]
    dilations = tuple(dil for _, dil in DIL_PATTERN)
    z_block = proj0.shape[-1] // gw - 1
    expand = np.zeros((2 * LANES, gw), np.float32)
    for hh in range(DIL_HEADS_PER_GROUP):
        expand[hh, DIL_HEAD_DIM * hh:DIL_HEAD_DIM * (hh + 1)] = 1.0
        expand[LANES + hh, DIL_HEAD_DIM * hh:DIL_HEAD_DIM * (hh + 1)] = 1.0
    perm_ts = [jnp.asarray(_phase_major_matrix(dil).T, BF16) for dil in dilations[1:]]
    const = lambda shape: pl.BlockSpec(shape, lambda bi, i: (0,) * len(shape))
    row = lambda w: pl.BlockSpec((1, tm, w), lambda bi, i: (bi, i, 0))
    view = lambda w: [pl.BlockSpec((1, tm // dil, dil * w), lambda bi, i: (bi, i, 0))
                      for dil in dilations]
    return pl.pallas_call(
        functools.partial(_dil_out_kernel, dilations=dilations, tm=tm),
        grid=(b, s // tm),
        in_specs=view(gw) + view(LANES) + [
            pl.BlockSpec((1, tm, gw), lambda bi, i: (bi, i, z_block)),
            row(d), pl.BlockSpec((1, 1, tm, pd), lambda bi, i: (layer, bi, i, 0)),
            const((PERM_ROWS, PERM_ROWS)), const((PERM_ROWS, PERM_ROWS)),
            const((2 * LANES, gw)), const((gw, d)), const((pd, d)), const((d, d)), const((1, d)),
        ],
        out_specs=row(d),
        out_shape=jax.ShapeDtypeStruct((b, s, d), F32),
        compiler_params=_cparams(2),
        name="dil_out",
    )(*outs, *stats, proj0, h, p, *perm_ts, jnp.asarray(expand, BF16),
      w_out.astype(BF16), w_up.astype(BF16), w_gate.astype(BF16), final_g.reshape(1, d))


def kernel(x, p, fox_norm, fox_w_in, fox_b_f, fox_w_out, dil_norm, dil_w_in, dil_w_out,
           ple_w_up, ple_w_gate, final_norm):
    s = x.shape[1]
    tm = min(512, s)
    tq = min(512, s)
    tk = min(512, s)
    assert tm == tq == tk, "the pruning table pairs fox_in tiles with query tiles and key blocks"
    kaug, qt, vt, zt, augq, kst, qst = _fox_in(x, fox_norm[0], fox_w_in[0], fox_b_f[0], tm=tm, ch=tk)
    pps = 2
    ot = _fox_attn(kaug, qt, vt, augq, _fox_prune_table(kst, qst, 2 * pps), tq=tq, tk=tk,
                   pairs_per_step=pps)
    h = _fox_out(ot, zt, x, p, 0, fox_w_out[0], ple_w_up[0], ple_w_gate[0], tm=tm)

    gw = DIL_HEADS_PER_GROUP * DIL_HEAD_DIM
    n_groups = len(DIL_PATTERN)
    k_scale = DIL_HEAD_DIM ** -0.5 * LOG2E
    col = jnp.arange(dil_w_in.shape[-1]) // (n_groups * gw)
    w1 = (dil_w_in[0] * jnp.where(col == 1, k_scale, 1.0)[None, :]).astype(BF16)
    qkv = lambda g: [n_groups * part + g for part in range(3)]
    (proj0,) = _dil_in(h, dil_norm[0], w1, [qkv(0) + [3 * n_groups]], (1,), tm=tm, part_cols=gw)
    views = [proj0] + list(_dil_in(h, dil_norm[0], w1, [qkv(1), qkv(2)],
                                   (DIL_PATTERN[1][1], DIL_PATTERN[2][1]), tm=min(PERM_ROWS, s),
                                   part_cols=gw))
    outs, stats = [], []
    for group, (_, dilation) in enumerate(DIL_PATTERN):
        kb = min(DIL_SUB_BLOCKS, s // dilation // DIL_WINDOW_STEPS)
        phases = min(dilation, DIL_SUB_BLOCKS // kb)
        o, st = _dil_attn(views[group], group, kb=kb, phases=phases)
        outs.append(o)
        stats.append(st)
    return _dil_out(outs, stats, proj0, h, p, 1, dil_w_out[0], ple_w_up[1], ple_w_gate[1],
                    final_norm, tm=tm)
```

```python
import functools

import numpy as np
import jax
import jax.numpy as jnp
from jax import lax
from jax.experimental import pallas as pl
from jax.experimental.pallas import tpu as pltpu

F32 = jnp.float32
BF16 = jnp.bfloat16

RMS_EPS = 1e-6
FOX_HEADS = 16
FOX_HEAD_DIM = 64
FOX_PAIRS = FOX_HEADS // 2
DIL_PATTERN = ((128, 1), (512, 4), (2048, 16))
DIL_HEADS_PER_GROUP = 8
DIL_HEAD_DIM = 128
DIL_WINDOW_STEPS = 128
ALIBI_MAX_EXP = 8.0
MASK_VALUE = -1e30
LOG2E = 1.4426950408889634

LANES = 128
AUG_SLOTS_PER_HEAD = 16
QUERY_LANES = 256
PRUNE_LOG2 = 50.0
NORM_SLACK = 1.02
VMEM_LIMIT_BYTES = 56 * 1024 * 1024

NT_DIMS = (((1,), (1,)), ((), ()))
TN_DIMS = (((0,), (0,)), ((), ()))


def _cparams(n_axes):
    return pltpu.CompilerParams(
        dimension_semantics=("arbitrary",) * n_axes,
        vmem_limit_bytes=VMEM_LIMIT_BYTES,
    )


def _rms_norm(x, g):
    ms = jnp.mean(x * x, axis=-1, keepdims=True)
    return x * lax.rsqrt(ms + RMS_EPS) * g


def _log_sigmoid(x):
    return jnp.minimum(x, 0.0) - jnp.log1p(jnp.exp(-jnp.abs(x)))


def _split3(x):
    hi = x.astype(BF16).astype(F32)
    r1 = x - hi
    mid = r1.astype(BF16).astype(F32)
    lo = (r1 - mid).astype(BF16).astype(F32)
    return hi, mid, lo


def _dot(a, b):
    return jnp.dot(a, b, preferred_element_type=F32)


def _fox_in_kernel(x_ref, g_ref, wk_ref, wt_ref, wft_ref, bt_ref,
                   triu_ref, pk_ref, pqt_ref, gk_ref, gq_ref,
                   kaug_ref, qt_ref, vt_ref, zt_ref, augq_ref, kst_ref, qst_ref,
                   carryt_ref, *, tm, ch):
    i = pl.program_id(1)

    @pl.when(i == 0)
    def _():
        carryt_ref[...] = jnp.zeros_like(carryt_ref)

    hn = _rms_norm(x_ref[0], g_ref[...]).astype(BF16)
    width = FOX_HEADS * FOX_HEAD_DIM

    ft = lax.dot_general(wft_ref[...], hn, NT_DIMS, preferred_element_type=F32)
    hi, mid, lo = _split3(_log_sigmoid(ft + bt_ref[...]))
    stack = jnp.concatenate([hi, mid, lo], axis=0).astype(BF16)
    cs = _dot(stack, triu_ref[...])
    ct = cs[0:16] + cs[16:32] + cs[32:48] + carryt_ref[:, 0:1]
    carryt_ref[...] = jnp.broadcast_to(ct[:, tm - 1:tm], carryt_ref.shape)
    c = jnp.concatenate([ct, jnp.zeros((LANES - FOX_HEADS, tm), F32)], axis=0).T

    k = _dot(hn, wk_ref[...])
    nhi, nmid, nlo = _split3(c * -LOG2E)
    lane = lax.broadcasted_iota(jnp.int32, (tm, LANES), 1)
    slots = jnp.where(lane < 16, nhi,
                      jnp.where(lane < 32, pltpu.roll(nmid, 16, 1),
                                jnp.where(lane < 48, pltpu.roll(nlo, 32, 1),
                                          jnp.where(lane == 48, 1.0, 0.0))))
    kaug = _dot(slots.astype(BF16), pk_ref[...])
    kb = k.astype(BF16)
    for p in range(FOX_PAIRS):
        kaug_ref[0, :, 2 * LANES * p:2 * LANES * p + LANES] = kb[:, LANES * p:LANES * (p + 1)]
        kaug_ref[0, :, 2 * LANES * p + LANES:2 * LANES * (p + 1)] = (
            kaug[:, LANES * p:LANES * (p + 1)].astype(BF16))
    kf = kb.astype(F32)
    kn2 = jnp.max(_dot((kf * kf).astype(BF16), gk_ref[...]), axis=0, keepdims=True)
    kst_ref[0, 0] = jnp.concatenate(
        [kn2, c[0:1, :], c[tm - 1:tm, :], jnp.zeros((5, LANES), F32)], axis=0)

    rows = 512
    for c0 in range(0, 3 * width, rows):
        res = lax.dot_general(wt_ref[c0:c0 + rows, :], hn, NT_DIMS,
                              preferred_element_type=F32).astype(BF16)
        which, off = divmod(c0, width)
        if which == 0:
            qt_ref[0, off:off + rows, :] = res
        elif which == 1:
            for j in range(tm // ch):
                vt_ref[0, j, off:off + rows, :] = res[:, j * ch:(j + 1) * ch]
        else:
            zt_ref[0, off:off + rows, :] = res

    hi, mid, lo = _split3(ct * LOG2E)
    ones = jnp.where(lax.broadcasted_iota(jnp.int32, (16, tm), 0) == 0, 1.0, 0.0)
    slots_t = jnp.concatenate([hi, mid, lo, ones], axis=0).astype(BF16)
    augq_ref[0] = _dot(pqt_ref[...], slots_t).astype(BF16)
    qf = qt_ref[0].astype(F32)
    qn2 = jnp.max(_dot(gq_ref[...], (qf * qf).astype(BF16)), axis=1, keepdims=True)
    qst_ref[0, 0] = jnp.broadcast_to(qn2, (FOX_HEADS, LANES))


def _placement_matrices():
    pk = np.zeros((LANES, FOX_PAIRS * LANES), np.float32)
    pqt = np.zeros((FOX_HEADS * AUG_SLOTS_PER_HEAD, 64), np.float32)
    for h in range(FOX_HEADS):
        p, odd = divmod(h, 2)
        base = 6 * odd
        for piece in range(3):
            pk[48, LANES * p + base + piece] = 1.0
            pk[16 * piece + h, LANES * p + base + 3 + piece] = 1.0
            pqt[AUG_SLOTS_PER_HEAD * h + base + piece, 16 * piece + h] = 1.0
            pqt[AUG_SLOTS_PER_HEAD * h + base + 3 + piece, 48] = 1.0
    return jnp.asarray(pk, BF16), jnp.asarray(pqt, BF16)


def _fox_in(x, norm_g, w_in, b_f, *, tm, ch):
    b, s, d = x.shape
    width = FOX_HEADS * FOX_HEAD_DIM
    wq = w_in[:, :width] * (FOX_HEAD_DIM ** -0.5 * LOG2E)
    wk = w_in[:, width:2 * width]
    wv = w_in[:, 2 * width:3 * width]
    wz = w_in[:, 3 * width:4 * width]
    wf = w_in[:, 4 * width:]
    wt = jnp.concatenate([wq, wv, wz], axis=1).astype(BF16).T
    wft = wf.T.astype(BF16)
    bt = b_f.reshape(FOX_HEADS, 1)
    r = np.arange(tm)
    triu = jnp.asarray(r[:, None] <= r[None, :], BF16)
    pk, pqt = _placement_matrices()
    head_of = np.arange(width) // FOX_HEAD_DIM
    gk = jnp.asarray(head_of[:, None] == np.arange(LANES)[None, :], BF16)
    gq = jnp.asarray(np.arange(FOX_HEADS)[:, None] == head_of[None, :], BF16)

    const = lambda shape: pl.BlockSpec(shape, lambda bi, i: (0,) * len(shape))
    return pl.pallas_call(
        functools.partial(_fox_in_kernel, tm=tm, ch=ch),
        grid=(b, s // tm),
        in_specs=[
            pl.BlockSpec((1, tm, d), lambda bi, i: (bi, i, 0)),
            const((1, d)),
            const(wk.shape), const(wt.shape), const(wft.shape),
            const((FOX_HEADS, 1)),
            const((tm, tm)),
            const(pk.shape), const(pqt.shape), const(gk.shape), const(gq.shape),
        ],
        out_specs=[
            pl.BlockSpec((1, tm, 2 * width), lambda bi, i: (bi, i, 0)),
            pl.BlockSpec((1, width, tm), lambda bi, i: (bi, 0, i)),
            pl.BlockSpec((1, tm // ch, width, ch), lambda bi, i: (bi, i, 0, 0)),
            pl.BlockSpec((1, width, tm), lambda bi, i: (bi, 0, i)),
            pl.BlockSpec((1, FOX_HEADS * AUG_SLOTS_PER_HEAD, tm), lambda bi, i: (bi, 0, i)),
            pl.BlockSpec((1, 1, 8, LANES), lambda bi, i: (bi, i, 0, 0)),
            pl.BlockSpec((1, 1, FOX_HEADS, LANES), lambda bi, i: (bi, i, 0, 0)),
        ],
        out_shape=[
            jax.ShapeDtypeStruct((b, s, 2 * width), BF16),
            jax.ShapeDtypeStruct((b, width, s), BF16),
            jax.ShapeDtypeStruct((b, s // ch, width, ch), BF16),
            jax.ShapeDtypeStruct((b, width, s), BF16),
            jax.ShapeDtypeStruct((b, FOX_HEADS * AUG_SLOTS_PER_HEAD, s), BF16),
            jax.ShapeDtypeStruct((b, s // tm, 8, LANES), F32),
            jax.ShapeDtypeStruct((b, s // tm, FOX_HEADS, LANES), F32),
        ],
        scratch_shapes=[pltpu.VMEM((FOX_HEADS, LANES), F32)],
        compiler_params=_cparams(2),
        name="fox_in",
    )(x, norm_g.reshape(1, d), wk.astype(BF16), wt, wft, bt, triu, pk, pqt, gk, gq)


def _fox_attn_kernel(kstart_ref, kaug_ref, vt_ref, qt_ref, augq_ref, ot_ref,
                     s_scr, p_scr, *, tq, tk, pairs_per_step):
    qi = pl.program_id(2)
    hd = FOX_HEAD_DIM
    zeros_hd = jnp.zeros((hd, tq), BF16)
    zeros_tail = jnp.zeros((LANES - AUG_SLOTS_PER_HEAD, tq), BF16)
    ones_block = lambda n: jnp.where(lax.broadcasted_iota(jnp.int32, (16, n), 0) == 0, 1.0, 0.0).astype(BF16)
    ones_rows = ones_block(tk)
    n_blocks = (qi * tq) // tk + 1
    k_first = kstart_ref[pl.program_id(0), pl.program_id(1), qi]

    heads = [(u, h) for u in range(pairs_per_step) for h in range(2)]
    w_head = {}
    for u, h in heads:
        q_rows = qt_ref[0, LANES * u + hd * h:LANES * u + hd * (h + 1), :]
        aug_rows = augq_ref[0, 2 * AUG_SLOTS_PER_HEAD * u + AUG_SLOTS_PER_HEAD * h:
                            2 * AUG_SLOTS_PER_HEAD * u + AUG_SLOTS_PER_HEAD * (h + 1), :]
        q_part = [q_rows, zeros_hd] if h == 0 else [zeros_hd, q_rows]
        w_head[u, h] = jnp.concatenate(q_part + [aug_rows, zeros_tail], axis=0)
    chains = [(u, h, sub) for u, h in heads for sub in range(tq // QUERY_LANES)]

    def issue_scores(kj, slot):
        ks = pl.multiple_of(kj * tk, tk)
        block_max = []
        for c, (u, h, sub) in enumerate(chains):
            kb = kaug_ref[0, pl.ds(ks, tk), 2 * LANES * u:2 * LANES * (u + 1)]
            s = _dot(kb, w_head[u, h][:, QUERY_LANES * sub:QUERY_LANES * (sub + 1)])
            s_scr[slot, c] = s
            block_max.append(jnp.max(s, axis=0, keepdims=True))
        return tuple(block_max)

    tri_key = lax.broadcasted_iota(jnp.int32, (QUERY_LANES, QUERY_LANES), 0)
    tri_qry = lax.broadcasted_iota(jnp.int32, (QUERY_LANES, QUERY_LANES), 1)
    causal = jnp.where(tri_key <= tri_qry, 0.0, MASK_VALUE)

    def pv_block(kj, c):
        u, h, _ = chains[c]
        v_rows = vt_ref[0, kj, LANES * u + hd * h:LANES * u + hd * (h + 1), :]
        return _dot(jnp.concatenate([v_rows, ones_rows], axis=0), p_scr[c])

    def step(kj, slot, state, has_prev=True):
        carry, block_max = state
        next_max = issue_scores(kj + 1, 1 - slot)
        out = []
        for c in range(len(chains)):
            m_old, alpha_prev, acc = carry[c]
            if has_prev:
                acc = acc * alpha_prev + pv_block(kj - 1, c)
            m_new = jnp.maximum(m_old, block_max[c])
            p_scr[c] = jnp.exp2(s_scr[slot, c] - m_new).astype(BF16)
            out.append((m_new, jnp.exp2(m_old - m_new), acc))
        return tuple(out), next_max

    def diagonal(kj, slot, state):
        carry, _ = state
        for c, (u, h, sub) in enumerate(chains):
            m_old, alpha_prev, acc = carry[c]
            acc = acc * alpha_prev + pv_block(jnp.maximum(kj - 1, 0), c)
            lo = QUERY_LANES * sub
            vis = lo + QUERY_LANES
            mid = s_scr[slot, c, lo:vis, :] + causal
            m_new = jnp.maximum(m_old, jnp.max(mid, axis=0, keepdims=True))
            if lo:
                top = s_scr[slot, c, 0:lo, :]
                m_new = jnp.maximum(m_new, jnp.max(top, axis=0, keepdims=True))
                p = jnp.concatenate([jnp.exp2(top - m_new).astype(BF16),
                                     jnp.exp2(mid - m_new).astype(BF16)], axis=0)
            else:
                p = jnp.exp2(mid - m_new).astype(BF16)
            v_rows = vt_ref[0, kj, LANES * u + hd * h:LANES * u + hd * (h + 1), 0:vis]
            pv = _dot(jnp.concatenate([v_rows, ones_block(vis)], axis=0), p)
            acc = acc * jnp.exp2(m_old - m_new) + pv
            ot_ref[0, LANES * u + hd * h:LANES * u + hd * (h + 1),
                   QUERY_LANES * sub:QUERY_LANES * (sub + 1)] = (acc[0:hd] / acc[hd:hd + 1]).astype(BF16)

    row = lambda value: jnp.full((1, QUERY_LANES), value, F32)
    carry = tuple((row(MASK_VALUE), row(1.0), jnp.zeros((hd + 16, QUERY_LANES), F32)) for _ in chains)
    state = (carry, tuple(row(0.0) for _ in chains))
    n_unmasked = n_blocks - 1 - k_first
    has_unmasked = jnp.minimum(n_unmasked, 1)

    def first(_, st):
        return step(k_first, 0, (st[0], issue_scores(k_first, 0)), has_prev=False)

    def only_diagonal(_, st):
        p_scr[...] = jnp.zeros_like(p_scr)
        return st[0], issue_scores(k_first, 0)

    def pair(i, st):
        kj = k_first + 1 + 2 * i
        return step(kj + 1, 0, step(kj, 1, st))

    state = lax.fori_loop(0, has_unmasked, first, state)
    state = lax.fori_loop(0, 1 - has_unmasked, only_diagonal, state)
    n_rest = jnp.maximum(n_unmasked - 1, 0)
    state = lax.fori_loop(0, n_rest // 2, pair, state)
    state = lax.fori_loop(0, lax.rem(n_rest, 2), lambda _, st: step(n_blocks - 2, 1, st), state)
    diagonal(n_blocks - 1, lax.rem(n_unmasked, 2), state)


def _fox_prune_table(kst, qst, heads_per_step):
    kn = jnp.sqrt(kst[:, :, 0, :FOX_HEADS]) * NORM_SLACK
    qn = jnp.sqrt(qst[:, :, :, 0]) * NORM_SLACK
    c_first = kst[:, :, 1, :FOX_HEADS] * LOG2E
    c_last = kst[:, :, 2, :FOX_HEADS] * LOG2E
    bound = (qn[:, :, None, :] * kn[:, None, :, :] + c_first[:, :, None, :] - c_last[:, None, :, :]
             + (qn * kn)[:, :, None, :])
    n = kst.shape[1]
    earlier = jnp.arange(n)[None, :, None] < jnp.arange(n)[:, None, None]
    skip = (bound < -PRUNE_LOG2) & earlier[None]
    block = jnp.arange(n, dtype=jnp.int32)[None, None, :, None]
    lead = jnp.min(jnp.where(skip, n, block), axis=2)
    lead = jnp.min(lead.reshape(lead.shape[0], n, FOX_HEADS // heads_per_step, heads_per_step), axis=3)
    return jnp.transpose(lead, (0, 2, 1)).astype(jnp.int32)


def _fox_attn(kaug, qt, vt, augq, kstart, *, tq, tk, pairs_per_step):
    b, s, _ = kaug.shape
    width = FOX_HEADS * FOX_HEAD_DIM
    nk = s // tk
    pps = pairs_per_step
    n_chains = 2 * pps * (tq // QUERY_LANES)
    grid_spec = pltpu.PrefetchScalarGridSpec(
        num_scalar_prefetch=1,
        grid=(b, FOX_PAIRS // pps, s // tq),
        in_specs=[
            pl.BlockSpec((1, s, 2 * LANES * pps), lambda bi, p, qi, ks: (bi, 0, p)),
            pl.BlockSpec((1, nk, LANES * pps, tk), lambda bi, p, qi, ks: (bi, 0, p, 0)),
            pl.BlockSpec((1, LANES * pps, tq), lambda bi, p, qi, ks: (bi, p, qi)),
            pl.BlockSpec((1, 2 * AUG_SLOTS_PER_HEAD * pps, tq), lambda bi, p, qi, ks: (bi, p, qi)),
        ],
        out_specs=pl.BlockSpec((1, LANES * pps, tq), lambda bi, p, qi, ks: (bi, p, qi)),
        scratch_shapes=[pltpu.VMEM((2, n_chains, tk, QUERY_LANES), F32),
                        pltpu.VMEM((n_chains, tk, QUERY_LANES), BF16)],
    )
    return pl.pallas_call(
        functools.partial(_fox_attn_kernel, tq=tq, tk=tk, pairs_per_step=pps),
        grid_spec=grid_spec,
        out_shape=jax.ShapeDtypeStruct((b, width, s), BF16),
        compiler_params=_cparams(3),
        name="fox_attn",
    )(kstart, kaug, vt, qt, augq)


def _ple(h1, p, wup_ref, wgate_ref):
    gate = jax.nn.sigmoid(_dot(h1.astype(BF16), wgate_ref[...]))
    up = _dot(p.astype(BF16), wup_ref[...])
    return h1 + up * gate


def _fox_out_kernel(ot_ref, zt_ref, x_ref, p_ref, wout_ref, wup_ref, wgate_ref, h_ref):
    z = zt_ref[0].astype(F32)
    gt = (ot_ref[0].astype(F32) * (z * jax.nn.sigmoid(z))).astype(BF16)
    y = lax.dot_general(gt, wout_ref[...], TN_DIMS, preferred_element_type=F32)
    h_ref[0] = _ple(x_ref[0] + y, p_ref[0, 0], wup_ref, wgate_ref)


def _fox_out(ot, zt, x, p, layer, w_out, w_up, w_gate, *, tm):
    b, s, d = x.shape
    width = ot.shape[1]
    pd = p.shape[-1]
    const = lambda shape: pl.BlockSpec(shape, lambda bi, i: (0,) * len(shape))
    return pl.pallas_call(
        _fox_out_kernel,
        grid=(b, s // tm),
        in_specs=[
            pl.BlockSpec((1, width, tm), lambda bi, i: (bi, 0, i)),
            pl.BlockSpec((1, width, tm), lambda bi, i: (bi, 0, i)),
            pl.BlockSpec((1, tm, d), lambda bi, i: (bi, i, 0)),
            pl.BlockSpec((1, 1, tm, pd), lambda bi, i: (layer, bi, i, 0)),
            const((width, d)), const((pd, d)), const((d, d)),
        ],
        out_specs=pl.BlockSpec((1, tm, d), lambda bi, i: (bi, i, 0)),
        out_shape=jax.ShapeDtypeStruct((b, s, d), F32),
        compiler_params=_cparams(2),
        name="fox_out",
    )(ot, zt, x, p, w_out.astype(BF16), w_up.astype(BF16), w_gate.astype(BF16))


PERM_ROWS = 256
DIL_SUB_BLOCKS = 8


def _phase_major_matrix(dilation):
    n = PERM_ROWS // dilation
    p = np.zeros((PERM_ROWS, PERM_ROWS), np.float32)
    for r in range(dilation):
        for i in range(n):
            p[r * n + i, dilation * i + r] = 1.0
    return p


def _dil_in_kernel(h_ref, g_ref, *rest, dilations, parts, tm):
    n_w = sum(parts)
    n_perm = sum(d > 1 for d in dilations)
    w_refs, perm_refs, out_refs = rest[:n_w], list(rest[n_w:n_w + n_perm]), rest[n_w + n_perm:]
    hn = _rms_norm(h_ref[0], g_ref[...]).astype(BF16)
    first = 0
    for dil, n_parts, o_ref in zip(dilations, parts, out_refs):
        group_w = w_refs[first:first + n_parts]
        first += n_parts
        pc = group_w[0].shape[1]
        if dil == 1:
            for part, w_ref in enumerate(group_w):
                o_ref[0, :, pc * part:pc * (part + 1)] = _dot(hn, w_ref[...]).astype(BF16)
            continue
        perm = perm_refs.pop(0)[...]
        n = PERM_ROWS // dil
        for c in range(tm // PERM_ROWS):
            rows = slice(PERM_ROWS * c, PERM_ROWS * (c + 1))
            hp = _dot(perm, hn[rows]).astype(BF16)
            for part, w_ref in enumerate(group_w):
                res = _dot(hp, w_ref[...]).astype(BF16)
                for r in range(dil):
                    dst = pc * (part * dil + r)
                    o_ref[0, n * c:n * (c + 1), dst:dst + pc] = res[n * r:n * (r + 1)]


def _dil_in(h, norm_g, w, groups, dilations, *, tm, part_cols):
    b, s, d = h.shape
    parts = tuple(len(blocks) for blocks in groups)
    perms = [jnp.asarray(_phase_major_matrix(dil), BF16) for dil in dilations if dil > 1]
    const = lambda shape: pl.BlockSpec(shape, lambda bi, i: (0,) * len(shape))
    w_specs = [pl.BlockSpec((d, part_cols), lambda bi, i, cb=cb: (0, cb))
               for blocks in groups for cb in blocks]
    widths = [dil * n_parts * part_cols for dil, n_parts in zip(dilations, parts)]
    return pl.pallas_call(
        functools.partial(_dil_in_kernel, dilations=dilations, parts=parts, tm=tm),
        grid=(b, s // tm),
        in_specs=[pl.BlockSpec((1, tm, d), lambda bi, i: (bi, i, 0)), const((1, d))]
                 + w_specs + [const(p.shape) for p in perms],
        out_specs=[pl.BlockSpec((1, tm // dil, width), lambda bi, i: (bi, i, 0))
                   for dil, width in zip(dilations, widths)],
        out_shape=[jax.ShapeDtypeStruct((b, s // dil, width), BF16)
                   for dil, width in zip(dilations, widths)],
        compiler_params=_cparams(2),
        name="dil_in_" + "_".join(str(dil) for dil in dilations),
    )(h, norm_g.reshape(1, d), *([w] * sum(parts)), *perms)


def _dil_attn_kernel(q_ref, kc_ref, vc_ref, kp_ref, vp_ref, o_ref, st_ref,
                     kcat, vcat, bias_scr, *, dilation, kb, phases, neg_slopes):
    n = pl.program_id(2)
    nw = DIL_WINDOW_STEPS
    hd = DIL_HEAD_DIM
    gw = DIL_HEADS_PER_GROUP * hd

    @pl.when((pl.program_id(0) == 0) & (pl.program_id(1) == 0) & (n == 0))
    def _():
        row = lax.broadcasted_iota(jnp.int32, (nw, 2 * nw), 0)
        col = lax.broadcasted_iota(jnp.int32, (nw, 2 * nw), 1)
        dist = nw + row - col
        band = (dist >= 0) & (dist <= nw)
        dist_f = (dist * dilation).astype(F32)
        for h in range(DIL_HEADS_PER_GROUP):
            bias_scr[h] = jnp.where(band, neg_slopes[h] * dist_f * LOG2E, MASK_VALUE)

    for ph in range(phases):
        cols = slice(gw * ph, gw * (ph + 1))
        kcat[ph, 0:nw] = kp_ref[0, :, cols]
        kcat[ph, nw:nw * (kb + 1)] = kc_ref[0, :, cols]
        vcat[ph, 0:nw] = vp_ref[0, :, cols]
        vcat[ph, nw:nw * (kb + 1)] = vc_ref[0, :, cols]
    lane = lax.broadcasted_iota(jnp.int32, (nw, LANES), 1)
    col = lax.broadcasted_iota(jnp.int32, (nw, 2 * nw), 1)
    no_history = jnp.where(col < jnp.where(n == 0, nw, 0), MASK_VALUE, 0.0)

    heads = [slice(hd * h, hd * (h + 1)) for h in range(DIL_HEADS_PER_GROUP)]
    problems = [(ph, jb) for ph in range(phases) for jb in range(kb)]
    scores = [[lax.dot_general(q_ref[0, nw * jb:nw * (jb + 1), gw * ph + hd * h:gw * ph + hd * (h + 1)],
                               kcat[ph, nw * jb:nw * (jb + 2), hs], NT_DIMS,
                               preferred_element_type=F32) for h, hs in enumerate(heads)]
              for ph, jb in problems]
    for i, (ph, jb) in enumerate(problems):
        rows = slice(nw * jb, nw * (jb + 1))
        stat_tile = jnp.zeros((nw, LANES), F32)
        for h, hs in enumerate(heads):
            t = scores[i][h] + bias_scr[h]
            if jb == 0:
                t = t + no_history
            m = jnp.max(t, axis=1, keepdims=True)
            p = jnp.exp2(t - m)
            l = jnp.sum(p, axis=1, keepdims=True)
            o_ref[0, rows, gw * ph + hd * h:gw * ph + hd * (h + 1)] = _dot(
                p.astype(BF16), vcat[ph, nw * jb:nw * (jb + 2), hs]).astype(BF16)
            stat_tile = jnp.where(lane == h, m, stat_tile)
            stat_tile = jnp.where(lane == h + DIL_HEADS_PER_GROUP, l, stat_tile)
        st_ref[0, rows, LANES * ph:LANES * (ph + 1)] = stat_tile


def _alibi_neg_slopes(group):
    n = len(DIL_PATTERN) * DIL_HEADS_PER_GROUP
    k = np.arange(1, n + 1, dtype=np.float32)
    slopes = np.float32(2.0) ** (np.float32(-ALIBI_MAX_EXP) * k / np.float32(n))
    lo = group * DIL_HEADS_PER_GROUP
    return tuple(float(-v) for v in slopes[lo:lo + DIL_HEADS_PER_GROUP])


def _dil_attn(view, group, *, kb, phases):
    _, dilation = DIL_PATTERN[group]
    b, length, _ = view.shape
    nw = DIL_WINDOW_STEPS
    gw = DIL_HEADS_PER_GROUP * DIL_HEAD_DIM
    rows = nw * kb
    steps = dilation // phases

    def cur(part):
        return pl.BlockSpec((1, rows, phases * gw), lambda bi, r, n: (bi, n, part * steps + r))

    def prev(part):
        return pl.BlockSpec((1, nw, phases * gw),
                            lambda bi, r, n: (bi, jnp.maximum(n * kb - 1, 0), part * steps + r))

    o, st = pl.pallas_call(
        functools.partial(_dil_attn_kernel, dilation=dilation, kb=kb, phases=phases,
                          neg_slopes=_alibi_neg_slopes(group)),
        grid=(b, steps, length // rows),
        in_specs=[cur(0), cur(1), cur(2), prev(1), prev(2)],
        out_specs=[
            pl.BlockSpec((1, rows, phases * gw), lambda bi, r, n: (bi, n, r)),
            pl.BlockSpec((1, rows, phases * LANES), lambda bi, r, n: (bi, n, r)),
        ],
        out_shape=[
            jax.ShapeDtypeStruct((b, length, dilation * gw), BF16),
            jax.ShapeDtypeStruct((b, length, dilation * LANES), F32),
        ],
        scratch_shapes=[pltpu.VMEM((phases, nw * (kb + 1), gw), BF16),
                        pltpu.VMEM((phases, nw * (kb + 1), gw), BF16),
                        pltpu.VMEM((DIL_HEADS_PER_GROUP, nw, 2 * nw), F32)],
        compiler_params=_cparams(3),
        name=f"dil_attn_g{group}",
    )(view, view, view, view, view)
    return o, st


def _natural_rows(blk, perm_t, dilation, tm):
    cols = blk.shape[1] // dilation
    if dilation == 1:
        return blk
    n = PERM_ROWS // dilation
    chunks = []
    for c in range(tm // PERM_ROWS):
        phase_major = jnp.concatenate(
            [blk[n * c:n * (c + 1), cols * r:cols * (r + 1)] for r in range(dilation)], axis=0)
        chunks.append(_dot(perm_t, phase_major))
    return jnp.concatenate(chunks, axis=0)


def _dil_out_kernel(o0_ref, o1_ref, o2_ref, s0_ref, s1_ref, s2_ref, z_ref, h_ref, p_ref,
                    pt1_ref, pt2_ref, e_ref, wout_ref, wup_ref, wgate_ref, g_ref,
                    out_ref, *, dilations, tm):
    perm_ts = (None, pt1_ref[...], pt2_ref[...])
    n_heads = DIL_HEADS_PER_GROUP
    row_max, row_sum = [], []
    for st_ref, perm_t, dil in zip((s0_ref, s1_ref, s2_ref), perm_ts, dilations):
        if dil == 1:
            st = st_ref[0]
        else:
            hi, mid, lo = (_natural_rows(piece.astype(BF16), perm_t, dil, tm)
                           for piece in _split3(st_ref[0]))
            st = hi + mid + lo
        row_max.append(st)
        row_sum.append(pltpu.roll(st, LANES - n_heads, 1))
    m = jnp.maximum(jnp.maximum(row_max[0], row_max[1]), row_max[2])
    e = [jnp.exp2(v - m) for v in row_max]
    inv = 1.0 / (e[0] * row_sum[0] + e[1] * row_sum[1] + e[2] * row_sum[2])
    live = lax.broadcasted_iota(jnp.int32, (tm, LANES), 1) < n_heads
    o = None
    for ev, o_ref, perm_t, dil in zip(e, (o0_ref, o1_ref, o2_ref), perm_ts, dilations):
        w = jnp.where(live, ev * inv, 0.0)
        w_hi = w.astype(BF16)
        w_lo = (w - w_hi.astype(F32)).astype(BF16)
        w_full = _dot(jnp.concatenate([w_hi, w_lo], axis=1), e_ref[...])
        term = w_full * _natural_rows(o_ref[0], perm_t, dil, tm)
        o = term if o is None else o + term
    z = z_ref[0].astype(F32)
    g = (o * (z * jax.nn.sigmoid(z))).astype(BF16)
    h1 = h_ref[0] + _dot(g, wout_ref[...])
    out_ref[0] = _rms_norm(_ple(h1, p_ref[0, 0], wup_ref, wgate_ref), g_ref[...])


def _dil_out(outs, stats, proj0, h, p, layer, w_out, w_up, w_gate, final_g, *, tm):
    b, s, d = h.shape
    gw = DIL_HEADS_PER_GROUP * DIL_HEAD_DIM
    pd = p.shape[-1]
    dilations = tuple(dil for _, dil in DIL_PATTERN)
    z_block = proj0.shape[-1] // gw - 1
    expand = np.zeros((2 * LANES, gw), np.float32)
    for hh in range(DIL_HEADS_PER_GROUP):
        expand[hh, DIL_HEAD_DIM * hh:DIL_HEAD_DIM * (hh + 1)] = 1.0
        expand[LANES + hh, DIL_HEAD_DIM * hh:DIL_HEAD_DIM * (hh + 1)] = 1.0
    perm_ts = [jnp.asarray(_phase_major_matrix(dil).T, BF16) for dil in dilations[1:]]
    const = lambda shape: pl.BlockSpec(shape, lambda bi, i: (0,) * len(shape))
    row = lambda w: pl.BlockSpec((1, tm, w), lambda bi, i: (bi, i, 0))
    view = lambda w: [pl.BlockSpec((1, tm // dil, dil * w), lambda bi, i: (bi, i, 0))
                      for dil in dilations]
    return pl.pallas_call(
        functools.partial(_dil_out_kernel, dilations=dilations, tm=tm),
        grid=(b, s // tm),
        in_specs=view(gw) + view(LANES) + [
            pl.BlockSpec((1, tm, gw), lambda bi, i: (bi, i, z_block)),
            row(d), pl.BlockSpec((1, 1, tm, pd), lambda bi, i: (layer, bi, i, 0)),
            const((PERM_ROWS, PERM_ROWS)), const((PERM_ROWS, PERM_ROWS)),
            const((2 * LANES, gw)), const((gw, d)), const((pd, d)), const((d, d)), const((1, d)),
        ],
        out_specs=row(d),
        out_shape=jax.ShapeDtypeStruct((b, s, d), F32),
        compiler_params=_cparams(2),
        name="dil_out",
    )(*outs, *stats, proj0, h, p, *perm_ts, jnp.asarray(expand, BF16),
      w_out.astype(BF16), w_up.astype(BF16), w_gate.astype(BF16), final_g.reshape(1, d))


def kernel(x, p, fox_norm, fox_w_in, fox_b_f, fox_w_out, dil_norm, dil_w_in, dil_w_out,
           ple_w_up, ple_w_gate, final_norm):
    s = x.shape[1]
    tm = min(512, s)
    tq = min(512, s)
    tk = min(512, s)
    assert tm == tq == tk, "the pruning table pairs fox_in tiles with query tiles and key blocks"
    kaug, qt, vt, zt, augq, kst, qst = _fox_in(x, fox_norm[0], fox_w_in[0], fox_b_f[0], tm=tm, ch=tk)
    pps = 2
    ot = _fox_attn(kaug, qt, vt, augq, _fox_prune_table(kst, qst, 2 * pps), tq=tq, tk=tk,
                   pairs_per_step=pps)
    h = _fox_out(ot, zt, x, p, 0, fox_w_out[0], ple_w_up[0], ple_w_gate[0], tm=tm)

    gw = DIL_HEADS_PER_GROUP * DIL_HEAD_DIM
    n_groups = len(DIL_PATTERN)
    k_scale = DIL_HEAD_DIM ** -0.5 * LOG2E
    col = jnp.arange(dil_w_in.shape[-1]) // (n_groups * gw)
    w1 = (dil_w_in[0] * jnp.where(col == 1, k_scale, 1.0)[None, :]).astype(BF16)
    qkv = lambda g: [n_groups * part + g for part in range(3)]
    (proj0,) = _dil_in(h, dil_norm[0], w1, [qkv(0) + [3 * n_groups]], (1,), tm=min(2 * tm, s),
                       part_cols=gw)
    views = [proj0] + list(_dil_in(h, dil_norm[0], w1, [qkv(1), qkv(2)],
                                   (DIL_PATTERN[1][1], DIL_PATTERN[2][1]), tm=tm,
                                   part_cols=gw))
    outs, stats = [], []
    for group, (_, dilation) in enumerate(DIL_PATTERN):
        kb = min(DIL_SUB_BLOCKS, s // dilation // DIL_WINDOW_STEPS)
        phases = min(dilation, DIL_SUB_BLOCKS // kb)
        o, st = _dil_attn(views[group], group, kb=kb, phases=phases)
        outs.append(o)
        stats.append(st)
    return _dil_out(outs, stats, proj0, h, p, 1, dil_w_out[0], ple_w_up[1], ple_w_gate[1],
                    final_norm, tm=tm)
```

```python
import functools

import numpy as np
import jax
import jax.numpy as jnp
from jax import lax
from jax.experimental import pallas as pl
from jax.experimental.pallas import tpu as pltpu

F32 = jnp.float32
BF16 = jnp.bfloat16

RMS_EPS = 1e-6
FOX_HEADS = 16
FOX_HEAD_DIM = 64
FOX_PAIRS = FOX_HEADS // 2
DIL_PATTERN = ((128, 1), (512, 4), (2048, 16))
DIL_HEADS_PER_GROUP = 8
DIL_HEAD_DIM = 128
DIL_WINDOW_STEPS = 128
ALIBI_MAX_EXP = 8.0
MASK_VALUE = -1e30
LOG2E = 1.4426950408889634

LANES = 128
AUG_SLOTS_PER_HEAD = 16
QUERY_LANES = 256
PRUNE_LOG2 = 50.0
NORM_SLACK = 1.02
VMEM_LIMIT_BYTES = 56 * 1024 * 1024

NT_DIMS = (((1,), (1,)), ((), ()))
TN_DIMS = (((0,), (0,)), ((), ()))


def _cparams(n_axes):
    return pltpu.CompilerParams(
        dimension_semantics=("arbitrary",) * n_axes,
        vmem_limit_bytes=VMEM_LIMIT_BYTES,
    )


def _rms_norm(x, g):
    ms = jnp.mean(x * x, axis=-1, keepdims=True)
    return x * lax.rsqrt(ms + RMS_EPS) * g


def _log_sigmoid(x):
    return jnp.minimum(x, 0.0) - jnp.log1p(jnp.exp(-jnp.abs(x)))


def _split3(x):
    hi = x.astype(BF16).astype(F32)
    r1 = x - hi
    mid = r1.astype(BF16).astype(F32)
    lo = (r1 - mid).astype(BF16).astype(F32)
    return hi, mid, lo


def _dot(a, b):
    return jnp.dot(a, b, preferred_element_type=F32)


def _fox_in_kernel(x_ref, g_ref, wk_ref, wt_ref, wft_ref, bt_ref,
                   triu_ref, pk_ref, pqt_ref, gk_ref, gq_ref,
                   kaug_ref, qt_ref, vt_ref, zt_ref, augq_ref, kst_ref, qst_ref,
                   carryt_ref, *, tm, ch):
    i = pl.program_id(1)

    @pl.when(i == 0)
    def _():
        carryt_ref[...] = jnp.zeros_like(carryt_ref)

    hn = _rms_norm(x_ref[0], g_ref[...]).astype(BF16)
    width = FOX_HEADS * FOX_HEAD_DIM

    ft = lax.dot_general(wft_ref[...], hn, NT_DIMS, preferred_element_type=F32)
    hi, mid, lo = _split3(_log_sigmoid(ft + bt_ref[...]))
    stack = jnp.concatenate([hi, mid, lo], axis=0).astype(BF16)
    cs = _dot(stack, triu_ref[...])
    ct = cs[0:16] + cs[16:32] + cs[32:48] + carryt_ref[:, 0:1]
    carryt_ref[...] = jnp.broadcast_to(ct[:, tm - 1:tm], carryt_ref.shape)
    c = jnp.concatenate([ct, jnp.zeros((LANES - FOX_HEADS, tm), F32)], axis=0).T

    k = _dot(hn, wk_ref[...])
    nhi, nmid, nlo = _split3(c * -LOG2E)
    lane = lax.broadcasted_iota(jnp.int32, (tm, LANES), 1)
    slots = jnp.where(lane < 16, nhi,
                      jnp.where(lane < 32, pltpu.roll(nmid, 16, 1),
                                jnp.where(lane < 48, pltpu.roll(nlo, 32, 1),
                                          jnp.where(lane == 48, 1.0, 0.0))))
    kaug = _dot(slots.astype(BF16), pk_ref[...])
    kb = k.astype(BF16)
    for p in range(FOX_PAIRS):
        kaug_ref[0, :, 2 * LANES * p:2 * LANES * p + LANES] = kb[:, LANES * p:LANES * (p + 1)]
        kaug_ref[0, :, 2 * LANES * p + LANES:2 * LANES * (p + 1)] = (
            kaug[:, LANES * p:LANES * (p + 1)].astype(BF16))
    kf = kb.astype(F32)
    kn2 = jnp.max(_dot((kf * kf).astype(BF16), gk_ref[...]), axis=0, keepdims=True)
    kst_ref[0, 0] = jnp.concatenate(
        [kn2, c[0:1, :], c[tm - 1:tm, :], jnp.zeros((5, LANES), F32)], axis=0)

    rows = 512
    for c0 in range(0, 3 * width, rows):
        res = lax.dot_general(wt_ref[c0:c0 + rows, :], hn, NT_DIMS,
                              preferred_element_type=F32).astype(BF16)
        which, off = divmod(c0, width)
        if which == 0:
            qt_ref[0, off:off + rows, :] = res
        elif which == 1:
            for j in range(tm // ch):
                vt_ref[0, j, off:off + rows, :] = res[:, j * ch:(j + 1) * ch]
        else:
            zt_ref[0, off:off + rows, :] = res

    hi, mid, lo = _split3(ct * LOG2E)
    ones = jnp.where(lax.broadcasted_iota(jnp.int32, (16, tm), 0) == 0, 1.0, 0.0)
    slots_t = jnp.concatenate([hi, mid, lo, ones], axis=0).astype(BF16)
    augq_ref[0] = _dot(pqt_ref[...], slots_t).astype(BF16)
    qf = qt_ref[0].astype(F32)
    qn2 = jnp.max(_dot(gq_ref[...], (qf * qf).astype(BF16)), axis=1, keepdims=True)
    qst_ref[0, 0] = jnp.broadcast_to(qn2, (FOX_HEADS, LANES))


def _placement_matrices():
    pk = np.zeros((LANES, FOX_PAIRS * LANES), np.float32)
    pqt = np.zeros((FOX_HEADS * AUG_SLOTS_PER_HEAD, 64), np.float32)
    for h in range(FOX_HEADS):
        p, odd = divmod(h, 2)
        base = 6 * odd
        for piece in range(3):
            pk[48, LANES * p + base + piece] = 1.0
            pk[16 * piece + h, LANES * p + base + 3 + piece] = 1.0
            pqt[AUG_SLOTS_PER_HEAD * h + base + piece, 16 * piece + h] = 1.0
            pqt[AUG_SLOTS_PER_HEAD * h + base + 3 + piece, 48] = 1.0
    return jnp.asarray(pk, BF16), jnp.asarray(pqt, BF16)


def _fox_in(x, norm_g, w_in, b_f, *, tm, ch):
    b, s, d = x.shape
    width = FOX_HEADS * FOX_HEAD_DIM
    wq = w_in[:, :width] * (FOX_HEAD_DIM ** -0.5 * LOG2E)
    wk = w_in[:, width:2 * width]
    wv = w_in[:, 2 * width:3 * width]
    wz = w_in[:, 3 * width:4 * width]
    wf = w_in[:, 4 * width:]
    wt = jnp.concatenate([wq, wv, wz], axis=1).astype(BF16).T
    wft = wf.T.astype(BF16)
    bt = b_f.reshape(FOX_HEADS, 1)
    r = np.arange(tm)
    triu = jnp.asarray(r[:, None] <= r[None, :], BF16)
    pk, pqt = _placement_matrices()
    head_of = np.arange(width) // FOX_HEAD_DIM
    gk = jnp.asarray(head_of[:, None] == np.arange(LANES)[None, :], BF16)
    gq = jnp.asarray(np.arange(FOX_HEADS)[:, None] == head_of[None, :], BF16)

    const = lambda shape: pl.BlockSpec(shape, lambda bi, i: (0,) * len(shape))
    return pl.pallas_call(
        functools.partial(_fox_in_kernel, tm=tm, ch=ch),
        grid=(b, s // tm),
        in_specs=[
            pl.BlockSpec((1, tm, d), lambda bi, i: (bi, i, 0)),
            const((1, d)),
            const(wk.shape), const(wt.shape), const(wft.shape),
            const((FOX_HEADS, 1)),
            const((tm, tm)),
            const(pk.shape), const(pqt.shape), const(gk.shape), const(gq.shape),
        ],
        out_specs=[
            pl.BlockSpec((1, tm, 2 * width), lambda bi, i: (bi, i, 0)),
            pl.BlockSpec((1, width, tm), lambda bi, i: (bi, 0, i)),
            pl.BlockSpec((1, tm // ch, width, ch), lambda bi, i: (bi, i, 0, 0)),
            pl.BlockSpec((1, width, tm), lambda bi, i: (bi, 0, i)),
            pl.BlockSpec((1, FOX_HEADS * AUG_SLOTS_PER_HEAD, tm), lambda bi, i: (bi, 0, i)),
            pl.BlockSpec((1, 1, 8, LANES), lambda bi, i: (bi, i, 0, 0)),
            pl.BlockSpec((1, 1, FOX_HEADS, LANES), lambda bi, i: (bi, i, 0, 0)),
        ],
        out_shape=[
            jax.ShapeDtypeStruct((b, s, 2 * width), BF16),
            jax.ShapeDtypeStruct((b, width, s), BF16),
            jax.ShapeDtypeStruct((b, s // ch, width, ch), BF16),
            jax.ShapeDtypeStruct((b, width, s), BF16),
            jax.ShapeDtypeStruct((b, FOX_HEADS * AUG_SLOTS_PER_HEAD, s), BF16),
            jax.ShapeDtypeStruct((b, s // tm, 8, LANES), F32),
            jax.ShapeDtypeStruct((b, s // tm, FOX_HEADS, LANES), F32),
        ],
        scratch_shapes=[pltpu.VMEM((FOX_HEADS, LANES), F32)],
        compiler_params=_cparams(2),
        name="fox_in",
    )(x, norm_g.reshape(1, d), wk.astype(BF16), wt, wft, bt, triu, pk, pqt, gk, gq)


def _fox_attn_kernel(kstart_ref, kaug_ref, vt_ref, qt_ref, augq_ref, ot_ref,
                     s_scr, p_scr, *, tq, tk, pairs_per_step):
    qi = pl.program_id(2)
    hd = FOX_HEAD_DIM
    zeros_hd = jnp.zeros((hd, tq), BF16)
    zeros_tail = jnp.zeros((LANES - AUG_SLOTS_PER_HEAD, tq), BF16)
    ones_block = lambda n: jnp.where(lax.broadcasted_iota(jnp.int32, (16, n), 0) == 0, 1.0, 0.0).astype(BF16)
    ones_rows = ones_block(tk)
    n_blocks = (qi * tq) // tk + 1
    k_first = kstart_ref[pl.program_id(0), pl.program_id(1), qi]

    heads = [(u, h) for u in range(pairs_per_step) for h in range(2)]
    w_head = {}
    for u, h in heads:
        q_rows = qt_ref[0, LANES * u + hd * h:LANES * u + hd * (h + 1), :]
        aug_rows = augq_ref[0, 2 * AUG_SLOTS_PER_HEAD * u + AUG_SLOTS_PER_HEAD * h:
                            2 * AUG_SLOTS_PER_HEAD * u + AUG_SLOTS_PER_HEAD * (h + 1), :]
        q_part = [q_rows, zeros_hd] if h == 0 else [zeros_hd, q_rows]
        w_head[u, h] = jnp.concatenate(q_part + [aug_rows, zeros_tail], axis=0)
    chains = [(u, h, sub) for u, h in heads for sub in range(tq // QUERY_LANES)]

    def issue_scores(kj, slot):
        ks = pl.multiple_of(kj * tk, tk)
        block_max = []
        for c, (u, h, sub) in enumerate(chains):
            kb = kaug_ref[0, pl.ds(ks, tk), 2 * LANES * u:2 * LANES * (u + 1)]
            s = _dot(kb, w_head[u, h][:, QUERY_LANES * sub:QUERY_LANES * (sub + 1)])
            s_scr[slot, c] = s
            block_max.append(jnp.max(s, axis=0, keepdims=True))
        return tuple(block_max)

    tri_key = lax.broadcasted_iota(jnp.int32, (QUERY_LANES, QUERY_LANES), 0)
    tri_qry = lax.broadcasted_iota(jnp.int32, (QUERY_LANES, QUERY_LANES), 1)
    causal = jnp.where(tri_key <= tri_qry, 0.0, MASK_VALUE)

    def pv_block(kj, c):
        u, h, _ = chains[c]
        v_rows = vt_ref[0, kj, LANES * u + hd * h:LANES * u + hd * (h + 1), :]
        return _dot(jnp.concatenate([v_rows, ones_rows], axis=0), p_scr[c])

    def step(kj, slot, state, has_prev=True):
        carry, block_max = state
        next_max = issue_scores(kj + 1, 1 - slot)
        out = []
        for c in range(len(chains)):
            m_old, alpha_prev, acc = carry[c]
            if has_prev:
                acc = acc * alpha_prev + pv_block(kj - 1, c)
            m_new = jnp.maximum(m_old, block_max[c])
            p_scr[c] = jnp.exp2(s_scr[slot, c] - m_new).astype(BF16)
            out.append((m_new, jnp.exp2(m_old - m_new), acc))
        return tuple(out), next_max

    def diagonal(kj, slot, state):
        carry, _ = state
        for c, (u, h, sub) in enumerate(chains):
            m_old, alpha_prev, acc = carry[c]
            acc = acc * alpha_prev + pv_block(jnp.maximum(kj - 1, 0), c)
            lo = QUERY_LANES * sub
            vis = lo + QUERY_LANES
            mid = s_scr[slot, c, lo:vis, :] + causal
            m_new = jnp.maximum(m_old, jnp.max(mid, axis=0, keepdims=True))
            if lo:
                top = s_scr[slot, c, 0:lo, :]
                m_new = jnp.maximum(m_new, jnp.max(top, axis=0, keepdims=True))
                p = jnp.concatenate([jnp.exp2(top - m_new).astype(BF16),
                                     jnp.exp2(mid - m_new).astype(BF16)], axis=0)
            else:
                p = jnp.exp2(mid - m_new).astype(BF16)
            v_rows = vt_ref[0, kj, LANES * u + hd * h:LANES * u + hd * (h + 1), 0:vis]
            pv = _dot(jnp.concatenate([v_rows, ones_block(vis)], axis=0), p)
            acc = acc * jnp.exp2(m_old - m_new) + pv
            ot_ref[0, LANES * u + hd * h:LANES * u + hd * (h + 1),
                   QUERY_LANES * sub:QUERY_LANES * (sub + 1)] = (acc[0:hd] / acc[hd:hd + 1]).astype(BF16)

    row = lambda value: jnp.full((1, QUERY_LANES), value, F32)
    carry = tuple((row(MASK_VALUE), row(1.0), jnp.zeros((hd + 16, QUERY_LANES), F32)) for _ in chains)
    state = (carry, tuple(row(0.0) for _ in chains))
    n_unmasked = n_blocks - 1 - k_first
    has_unmasked = jnp.minimum(n_unmasked, 1)

    def first(_, st):
        return step(k_first, 0, (st[0], issue_scores(k_first, 0)), has_prev=False)

    def only_diagonal(_, st):
        p_scr[...] = jnp.zeros_like(p_scr)
        return st[0], issue_scores(k_first, 0)

    def pair(i, st):
        kj = k_first + 1 + 2 * i
        return step(kj + 1, 0, step(kj, 1, st))

    state = lax.fori_loop(0, has_unmasked, first, state)
    state = lax.fori_loop(0, 1 - has_unmasked, only_diagonal, state)
    n_rest = jnp.maximum(n_unmasked - 1, 0)
    state = lax.fori_loop(0, n_rest // 2, pair, state)
    state = lax.fori_loop(0, lax.rem(n_rest, 2), lambda _, st: step(n_blocks - 2, 1, st), state)
    diagonal(n_blocks - 1, lax.rem(n_unmasked, 2), state)


def _fox_prune_table(kst, qst, heads_per_step):
    kn = jnp.sqrt(kst[:, :, 0, :FOX_HEADS]) * NORM_SLACK
    qn = jnp.sqrt(qst[:, :, :, 0]) * NORM_SLACK
    c_first = kst[:, :, 1, :FOX_HEADS] * LOG2E
    c_last = kst[:, :, 2, :FOX_HEADS] * LOG2E
    bound = (qn[:, :, None, :] * kn[:, None, :, :] + c_first[:, :, None, :] - c_last[:, None, :, :]
             + (qn * kn)[:, :, None, :])
    n = kst.shape[1]
    earlier = jnp.arange(n)[None, :, None] < jnp.arange(n)[:, None, None]
    skip = (bound < -PRUNE_LOG2) & earlier[None]
    block = jnp.arange(n, dtype=jnp.int32)[None, None, :, None]
    lead = jnp.min(jnp.where(skip, n, block), axis=2)
    lead = jnp.min(lead.reshape(lead.shape[0], n, FOX_HEADS // heads_per_step, heads_per_step), axis=3)
    return jnp.transpose(lead, (0, 2, 1)).astype(jnp.int32)


def _fox_attn(kaug, qt, vt, augq, kstart, *, tq, tk, pairs_per_step):
    b, s, _ = kaug.shape
    width = FOX_HEADS * FOX_HEAD_DIM
    nk = s // tk
    pps = pairs_per_step
    n_chains = 2 * pps * (tq // QUERY_LANES)
    grid_spec = pltpu.PrefetchScalarGridSpec(
        num_scalar_prefetch=1,
        grid=(b, FOX_PAIRS // pps, s // tq),
        in_specs=[
            pl.BlockSpec((1, s, 2 * LANES * pps), lambda bi, p, qi, ks: (bi, 0, p)),
            pl.BlockSpec((1, nk, LANES * pps, tk), lambda bi, p, qi, ks: (bi, 0, p, 0)),
            pl.BlockSpec((1, LANES * pps, tq), lambda bi, p, qi, ks: (bi, p, qi)),
            pl.BlockSpec((1, 2 * AUG_SLOTS_PER_HEAD * pps, tq), lambda bi, p, qi, ks: (bi, p, qi)),
        ],
        out_specs=pl.BlockSpec((1, LANES * pps, tq), lambda bi, p, qi, ks: (bi, p, qi)),
        scratch_shapes=[pltpu.VMEM((2, n_chains, tk, QUERY_LANES), F32),
                        pltpu.VMEM((n_chains, tk, QUERY_LANES), BF16)],
    )
    return pl.pallas_call(
        functools.partial(_fox_attn_kernel, tq=tq, tk=tk, pairs_per_step=pps),
        grid_spec=grid_spec,
        out_shape=jax.ShapeDtypeStruct((b, width, s), BF16),
        compiler_params=_cparams(3),
        name="fox_attn",
    )(kstart, kaug, vt, qt, augq)


def _ple(h1, p, wup_ref, wgate_ref):
    gate = jax.nn.sigmoid(_dot(h1.astype(BF16), wgate_ref[...]))
    up = _dot(p.astype(BF16), wup_ref[...])
    return h1 + up * gate


def _fox_out_kernel(ot_ref, zt_ref, x_ref, p_ref, wout_ref, wup_ref, wgate_ref, h_ref):
    z = zt_ref[0].astype(F32)
    gt = (ot_ref[0].astype(F32) * (z * jax.nn.sigmoid(z))).astype(BF16)
    y = lax.dot_general(gt, wout_ref[...], TN_DIMS, preferred_element_type=F32)
    h_ref[0] = _ple(x_ref[0] + y, p_ref[0, 0], wup_ref, wgate_ref)


def _fox_out(ot, zt, x, p, layer, w_out, w_up, w_gate, *, tm):
    b, s, d = x.shape
    width = ot.shape[1]
    pd = p.shape[-1]
    const = lambda shape: pl.BlockSpec(shape, lambda bi, i: (0,) * len(shape))
    return pl.pallas_call(
        _fox_out_kernel,
        grid=(b, s // tm),
        in_specs=[
            pl.BlockSpec((1, width, tm), lambda bi, i: (bi, 0, i)),
            pl.BlockSpec((1, width, tm), lambda bi, i: (bi, 0, i)),
            pl.BlockSpec((1, tm, d), lambda bi, i: (bi, i, 0)),
            pl.BlockSpec((1, 1, tm, pd), lambda bi, i: (layer, bi, i, 0)),
            const((width, d)), const((pd, d)), const((d, d)),
        ],
        out_specs=pl.BlockSpec((1, tm, d), lambda bi, i: (bi, i, 0)),
        out_shape=jax.ShapeDtypeStruct((b, s, d), F32),
        compiler_params=_cparams(2),
        name="fox_out",
    )(ot, zt, x, p, w_out.astype(BF16), w_up.astype(BF16), w_gate.astype(BF16))


PERM_ROWS = 256
DIL_SUB_BLOCKS = 8


def _phase_major_matrix(dilation):
    n = PERM_ROWS // dilation
    p = np.zeros((PERM_ROWS, PERM_ROWS), np.float32)
    for r in range(dilation):
        for i in range(n):
            p[r * n + i, dilation * i + r] = 1.0
    return p


def _dil_in_kernel(h_ref, g_ref, *rest, dilations, parts, tm):
    n_w = sum(parts)
    n_perm = sum(d > 1 for d in dilations)
    w_refs, perm_refs, out_refs = rest[:n_w], list(rest[n_w:n_w + n_perm]), rest[n_w + n_perm:]
    hn = _rms_norm(h_ref[0], g_ref[...]).astype(BF16)
    first = 0
    for dil, n_parts, o_ref in zip(dilations, parts, out_refs):
        group_w = w_refs[first:first + n_parts]
        first += n_parts
        pc = group_w[0].shape[1]
        if dil == 1:
            for part, w_ref in enumerate(group_w):
                o_ref[0, :, pc * part:pc * (part + 1)] = _dot(hn, w_ref[...]).astype(BF16)
            continue
        perm = perm_refs.pop(0)[...]
        n = PERM_ROWS // dil
        for c in range(tm // PERM_ROWS):
            rows = slice(PERM_ROWS * c, PERM_ROWS * (c + 1))
            hp = _dot(perm, hn[rows]).astype(BF16)
            for part, w_ref in enumerate(group_w):
                res = _dot(hp, w_ref[...]).astype(BF16)
                for r in range(dil):
                    dst = pc * (part * dil + r)
                    o_ref[0, n * c:n * (c + 1), dst:dst + pc] = res[n * r:n * (r + 1)]


def _dil_in(h, norm_g, w, groups, dilations, *, tm, part_cols):
    b, s, d = h.shape
    parts = tuple(len(blocks) for blocks in groups)
    perms = [jnp.asarray(_phase_major_matrix(dil), BF16) for dil in dilations if dil > 1]
    const = lambda shape: pl.BlockSpec(shape, lambda bi, i: (0,) * len(shape))
    w_specs = [pl.BlockSpec((d, part_cols), lambda bi, i, cb=cb: (0, cb))
               for blocks in groups for cb in blocks]
    widths = [dil * n_parts * part_cols for dil, n_parts in zip(dilations, parts)]
    return pl.pallas_call(
        functools.partial(_dil_in_kernel, dilations=dilations, parts=parts, tm=tm),
        grid=(b, s // tm),
        in_specs=[pl.BlockSpec((1, tm, d), lambda bi, i: (bi, i, 0)), const((1, d))]
                 + w_specs + [const(p.shape) for p in perms],
        out_specs=[pl.BlockSpec((1, tm // dil, width), lambda bi, i: (bi, i, 0))
                   for dil, width in zip(dilations, widths)],
        out_shape=[jax.ShapeDtypeStruct((b, s // dil, width), BF16)
                   for dil, width in zip(dilations, widths)],
        compiler_params=_cparams(2),
        name="dil_in_" + "_".join(str(dil) for dil in dilations),
    )(h, norm_g.reshape(1, d), *([w] * sum(parts)), *perms)


def _dil_attn_kernel(q_ref, kc_ref, vc_ref, kp_ref, vp_ref, o_ref, st_ref,
                     kcat, vcat, bias_scr, *, dilation, kb, phases, neg_slopes):
    n = pl.program_id(2)
    nw = DIL_WINDOW_STEPS
    hd = DIL_HEAD_DIM
    gw = DIL_HEADS_PER_GROUP * hd

    @pl.when((pl.program_id(0) == 0) & (pl.program_id(1) == 0) & (n == 0))
    def _():
        row = lax.broadcasted_iota(jnp.int32, (nw, 2 * nw), 0)
        col = lax.broadcasted_iota(jnp.int32, (nw, 2 * nw), 1)
        dist = nw + row - col
        band = (dist >= 0) & (dist <= nw)
        dist_f = (dist * dilation).astype(F32)
        for h in range(DIL_HEADS_PER_GROUP):
            bias_scr[h] = jnp.where(band, neg_slopes[h] * dist_f * LOG2E, MASK_VALUE)

    for ph in range(phases):
        cols = slice(gw * ph, gw * (ph + 1))
        kcat[ph, 0:nw] = kp_ref[0, :, cols]
        kcat[ph, nw:nw * (kb + 1)] = kc_ref[0, :, cols]
        vcat[ph, 0:nw] = vp_ref[0, :, cols]
        vcat[ph, nw:nw * (kb + 1)] = vc_ref[0, :, cols]
    lane = lax.broadcasted_iota(jnp.int32, (nw, LANES), 1)
    col = lax.broadcasted_iota(jnp.int32, (nw, 2 * nw), 1)
    no_history = jnp.where(col < jnp.where(n == 0, nw, 0), MASK_VALUE, 0.0)

    heads = [slice(hd * h, hd * (h + 1)) for h in range(DIL_HEADS_PER_GROUP)]
    problems = [(ph, jb) for ph in range(phases) for jb in range(kb)]
    scores = [[lax.dot_general(q_ref[0, nw * jb:nw * (jb + 1), gw * ph + hd * h:gw * ph + hd * (h + 1)],
                               kcat[ph, nw * jb:nw * (jb + 2), hs], NT_DIMS,
                               preferred_element_type=F32) for h, hs in enumerate(heads)]
              for ph, jb in problems]
    for i, (ph, jb) in enumerate(problems):
        rows = slice(nw * jb, nw * (jb + 1))
        stat_tile = jnp.zeros((nw, LANES), F32)
        for h, hs in enumerate(heads):
            t = scores[i][h] + bias_scr[h]
            if jb == 0:
                t = t + no_history
            m = jnp.max(t, axis=1, keepdims=True)
            p = jnp.exp2(t - m)
            l = jnp.sum(p, axis=1, keepdims=True)
            o_ref[0, rows, gw * ph + hd * h:gw * ph + hd * (h + 1)] = _dot(
                p.astype(BF16), vcat[ph, nw * jb:nw * (jb + 2), hs]).astype(BF16)
            stat_tile = jnp.where(lane == h, m, stat_tile)
            stat_tile = jnp.where(lane == h + DIL_HEADS_PER_GROUP, l, stat_tile)
        st_ref[0, rows, LANES * ph:LANES * (ph + 1)] = stat_tile


def _alibi_neg_slopes(group):
    n = len(DIL_PATTERN) * DIL_HEADS_PER_GROUP
    k = np.arange(1, n + 1, dtype=np.float32)
    slopes = np.float32(2.0) ** (np.float32(-ALIBI_MAX_EXP) * k / np.float32(n))
    lo = group * DIL_HEADS_PER_GROUP
    return tuple(float(-v) for v in slopes[lo:lo + DIL_HEADS_PER_GROUP])


def _dil_attn(view, group, *, kb, phases):
    _, dilation = DIL_PATTERN[group]
    b, length, _ = view.shape
    nw = DIL_WINDOW_STEPS
    gw = DIL_HEADS_PER_GROUP * DIL_HEAD_DIM
    rows = nw * kb
    steps = dilation // phases

    def cur(part):
        return pl.BlockSpec((1, rows, phases * gw), lambda bi, r, n: (bi, n, part * steps + r))

    def prev(part):
        return pl.BlockSpec((1, nw, phases * gw),
                            lambda bi, r, n: (bi, jnp.maximum(n * kb - 1, 0), part * steps + r))

    o, st = pl.pallas_call(
        functools.partial(_dil_attn_kernel, dilation=dilation, kb=kb, phases=phases,
                          neg_slopes=_alibi_neg_slopes(group)),
        grid=(b, steps, length // rows),
        in_specs=[cur(0), cur(1), cur(2), prev(1), prev(2)],
        out_specs=[
            pl.BlockSpec((1, rows, phases * gw), lambda bi, r, n: (bi, n, r)),
            pl.BlockSpec((1, rows, phases * LANES), lambda bi, r, n: (bi, n, r)),
        ],
        out_shape=[
            jax.ShapeDtypeStruct((b, length, dilation * gw), BF16),
            jax.ShapeDtypeStruct((b, length, dilation * LANES), F32),
        ],
        scratch_shapes=[pltpu.VMEM((phases, nw * (kb + 1), gw), BF16),
                        pltpu.VMEM((phases, nw * (kb + 1), gw), BF16),
                        pltpu.VMEM((DIL_HEADS_PER_GROUP, nw, 2 * nw), F32)],
        compiler_params=_cparams(3),
        name=f"dil_attn_g{group}",
    )(view, view, view, view, view)
    return o, st


def _natural_rows(blk, perm_t, dilation, tm):
    cols = blk.shape[1] // dilation
    if dilation == 1:
        return blk
    n = PERM_ROWS // dilation
    chunks = []
    for c in range(tm // PERM_ROWS):
        phase_major = jnp.concatenate(
            [blk[n * c:n * (c + 1), cols * r:cols * (r + 1)] for r in range(dilation)], axis=0)
        chunks.append(_dot(perm_t, phase_major))
    return jnp.concatenate(chunks, axis=0)


def _dil_out_kernel(o0_ref, o1_ref, o2_ref, s0_ref, s1_ref, s2_ref, z_ref, h_ref, p_ref,
                    pt1_ref, pt2_ref, e_ref, wout_ref, wup_ref, wgate_ref, g_ref,
                    out_ref, *, dilations, tm):
    perm_ts = (None, pt1_ref[...], pt2_ref[...])
    n_heads = DIL_HEADS_PER_GROUP
    row_max, row_sum = [], []
    for st_ref, perm_t, dil in zip((s0_ref, s1_ref, s2_ref), perm_ts, dilations):
        if dil == 1:
            st = st_ref[0]
        else:
            hi, mid, lo = (_natural_rows(piece.astype(BF16), perm_t, dil, tm)
                           for piece in _split3(st_ref[0]))
            st = hi + mid + lo
        row_max.append(st)
        row_sum.append(pltpu.roll(st, LANES - n_heads, 1))
    m = jnp.maximum(jnp.maximum(row_max[0], row_max[1]), row_max[2])
    e = [jnp.exp2(v - m) for v in row_max]
    inv = 1.0 / (e[0] * row_sum[0] + e[1] * row_sum[1] + e[2] * row_sum[2])
    live = lax.broadcasted_iota(jnp.int32, (tm, LANES), 1) < n_heads
    o = None
    for ev, o_ref, perm_t, dil in zip(e, (o0_ref, o1_ref, o2_ref), perm_ts, dilations):
        w = jnp.where(live, ev * inv, 0.0)
        w_hi = w.astype(BF16)
        w_lo = (w - w_hi.astype(F32)).astype(BF16)
        w_full = _dot(jnp.concatenate([w_hi, w_lo], axis=1), e_ref[...])
        term = w_full * _natural_rows(o_ref[0], perm_t, dil, tm)
        o = term if o is None else o + term
    z = z_ref[0].astype(F32)
    g = (o * (z * jax.nn.sigmoid(z))).astype(BF16)
    h1 = h_ref[0] + _dot(g, wout_ref[...])
    out_ref[0] = _rms_norm(_ple(h1, p_ref[0, 0], wup_ref, wgate_ref), g_ref[...])


def _dil_out(outs, stats, proj0, h, p, layer, w_out, w_up, w_gate, final_g, *, tm):
    b, s, d = h.shape
    gw = DIL_HEADS_PER_GROUP * DIL_HEAD_DIM
    pd = p.shape[-1]
    dilations = tuple(dil for _, dil in DIL_PATTERN)
    z_block = proj0.shape[-1] // gw - 1
    expand = np.zeros((2 * LANES, gw), np.float32)
    for hh in range(DIL_HEADS_PER_GROUP):
        expand[hh, DIL_HEAD_DIM * hh:DIL_HEAD_DIM * (hh + 1)] = 1.0
        expand[LANES + hh, DIL_HEAD_DIM * hh:DIL_HEAD_DIM * (hh + 1)] = 1.0
    perm_ts = [jnp.asarray(_phase_major_matrix(dil).T, BF16) for dil in dilations[1:]]
    const = lambda shape: pl.BlockSpec(shape, lambda bi, i: (0,) * len(shape))
    row = lambda w: pl.BlockSpec((1, tm, w), lambda bi, i: (bi, i, 0))
    view = lambda w: [pl.BlockSpec((1, tm // dil, dil * w), lambda bi, i: (bi, i, 0))
                      for dil in dilations]
    return pl.pallas_call(
        functools.partial(_dil_out_kernel, dilations=dilations, tm=tm),
        grid=(b, s // tm),
        in_specs=view(gw) + view(LANES) + [
            pl.BlockSpec((1, tm, gw), lambda bi, i: (bi, i, z_block)),
            row(d), pl.BlockSpec((1, 1, tm, pd), lambda bi, i: (layer, bi, i, 0)),
            const((PERM_ROWS, PERM_ROWS)), const((PERM_ROWS, PERM_ROWS)),
            const((2 * LANES, gw)), const((gw, d)), const((pd, d)), const((d, d)), const((1, d)),
        ],
        out_specs=row(d),
        out_shape=jax.ShapeDtypeStruct((b, s, d), F32),
        compiler_params=_cparams(2),
        name="dil_out",
    )(*outs, *stats, proj0, h, p, *perm_ts, jnp.asarray(expand, BF16),
      w_out.astype(BF16), w_up.astype(BF16), w_gate.astype(BF16), final_g.reshape(1, d))


def kernel(x, p, fox_norm, fox_w_in, fox_b_f, fox_w_out, dil_norm, dil_w_in, dil_w_out,
           ple_w_up, ple_w_gate, final_norm):
    s = x.shape[1]
    tm = min(512, s)
    tq = min(512, s)
    tk = min(512, s)
    assert tm == tq == tk, "the pruning table pairs fox_in tiles with query tiles and key blocks"
    kaug, qt, vt, zt, augq, kst, qst = _fox_in(x, fox_norm[0], fox_w_in[0], fox_b_f[0], tm=tm, ch=tk)
    pps = 2
    ot = _fox_attn(kaug, qt, vt, augq, _fox_prune_table(kst, qst, 2 * pps), tq=tq, tk=tk,
                   pairs_per_step=pps)
    h = _fox_out(ot, zt, x, p, 0, fox_w_out[0], ple_w_up[0], ple_w_gate[0], tm=min(2 * tm, s))

    gw = DIL_HEADS_PER_GROUP * DIL_HEAD_DIM
    n_groups = len(DIL_PATTERN)
    k_scale = DIL_HEAD_DIM ** -0.5 * LOG2E
    col = jnp.arange(dil_w_in.shape[-1]) // (n_groups * gw)
    w1 = (dil_w_in[0] * jnp.where(col == 1, k_scale, 1.0)[None, :]).astype(BF16)
    qkv = lambda g: [n_groups * part + g for part in range(3)]
    (proj0,) = _dil_in(h, dil_norm[0], w1, [qkv(0) + [3 * n_groups]], (1,), tm=min(2 * tm, s),
                       part_cols=gw)
    views = [proj0] + list(_dil_in(h, dil_norm[0], w1, [qkv(1), qkv(2)],
                                   (DIL_PATTERN[1][1], DIL_PATTERN[2][1]), tm=tm,
                                   part_cols=gw))
    outs, stats = [], []
    for group, (_, dilation) in enumerate(DIL_PATTERN):
        kb = min(DIL_SUB_BLOCKS, s // dilation // DIL_WINDOW_STEPS)
        phases = min(dilation, DIL_SUB_BLOCKS // kb)
        o, st = _dil_attn(views[group], group, kb=kb, phases=phases)
        outs.append(o)
        stats.append(st)
    return _dil_out(outs, stats, proj0, h, p, 1, dil_w_out[0], ple_w_up[1], ple_w_gate[1],
                    final_norm, tm=tm)
```

```python
import functools

import numpy as np
import jax
import jax.numpy as jnp
from jax import lax
from jax.experimental import pallas as pl
from jax.experimental.pallas import tpu as pltpu

F32 = jnp.float32
BF16 = jnp.bfloat16

RMS_EPS = 1e-6
FOX_HEADS = 16
FOX_HEAD_DIM = 64
FOX_PAIRS = FOX_HEADS // 2
DIL_PATTERN = ((128, 1), (512, 4), (2048, 16))
DIL_HEADS_PER_GROUP = 8
DIL_HEAD_DIM = 128
DIL_WINDOW_STEPS = 128
ALIBI_MAX_EXP = 8.0
MASK_VALUE = -1e30
LOG2E = 1.4426950408889634

LANES = 128
AUG_SLOTS_PER_HEAD = 16
QUERY_LANES = 256
PRUNE_LOG2 = 50.0
NORM_SLACK = 1.02
VMEM_LIMIT_BYTES = 56 * 1024 * 1024

NT_DIMS = (((1,), (1,)), ((), ()))
TN_DIMS = (((0,), (0,)), ((), ()))


def _cparams(n_axes):
    return pltpu.CompilerParams(
        dimension_semantics=("arbitrary",) * n_axes,
        vmem_limit_bytes=VMEM_LIMIT_BYTES,
    )


def _rms_norm(x, g):
    ms = jnp.mean(x * x, axis=-1, keepdims=True)
    return x * lax.rsqrt(ms + RMS_EPS) * g


def _log_sigmoid(x):
    return jnp.minimum(x, 0.0) - jnp.log1p(jnp.exp(-jnp.abs(x)))


def _split3(x):
    hi = x.astype(BF16).astype(F32)
    r1 = x - hi
    mid = r1.astype(BF16).astype(F32)
    lo = (r1 - mid).astype(BF16).astype(F32)
    return hi, mid, lo


def _dot(a, b):
    return jnp.dot(a, b, preferred_element_type=F32)


def _fox_in_kernel(x_ref, g_ref, wk_ref, wt_ref, wft_ref, bt_ref,
                   triu_ref, pk_ref, pqt_ref, gk_ref, gq_ref,
                   kaug_ref, qt_ref, vt_ref, zt_ref, augq_ref, kst_ref, qst_ref,
                   carryt_ref, *, tm, ch):
    i = pl.program_id(1)

    @pl.when(i == 0)
    def _():
        carryt_ref[...] = jnp.zeros_like(carryt_ref)

    hn = _rms_norm(x_ref[0], g_ref[...]).astype(BF16)
    width = FOX_HEADS * FOX_HEAD_DIM

    ft = lax.dot_general(wft_ref[...], hn, NT_DIMS, preferred_element_type=F32)
    hi, mid, lo = _split3(_log_sigmoid(ft + bt_ref[...]))
    stack = jnp.concatenate([hi, mid, lo], axis=0).astype(BF16)
    cs = _dot(stack, triu_ref[...])
    ct = cs[0:16] + cs[16:32] + cs[32:48] + carryt_ref[:, 0:1]
    carryt_ref[...] = jnp.broadcast_to(ct[:, tm - 1:tm], carryt_ref.shape)
    c = jnp.concatenate([ct, jnp.zeros((LANES - FOX_HEADS, tm), F32)], axis=0).T

    k = _dot(hn, wk_ref[...])
    nhi, nmid, nlo = _split3(c * -LOG2E)
    lane = lax.broadcasted_iota(jnp.int32, (tm, LANES), 1)
    slots = jnp.where(lane < 16, nhi,
                      jnp.where(lane < 32, pltpu.roll(nmid, 16, 1),
                                jnp.where(lane < 48, pltpu.roll(nlo, 32, 1),
                                          jnp.where(lane == 48, 1.0, 0.0))))
    kaug = _dot(slots.astype(BF16), pk_ref[...])
    kb = k.astype(BF16)
    for p in range(FOX_PAIRS):
        kaug_ref[0, :, 2 * LANES * p:2 * LANES * p + LANES] = kb[:, LANES * p:LANES * (p + 1)]
        kaug_ref[0, :, 2 * LANES * p + LANES:2 * LANES * (p + 1)] = (
            kaug[:, LANES * p:LANES * (p + 1)].astype(BF16))
    kf = kb.astype(F32)
    kn2 = jnp.max(_dot((kf * kf).astype(BF16), gk_ref[...]), axis=0, keepdims=True)
    kst_ref[0, 0] = jnp.concatenate(
        [kn2, c[0:1, :], c[tm - 1:tm, :], jnp.zeros((5, LANES), F32)], axis=0)

    rows = 512
    for c0 in range(0, 3 * width, rows):
        res = lax.dot_general(wt_ref[c0:c0 + rows, :], hn, NT_DIMS,
                              preferred_element_type=F32).astype(BF16)
        which, off = divmod(c0, width)
        if which == 0:
            qt_ref[0, off:off + rows, :] = res
        elif which == 1:
            for j in range(tm // ch):
                vt_ref[0, j, off:off + rows, :] = res[:, j * ch:(j + 1) * ch]
        else:
            zt_ref[0, off:off + rows, :] = res

    hi, mid, lo = _split3(ct * LOG2E)
    ones = jnp.where(lax.broadcasted_iota(jnp.int32, (16, tm), 0) == 0, 1.0, 0.0)
    slots_t = jnp.concatenate([hi, mid, lo, ones], axis=0).astype(BF16)
    augq_ref[0] = _dot(pqt_ref[...], slots_t).astype(BF16)
    qf = qt_ref[0].astype(F32)
    qn2 = jnp.max(_dot(gq_ref[...], (qf * qf).astype(BF16)), axis=1, keepdims=True)
    qst_ref[0, 0] = jnp.broadcast_to(qn2, (FOX_HEADS, LANES))


def _placement_matrices():
    pk = np.zeros((LANES, FOX_PAIRS * LANES), np.float32)
    pqt = np.zeros((FOX_HEADS * AUG_SLOTS_PER_HEAD, 64), np.float32)
    for h in range(FOX_HEADS):
        p, odd = divmod(h, 2)
        base = 6 * odd
        for piece in range(3):
            pk[48, LANES * p + base + piece] = 1.0
            pk[16 * piece + h, LANES * p + base + 3 + piece] = 1.0
            pqt[AUG_SLOTS_PER_HEAD * h + base + piece, 16 * piece + h] = 1.0
            pqt[AUG_SLOTS_PER_HEAD * h + base + 3 + piece, 48] = 1.0
    return jnp.asarray(pk, BF16), jnp.asarray(pqt, BF16)


def _fox_in(x, norm_g, w_in, b_f, *, tm, ch):
    b, s, d = x.shape
    width = FOX_HEADS * FOX_HEAD_DIM
    wq = w_in[:, :width] * (FOX_HEAD_DIM ** -0.5 * LOG2E)
    wk = w_in[:, width:2 * width]
    wv = w_in[:, 2 * width:3 * width]
    wz = w_in[:, 3 * width:4 * width]
    wf = w_in[:, 4 * width:]
    wt = jnp.concatenate([wq, wv, wz], axis=1).astype(BF16).T
    wft = wf.T.astype(BF16)
    bt = b_f.reshape(FOX_HEADS, 1)
    r = np.arange(tm)
    triu = jnp.asarray(r[:, None] <= r[None, :], BF16)
    pk, pqt = _placement_matrices()
    head_of = np.arange(width) // FOX_HEAD_DIM
    gk = jnp.asarray(head_of[:, None] == np.arange(LANES)[None, :], BF16)
    gq = jnp.asarray(np.arange(FOX_HEADS)[:, None] == head_of[None, :], BF16)

    const = lambda shape: pl.BlockSpec(shape, lambda bi, i: (0,) * len(shape))
    return pl.pallas_call(
        functools.partial(_fox_in_kernel, tm=tm, ch=ch),
        grid=(b, s // tm),
        in_specs=[
            pl.BlockSpec((1, tm, d), lambda bi, i: (bi, i, 0)),
            const((1, d)),
            const(wk.shape), const(wt.shape), const(wft.shape),
            const((FOX_HEADS, 1)),
            const((tm, tm)),
            const(pk.shape), const(pqt.shape), const(gk.shape), const(gq.shape),
        ],
        out_specs=[
            pl.BlockSpec((1, tm, 2 * width), lambda bi, i: (bi, i, 0)),
            pl.BlockSpec((1, width, tm), lambda bi, i: (bi, 0, i)),
            pl.BlockSpec((1, tm // ch, width, ch), lambda bi, i: (bi, i, 0, 0)),
            pl.BlockSpec((1, width, tm), lambda bi, i: (bi, 0, i)),
            pl.BlockSpec((1, FOX_HEADS * AUG_SLOTS_PER_HEAD, tm), lambda bi, i: (bi, 0, i)),
            pl.BlockSpec((1, 1, 8, LANES), lambda bi, i: (bi, i, 0, 0)),
            pl.BlockSpec((1, 1, FOX_HEADS, LANES), lambda bi, i: (bi, i, 0, 0)),
        ],
        out_shape=[
            jax.ShapeDtypeStruct((b, s, 2 * width), BF16),
            jax.ShapeDtypeStruct((b, width, s), BF16),
            jax.ShapeDtypeStruct((b, s // ch, width, ch), BF16),
            jax.ShapeDtypeStruct((b, width, s), BF16),
            jax.ShapeDtypeStruct((b, FOX_HEADS * AUG_SLOTS_PER_HEAD, s), BF16),
            jax.ShapeDtypeStruct((b, s // tm, 8, LANES), F32),
            jax.ShapeDtypeStruct((b, s // tm, FOX_HEADS, LANES), F32),
        ],
        scratch_shapes=[pltpu.VMEM((FOX_HEADS, LANES), F32)],
        compiler_params=_cparams(2),
        name="fox_in",
    )(x, norm_g.reshape(1, d), wk.astype(BF16), wt, wft, bt, triu, pk, pqt, gk, gq)


def _fox_attn_kernel(kstart_ref, kaug_ref, vt_ref, qt_ref, augq_ref, ot_ref,
                     s_scr, p_scr, *, tq, tk, pairs_per_step):
    qi = pl.program_id(2)
    hd = FOX_HEAD_DIM
    zeros_hd = jnp.zeros((hd, tq), BF16)
    zeros_tail = jnp.zeros((LANES - AUG_SLOTS_PER_HEAD, tq), BF16)
    ones_block = lambda n: jnp.where(lax.broadcasted_iota(jnp.int32, (16, n), 0) == 0, 1.0, 0.0).astype(BF16)
    ones_rows = ones_block(tk)
    n_blocks = (qi * tq) // tk + 1
    k_first = kstart_ref[pl.program_id(0), pl.program_id(1), qi]

    heads = [(u, h) for u in range(pairs_per_step) for h in range(2)]
    w_head = {}
    for u, h in heads:
        q_rows = qt_ref[0, LANES * u + hd * h:LANES * u + hd * (h + 1), :]
        aug_rows = augq_ref[0, 2 * AUG_SLOTS_PER_HEAD * u + AUG_SLOTS_PER_HEAD * h:
                            2 * AUG_SLOTS_PER_HEAD * u + AUG_SLOTS_PER_HEAD * (h + 1), :]
        q_part = [q_rows, zeros_hd] if h == 0 else [zeros_hd, q_rows]
        w_head[u, h] = jnp.concatenate(q_part + [aug_rows, zeros_tail], axis=0)
    chains = [(u, h, sub) for u, h in heads for sub in range(tq // QUERY_LANES)]

    def issue_scores(kj, slot):
        ks = pl.multiple_of(kj * tk, tk)
        block_max = []
        for c, (u, h, sub) in enumerate(chains):
            kb = kaug_ref[0, pl.ds(ks, tk), 2 * LANES * u:2 * LANES * (u + 1)]
            s = _dot(kb, w_head[u, h][:, QUERY_LANES * sub:QUERY_LANES * (sub + 1)])
            s_scr[slot, c] = s
            block_max.append(jnp.max(s, axis=0, keepdims=True))
        return tuple(block_max)

    tri_key = lax.broadcasted_iota(jnp.int32, (QUERY_LANES, QUERY_LANES), 0)
    tri_qry = lax.broadcasted_iota(jnp.int32, (QUERY_LANES, QUERY_LANES), 1)
    causal = jnp.where(tri_key <= tri_qry, 0.0, MASK_VALUE)

    def pv_block(kj, c):
        u, h, _ = chains[c]
        v_rows = vt_ref[0, kj, LANES * u + hd * h:LANES * u + hd * (h + 1), :]
        return _dot(jnp.concatenate([v_rows, ones_rows], axis=0), p_scr[c])

    def step(kj, slot, state, has_prev=True):
        carry, block_max = state
        next_max = issue_scores(kj + 1, 1 - slot)
        out = []
        for c in range(len(chains)):
            m_old, alpha_prev, acc = carry[c]
            if has_prev:
                acc = acc * alpha_prev + pv_block(kj - 1, c)
            m_new = jnp.maximum(m_old, block_max[c])
            p_scr[c] = jnp.exp2(s_scr[slot, c] - m_new).astype(BF16)
            out.append((m_new, jnp.exp2(m_old - m_new), acc))
        return tuple(out), next_max

    def diagonal(kj, slot, state):
        carry, _ = state
        for c, (u, h, sub) in enumerate(chains):
            m_old, alpha_prev, acc = carry[c]
            acc = acc * alpha_prev + pv_block(jnp.maximum(kj - 1, 0), c)
            lo = QUERY_LANES * sub
            vis = lo + QUERY_LANES
            mid = s_scr[slot, c, lo:vis, :] + causal
            m_new = jnp.maximum(m_old, jnp.max(mid, axis=0, keepdims=True))
            if lo:
                top = s_scr[slot, c, 0:lo, :]
                m_new = jnp.maximum(m_new, jnp.max(top, axis=0, keepdims=True))
                p = jnp.concatenate([jnp.exp2(top - m_new).astype(BF16),
                                     jnp.exp2(mid - m_new).astype(BF16)], axis=0)
            else:
                p = jnp.exp2(mid - m_new).astype(BF16)
            v_rows = vt_ref[0, kj, LANES * u + hd * h:LANES * u + hd * (h + 1), 0:vis]
            pv = _dot(jnp.concatenate([v_rows, ones_block(vis)], axis=0), p)
            acc = acc * jnp.exp2(m_old - m_new) + pv
            ot_ref[0, LANES * u + hd * h:LANES * u + hd * (h + 1),
                   QUERY_LANES * sub:QUERY_LANES * (sub + 1)] = (acc[0:hd] / acc[hd:hd + 1]).astype(BF16)

    row = lambda value: jnp.full((1, QUERY_LANES), value, F32)
    carry = tuple((row(MASK_VALUE), row(1.0), jnp.zeros((hd + 16, QUERY_LANES), F32)) for _ in chains)
    state = (carry, tuple(row(0.0) for _ in chains))
    n_unmasked = n_blocks - 1 - k_first
    has_unmasked = jnp.minimum(n_unmasked, 1)

    def first(_, st):
        return step(k_first, 0, (st[0], issue_scores(k_first, 0)), has_prev=False)

    def only_diagonal(_, st):
        p_scr[...] = jnp.zeros_like(p_scr)
        return st[0], issue_scores(k_first, 0)

    def pair(i, st):
        kj = k_first + 1 + 2 * i
        return step(kj + 1, 0, step(kj, 1, st))

    state = lax.fori_loop(0, has_unmasked, first, state)
    state = lax.fori_loop(0, 1 - has_unmasked, only_diagonal, state)
    n_rest = jnp.maximum(n_unmasked - 1, 0)
    state = lax.fori_loop(0, n_rest // 2, pair, state)
    state = lax.fori_loop(0, lax.rem(n_rest, 2), lambda _, st: step(n_blocks - 2, 1, st), state)
    diagonal(n_blocks - 1, lax.rem(n_unmasked, 2), state)


def _fox_prune_table(kst, qst, heads_per_step):
    kn = jnp.sqrt(kst[:, :, 0, :FOX_HEADS]) * NORM_SLACK
    qn = jnp.sqrt(qst[:, :, :, 0]) * NORM_SLACK
    c_first = kst[:, :, 1, :FOX_HEADS] * LOG2E
    c_last = kst[:, :, 2, :FOX_HEADS] * LOG2E
    bound = (qn[:, :, None, :] * kn[:, None, :, :] + c_first[:, :, None, :] - c_last[:, None, :, :]
             + (qn * kn)[:, :, None, :])
    n = kst.shape[1]
    earlier = jnp.arange(n)[None, :, None] < jnp.arange(n)[:, None, None]
    skip = (bound < -PRUNE_LOG2) & earlier[None]
    block = jnp.arange(n, dtype=jnp.int32)[None, None, :, None]
    lead = jnp.min(jnp.where(skip, n, block), axis=2)
    lead = jnp.min(lead.reshape(lead.shape[0], n, FOX_HEADS // heads_per_step, heads_per_step), axis=3)
    return jnp.transpose(lead, (0, 2, 1)).astype(jnp.int32)


def _fox_attn(kaug, qt, vt, augq, kstart, *, tq, tk, pairs_per_step):
    b, s, _ = kaug.shape
    width = FOX_HEADS * FOX_HEAD_DIM
    nk = s // tk
    pps = pairs_per_step
    n_chains = 2 * pps * (tq // QUERY_LANES)
    grid_spec = pltpu.PrefetchScalarGridSpec(
        num_scalar_prefetch=1,
        grid=(b, FOX_PAIRS // pps, s // tq),
        in_specs=[
            pl.BlockSpec((1, s, 2 * LANES * pps), lambda bi, p, qi, ks: (bi, 0, p)),
            pl.BlockSpec((1, nk, LANES * pps, tk), lambda bi, p, qi, ks: (bi, 0, p, 0)),
            pl.BlockSpec((1, LANES * pps, tq), lambda bi, p, qi, ks: (bi, p, qi)),
            pl.BlockSpec((1, 2 * AUG_SLOTS_PER_HEAD * pps, tq), lambda bi, p, qi, ks: (bi, p, qi)),
        ],
        out_specs=pl.BlockSpec((1, LANES * pps, tq), lambda bi, p, qi, ks: (bi, p, qi)),
        scratch_shapes=[pltpu.VMEM((2, n_chains, tk, QUERY_LANES), F32),
                        pltpu.VMEM((n_chains, tk, QUERY_LANES), BF16)],
    )
    return pl.pallas_call(
        functools.partial(_fox_attn_kernel, tq=tq, tk=tk, pairs_per_step=pps),
        grid_spec=grid_spec,
        out_shape=jax.ShapeDtypeStruct((b, width, s), BF16),
        compiler_params=_cparams(3),
        name="fox_attn",
    )(kstart, kaug, vt, qt, augq)


def _ple(h1, p, wup_ref, wgate_ref):
    gate = jax.nn.sigmoid(_dot(h1.astype(BF16), wgate_ref[...]))
    up = _dot(p.astype(BF16), wup_ref[...])
    return h1 + up * gate


def _fox_out_kernel(ot_ref, zt_ref, x_ref, p_ref, wout_ref, wup_ref, wgate_ref, h_ref):
    z = zt_ref[0].astype(F32)
    gt = (ot_ref[0].astype(F32) * (z * jax.nn.sigmoid(z))).astype(BF16)
    y = lax.dot_general(gt, wout_ref[...], TN_DIMS, preferred_element_type=F32)
    h_ref[0] = _ple(x_ref[0] + y, p_ref[0, 0], wup_ref, wgate_ref)


def _fox_out(ot, zt, x, p, layer, w_out, w_up, w_gate, *, tm):
    b, s, d = x.shape
    width = ot.shape[1]
    pd = p.shape[-1]
    const = lambda shape: pl.BlockSpec(shape, lambda bi, i: (0,) * len(shape))
    return pl.pallas_call(
        _fox_out_kernel,
        grid=(b, s // tm),
        in_specs=[
            pl.BlockSpec((1, width, tm), lambda bi, i: (bi, 0, i)),
            pl.BlockSpec((1, width, tm), lambda bi, i: (bi, 0, i)),
            pl.BlockSpec((1, tm, d), lambda bi, i: (bi, i, 0)),
            pl.BlockSpec((1, 1, tm, pd), lambda bi, i: (layer, bi, i, 0)),
            const((width, d)), const((pd, d)), const((d, d)),
        ],
        out_specs=pl.BlockSpec((1, tm, d), lambda bi, i: (bi, i, 0)),
        out_shape=jax.ShapeDtypeStruct((b, s, d), F32),
        compiler_params=_cparams(2),
        name="fox_out",
    )(ot, zt, x, p, w_out.astype(BF16), w_up.astype(BF16), w_gate.astype(BF16))


PERM_ROWS = 256
DIL_SUB_BLOCKS = 8


def _phase_major_matrix(dilation):
    n = PERM_ROWS // dilation
    p = np.zeros((PERM_ROWS, PERM_ROWS), np.float32)
    for r in range(dilation):
        for i in range(n):
            p[r * n + i, dilation * i + r] = 1.0
    return p


def _dil_in_kernel(h_ref, g_ref, *rest, dilations, parts, tm):
    n_w = sum(parts)
    n_perm = sum(d > 1 for d in dilations)
    w_refs, perm_refs, out_refs = rest[:n_w], list(rest[n_w:n_w + n_perm]), rest[n_w + n_perm:]
    hn = _rms_norm(h_ref[0], g_ref[...]).astype(BF16)
    first = 0
    for dil, n_parts, o_ref in zip(dilations, parts, out_refs):
        group_w = w_refs[first:first + n_parts]
        first += n_parts
        pc = group_w[0].shape[1]
        if dil == 1:
            for part, w_ref in enumerate(group_w):
                o_ref[0, :, pc * part:pc * (part + 1)] = _dot(hn, w_ref[...]).astype(BF16)
            continue
        perm = perm_refs.pop(0)[...]
        n = PERM_ROWS // dil
        for c in range(tm // PERM_ROWS):
            rows = slice(PERM_ROWS * c, PERM_ROWS * (c + 1))
            hp = _dot(perm, hn[rows]).astype(BF16)
            for part, w_ref in enumerate(group_w):
                res = _dot(hp, w_ref[...]).astype(BF16)
                for r in range(dil):
                    dst = pc * (part * dil + r)
                    o_ref[0, n * c:n * (c + 1), dst:dst + pc] = res[n * r:n * (r + 1)]


def _dil_in(h, norm_g, w, groups, dilations, *, tm, part_cols):
    b, s, d = h.shape
    parts = tuple(len(blocks) for blocks in groups)
    perms = [jnp.asarray(_phase_major_matrix(dil), BF16) for dil in dilations if dil > 1]
    const = lambda shape: pl.BlockSpec(shape, lambda bi, i: (0,) * len(shape))
    w_specs = [pl.BlockSpec((d, part_cols), lambda bi, i, cb=cb: (0, cb),
                            pipeline_mode=pl.Buffered(1))
               for blocks in groups for cb in blocks]
    widths = [dil * n_parts * part_cols for dil, n_parts in zip(dilations, parts)]
    return pl.pallas_call(
        functools.partial(_dil_in_kernel, dilations=dilations, parts=parts, tm=tm),
        grid=(b, s // tm),
        in_specs=[pl.BlockSpec((1, tm, d), lambda bi, i: (bi, i, 0)), const((1, d))]
                 + w_specs + [const(p.shape) for p in perms],
        out_specs=[pl.BlockSpec((1, tm // dil, width), lambda bi, i: (bi, i, 0))
                   for dil, width in zip(dilations, widths)],
        out_shape=[jax.ShapeDtypeStruct((b, s // dil, width), BF16)
                   for dil, width in zip(dilations, widths)],
        compiler_params=_cparams(2),
        name="dil_in_" + "_".join(str(dil) for dil in dilations),
    )(h, norm_g.reshape(1, d), *([w] * sum(parts)), *perms)


def _dil_attn_kernel(q_ref, kc_ref, vc_ref, kp_ref, vp_ref, o_ref, st_ref,
                     kcat, vcat, bias_scr, *, dilation, kb, phases, neg_slopes):
    n = pl.program_id(2)
    nw = DIL_WINDOW_STEPS
    hd = DIL_HEAD_DIM
    gw = DIL_HEADS_PER_GROUP * hd

    @pl.when((pl.program_id(0) == 0) & (pl.program_id(1) == 0) & (n == 0))
    def _():
        row = lax.broadcasted_iota(jnp.int32, (nw, 2 * nw), 0)
        col = lax.broadcasted_iota(jnp.int32, (nw, 2 * nw), 1)
        dist = nw + row - col
        band = (dist >= 0) & (dist <= nw)
        dist_f = (dist * dilation).astype(F32)
        for h in range(DIL_HEADS_PER_GROUP):
            bias_scr[h] = jnp.where(band, neg_slopes[h] * dist_f * LOG2E, MASK_VALUE)

    for ph in range(phases):
        cols = slice(gw * ph, gw * (ph + 1))
        kcat[ph, 0:nw] = kp_ref[0, :, cols]
        kcat[ph, nw:nw * (kb + 1)] = kc_ref[0, :, cols]
        vcat[ph, 0:nw] = vp_ref[0, :, cols]
        vcat[ph, nw:nw * (kb + 1)] = vc_ref[0, :, cols]
    lane = lax.broadcasted_iota(jnp.int32, (nw, LANES), 1)
    col = lax.broadcasted_iota(jnp.int32, (nw, 2 * nw), 1)
    no_history = jnp.where(col < jnp.where(n == 0, nw, 0), MASK_VALUE, 0.0)

    heads = [slice(hd * h, hd * (h + 1)) for h in range(DIL_HEADS_PER_GROUP)]
    problems = [(ph, jb) for ph in range(phases) for jb in range(kb)]
    scores = [[lax.dot_general(q_ref[0, nw * jb:nw * (jb + 1), gw * ph + hd * h:gw * ph + hd * (h + 1)],
                               kcat[ph, nw * jb:nw * (jb + 2), hs], NT_DIMS,
                               preferred_element_type=F32) for h, hs in enumerate(heads)]
              for ph, jb in problems]
    for i, (ph, jb) in enumerate(problems):
        rows = slice(nw * jb, nw * (jb + 1))
        stat_tile = jnp.zeros((nw, LANES), F32)
        for h, hs in enumerate(heads):
            t = scores[i][h] + bias_scr[h]
            if jb == 0:
                t = t + no_history
            m = jnp.max(t, axis=1, keepdims=True)
            p = jnp.exp2(t - m)
            l = jnp.sum(p, axis=1, keepdims=True)
            o_ref[0, rows, gw * ph + hd * h:gw * ph + hd * (h + 1)] = _dot(
                p.astype(BF16), vcat[ph, nw * jb:nw * (jb + 2), hs]).astype(BF16)
            stat_tile = jnp.where(lane == h, m, stat_tile)
            stat_tile = jnp.where(lane == h + DIL_HEADS_PER_GROUP, l, stat_tile)
        st_ref[0, rows, LANES * ph:LANES * (ph + 1)] = stat_tile


def _alibi_neg_slopes(group):
    n = len(DIL_PATTERN) * DIL_HEADS_PER_GROUP
    k = np.arange(1, n + 1, dtype=np.float32)
    slopes = np.float32(2.0) ** (np.float32(-ALIBI_MAX_EXP) * k / np.float32(n))
    lo = group * DIL_HEADS_PER_GROUP
    return tuple(float(-v) for v in slopes[lo:lo + DIL_HEADS_PER_GROUP])


def _dil_attn(view, group, *, kb, phases):
    _, dilation = DIL_PATTERN[group]
    b, length, _ = view.shape
    nw = DIL_WINDOW_STEPS
    gw = DIL_HEADS_PER_GROUP * DIL_HEAD_DIM
    rows = nw * kb
    steps = dilation // phases

    def cur(part):
        return pl.BlockSpec((1, rows, phases * gw), lambda bi, r, n: (bi, n, part * steps + r))

    def prev(part):
        return pl.BlockSpec((1, nw, phases * gw),
                            lambda bi, r, n: (bi, jnp.maximum(n * kb - 1, 0), part * steps + r))

    o, st = pl.pallas_call(
        functools.partial(_dil_attn_kernel, dilation=dilation, kb=kb, phases=phases,
                          neg_slopes=_alibi_neg_slopes(group)),
        grid=(b, steps, length // rows),
        in_specs=[cur(0), cur(1), cur(2), prev(1), prev(2)],
        out_specs=[
            pl.BlockSpec((1, rows, phases * gw), lambda bi, r, n: (bi, n, r)),
            pl.BlockSpec((1, rows, phases * LANES), lambda bi, r, n: (bi, n, r)),
        ],
        out_shape=[
            jax.ShapeDtypeStruct((b, length, dilation * gw), BF16),
            jax.ShapeDtypeStruct((b, length, dilation * LANES), F32),
        ],
        scratch_shapes=[pltpu.VMEM((phases, nw * (kb + 1), gw), BF16),
                        pltpu.VMEM((phases, nw * (kb + 1), gw), BF16),
                        pltpu.VMEM((DIL_HEADS_PER_GROUP, nw, 2 * nw), F32)],
        compiler_params=_cparams(3),
        name=f"dil_attn_g{group}",
    )(view, view, view, view, view)
    return o, st


def _natural_rows(blk, perm_t, dilation, tm):
    cols = blk.shape[1] // dilation
    if dilation == 1:
        return blk
    n = PERM_ROWS // dilation
    chunks = []
    for c in range(tm // PERM_ROWS):
        phase_major = jnp.concatenate(
            [blk[n * c:n * (c + 1), cols * r:cols * (r + 1)] for r in range(dilation)], axis=0)
        chunks.append(_dot(perm_t, phase_major))
    return jnp.concatenate(chunks, axis=0)


def _dil_out_kernel(o0_ref, o1_ref, o2_ref, s0_ref, s1_ref, s2_ref, z_ref, h_ref, p_ref,
                    pt1_ref, pt2_ref, e_ref, wout_ref, wup_ref, wgate_ref, g_ref,
                    out_ref, *, dilations, tm):
    perm_ts = (None, pt1_ref[...], pt2_ref[...])
    n_heads = DIL_HEADS_PER_GROUP
    row_max, row_sum = [], []
    for st_ref, perm_t, dil in zip((s0_ref, s1_ref, s2_ref), perm_ts, dilations):
        if dil == 1:
            st = st_ref[0]
        else:
            hi, mid, lo = (_natural_rows(piece.astype(BF16), perm_t, dil, tm)
                           for piece in _split3(st_ref[0]))
            st = hi + mid + lo
        row_max.append(st)
        row_sum.append(pltpu.roll(st, LANES - n_heads, 1))
    m = jnp.maximum(jnp.maximum(row_max[0], row_max[1]), row_max[2])
    e = [jnp.exp2(v - m) for v in row_max]
    inv = 1.0 / (e[0] * row_sum[0] + e[1] * row_sum[1] + e[2] * row_sum[2])
    live = lax.broadcasted_iota(jnp.int32, (tm, LANES), 1) < n_heads
    o = None
    for ev, o_ref, perm_t, dil in zip(e, (o0_ref, o1_ref, o2_ref), perm_ts, dilations):
        w = jnp.where(live, ev * inv, 0.0)
        w_hi = w.astype(BF16)
        w_lo = (w - w_hi.astype(F32)).astype(BF16)
        w_full = _dot(jnp.concatenate([w_hi, w_lo], axis=1), e_ref[...])
        term = w_full * _natural_rows(o_ref[0], perm_t, dil, tm)
        o = term if o is None else o + term
    z = z_ref[0].astype(F32)
    g = (o * (z * jax.nn.sigmoid(z))).astype(BF16)
    h1 = h_ref[0] + _dot(g, wout_ref[...])
    out_ref[0] = _rms_norm(_ple(h1, p_ref[0, 0], wup_ref, wgate_ref), g_ref[...])


def _dil_out(outs, stats, proj0, h, p, layer, w_out, w_up, w_gate, final_g, *, tm):
    b, s, d = h.shape
    gw = DIL_HEADS_PER_GROUP * DIL_HEAD_DIM
    pd = p.shape[-1]
    dilations = tuple(dil for _, dil in DIL_PATTERN)
    z_block = proj0.shape[-1] // gw - 1
    expand = np.zeros((2 * LANES, gw), np.float32)
    for hh in range(DIL_HEADS_PER_GROUP):
        expand[hh, DIL_HEAD_DIM * hh:DIL_HEAD_DIM * (hh + 1)] = 1.0
        expand[LANES + hh, DIL_HEAD_DIM * hh:DIL_HEAD_DIM * (hh + 1)] = 1.0
    perm_ts = [jnp.asarray(_phase_major_matrix(dil).T, BF16) for dil in dilations[1:]]
    const = lambda shape: pl.BlockSpec(shape, lambda bi, i: (0,) * len(shape))
    row = lambda w: pl.BlockSpec((1, tm, w), lambda bi, i: (bi, i, 0))
    view = lambda w: [pl.BlockSpec((1, tm // dil, dil * w), lambda bi, i: (bi, i, 0))
                      for dil in dilations]
    return pl.pallas_call(
        functools.partial(_dil_out_kernel, dilations=dilations, tm=tm),
        grid=(b, s // tm),
        in_specs=view(gw) + view(LANES) + [
            pl.BlockSpec((1, tm, gw), lambda bi, i: (bi, i, z_block)),
            row(d), pl.BlockSpec((1, 1, tm, pd), lambda bi, i: (layer, bi, i, 0)),
            const((PERM_ROWS, PERM_ROWS)), const((PERM_ROWS, PERM_ROWS)),
            const((2 * LANES, gw)), const((gw, d)), const((pd, d)), const((d, d)), const((1, d)),
        ],
        out_specs=row(d),
        out_shape=jax.ShapeDtypeStruct((b, s, d), F32),
        compiler_params=_cparams(2),
        name="dil_out",
    )(*outs, *stats, proj0, h, p, *perm_ts, jnp.asarray(expand, BF16),
      w_out.astype(BF16), w_up.astype(BF16), w_gate.astype(BF16), final_g.reshape(1, d))


def kernel(x, p, fox_norm, fox_w_in, fox_b_f, fox_w_out, dil_norm, dil_w_in, dil_w_out,
           ple_w_up, ple_w_gate, final_norm):
    s = x.shape[1]
    tm = min(512, s)
    tq = min(512, s)
    tk = min(512, s)
    assert tm == tq == tk, "the pruning table pairs fox_in tiles with query tiles and key blocks"
    kaug, qt, vt, zt, augq, kst, qst = _fox_in(x, fox_norm[0], fox_w_in[0], fox_b_f[0], tm=tm, ch=tk)
    pps = 2
    ot = _fox_attn(kaug, qt, vt, augq, _fox_prune_table(kst, qst, 2 * pps), tq=tq, tk=tk,
                   pairs_per_step=pps)
    h = _fox_out(ot, zt, x, p, 0, fox_w_out[0], ple_w_up[0], ple_w_gate[0], tm=min(2 * tm, s))

    gw = DIL_HEADS_PER_GROUP * DIL_HEAD_DIM
    n_groups = len(DIL_PATTERN)
    k_scale = DIL_HEAD_DIM ** -0.5 * LOG2E
    col = jnp.arange(dil_w_in.shape[-1]) // (n_groups * gw)
    w1 = (dil_w_in[0] * jnp.where(col == 1, k_scale, 1.0)[None, :]).astype(BF16)
    qkv = lambda g: [n_groups * part + g for part in range(3)]
    views = list(_dil_in(h, dil_norm[0], w1, [qkv(0) + [3 * n_groups], qkv(1), qkv(2)],
                         tuple(dil for _, dil in DIL_PATTERN), tm=tm, part_cols=gw))
    proj0 = views[0]
    outs, stats = [], []
    for group, (_, dilation) in enumerate(DIL_PATTERN):
        kb = min(DIL_SUB_BLOCKS, s // dilation // DIL_WINDOW_STEPS)
        phases = min(dilation, DIL_SUB_BLOCKS // kb)
        o, st = _dil_attn(views[group], group, kb=kb, phases=phases)
        outs.append(o)
        stats.append(st)
    return _dil_out(outs, stats, proj0, h, p, 1, dil_w_out[0], ple_w_up[1], ple_w_gate[1],
                    final_norm, tm=tm)
```

```python
import functools

import numpy as np
import jax
import jax.numpy as jnp
from jax import lax
from jax.experimental import pallas as pl
from jax.experimental.pallas import tpu as pltpu

F32 = jnp.float32
BF16 = jnp.bfloat16

RMS_EPS = 1e-6
FOX_HEADS = 16
FOX_HEAD_DIM = 64
FOX_PAIRS = FOX_HEADS // 2
DIL_PATTERN = ((128, 1), (512, 4), (2048, 16))
DIL_HEADS_PER_GROUP = 8
DIL_HEAD_DIM = 128
DIL_WINDOW_STEPS = 128
ALIBI_MAX_EXP = 8.0
MASK_VALUE = -1e30
LOG2E = 1.4426950408889634

LANES = 128
AUG_SLOTS_PER_HEAD = 16
QUERY_LANES = 256
PRUNE_LOG2 = 50.0
NORM_SLACK = 1.02
VMEM_LIMIT_BYTES = 56 * 1024 * 1024

NT_DIMS = (((1,), (1,)), ((), ()))
TN_DIMS = (((0,), (0,)), ((), ()))


def _cparams(n_axes):
    return pltpu.CompilerParams(
        dimension_semantics=("arbitrary",) * n_axes,
        vmem_limit_bytes=VMEM_LIMIT_BYTES,
    )


def _rms_norm(x, g):
    ms = jnp.mean(x * x, axis=-1, keepdims=True)
    return x * lax.rsqrt(ms + RMS_EPS) * g


def _log_sigmoid(x):
    return jnp.minimum(x, 0.0) - jnp.log1p(jnp.exp(-jnp.abs(x)))


def _split3(x):
    hi = x.astype(BF16).astype(F32)
    r1 = x - hi
    mid = r1.astype(BF16).astype(F32)
    lo = (r1 - mid).astype(BF16).astype(F32)
    return hi, mid, lo


def _dot(a, b):
    return jnp.dot(a, b, preferred_element_type=F32)


def _fox_in_kernel(x_ref, g_ref, wk_ref, wt_ref, wft_ref, bt_ref,
                   triu_ref, pk_ref, pqt_ref, gk_ref, gq_ref,
                   kaug_ref, qt_ref, vt_ref, zt_ref, augq_ref, kst_ref, qst_ref,
                   carryt_ref, *, tm, ch):
    i = pl.program_id(1)

    @pl.when(i == 0)
    def _():
        carryt_ref[...] = jnp.zeros_like(carryt_ref)

    hn = _rms_norm(x_ref[0], g_ref[...]).astype(BF16)
    width = FOX_HEADS * FOX_HEAD_DIM

    ft = lax.dot_general(wft_ref[...], hn, NT_DIMS, preferred_element_type=F32)
    hi, mid, lo = _split3(_log_sigmoid(ft + bt_ref[...]))
    stack = jnp.concatenate([hi, mid, lo], axis=0).astype(BF16)
    cs = _dot(stack, triu_ref[...])
    ct = cs[0:16] + cs[16:32] + cs[32:48] + carryt_ref[:, 0:1]
    carryt_ref[...] = jnp.broadcast_to(ct[:, tm - 1:tm], carryt_ref.shape)
    c = jnp.concatenate([ct, jnp.zeros((LANES - FOX_HEADS, tm), F32)], axis=0).T

    k = _dot(hn, wk_ref[...])
    nhi, nmid, nlo = _split3(c * -LOG2E)
    lane = lax.broadcasted_iota(jnp.int32, (tm, LANES), 1)
    slots = jnp.where(lane < 16, nhi,
                      jnp.where(lane < 32, pltpu.roll(nmid, 16, 1),
                                jnp.where(lane < 48, pltpu.roll(nlo, 32, 1),
                                          jnp.where(lane == 48, 1.0, 0.0))))
    kaug = _dot(slots.astype(BF16), pk_ref[...])
    kb = k.astype(BF16)
    for p in range(FOX_PAIRS):
        kaug_ref[0, :, 2 * LANES * p:2 * LANES * p + LANES] = kb[:, LANES * p:LANES * (p + 1)]
        kaug_ref[0, :, 2 * LANES * p + LANES:2 * LANES * (p + 1)] = (
            kaug[:, LANES * p:LANES * (p + 1)].astype(BF16))
    kf = kb.astype(F32)
    kn2 = jnp.max(_dot((kf * kf).astype(BF16), gk_ref[...]), axis=0, keepdims=True)
    kst_ref[0, 0] = jnp.concatenate(
        [kn2, c[0:1, :], c[tm - 1:tm, :], jnp.zeros((5, LANES), F32)], axis=0)

    rows = 512
    for c0 in range(0, 3 * width, rows):
        res = lax.dot_general(wt_ref[c0:c0 + rows, :], hn, NT_DIMS,
                              preferred_element_type=F32).astype(BF16)
        which, off = divmod(c0, width)
        if which == 0:
            qt_ref[0, off:off + rows, :] = res
        elif which == 1:
            for j in range(tm // ch):
                vt_ref[0, j, off:off + rows, :] = res[:, j * ch:(j + 1) * ch]
        else:
            zt_ref[0, off:off + rows, :] = res

    hi, mid, lo = _split3(ct * LOG2E)
    ones = jnp.where(lax.broadcasted_iota(jnp.int32, (16, tm), 0) == 0, 1.0, 0.0)
    slots_t = jnp.concatenate([hi, mid, lo, ones], axis=0).astype(BF16)
    augq_ref[0] = _dot(pqt_ref[...], slots_t).astype(BF16)
    qf = qt_ref[0].astype(F32)
    qn2 = jnp.max(_dot(gq_ref[...], (qf * qf).astype(BF16)), axis=1, keepdims=True)
    qst_ref[0, 0] = jnp.broadcast_to(qn2, (FOX_HEADS, LANES))


def _placement_matrices():
    pk = np.zeros((LANES, FOX_PAIRS * LANES), np.float32)
    pqt = np.zeros((FOX_HEADS * AUG_SLOTS_PER_HEAD, 64), np.float32)
    for h in range(FOX_HEADS):
        p, odd = divmod(h, 2)
        base = 6 * odd
        for piece in range(3):
            pk[48, LANES * p + base + piece] = 1.0
            pk[16 * piece + h, LANES * p + base + 3 + piece] = 1.0
            pqt[AUG_SLOTS_PER_HEAD * h + base + piece, 16 * piece + h] = 1.0
            pqt[AUG_SLOTS_PER_HEAD * h + base + 3 + piece, 48] = 1.0
    return jnp.asarray(pk, BF16), jnp.asarray(pqt, BF16)


def _fox_in(x, norm_g, w_in, b_f, *, tm, ch):
    b, s, d = x.shape
    width = FOX_HEADS * FOX_HEAD_DIM
    wq = w_in[:, :width] * (FOX_HEAD_DIM ** -0.5 * LOG2E)
    wk = w_in[:, width:2 * width]
    wv = w_in[:, 2 * width:3 * width]
    wz = w_in[:, 3 * width:4 * width]
    wf = w_in[:, 4 * width:]
    wt = jnp.concatenate([wq, wv, wz], axis=1).astype(BF16).T
    wft = wf.T.astype(BF16)
    bt = b_f.reshape(FOX_HEADS, 1)
    r = np.arange(tm)
    triu = jnp.asarray(r[:, None] <= r[None, :], BF16)
    pk, pqt = _placement_matrices()
    head_of = np.arange(width) // FOX_HEAD_DIM
    gk = jnp.asarray(head_of[:, None] == np.arange(LANES)[None, :], BF16)
    gq = jnp.asarray(np.arange(FOX_HEADS)[:, None] == head_of[None, :], BF16)

    const = lambda shape: pl.BlockSpec(shape, lambda bi, i: (0,) * len(shape))
    return pl.pallas_call(
        functools.partial(_fox_in_kernel, tm=tm, ch=ch),
        grid=(b, s // tm),
        in_specs=[
            pl.BlockSpec((1, tm, d), lambda bi, i: (bi, i, 0)),
            const((1, d)),
            const(wk.shape), const(wt.shape), const(wft.shape),
            const((FOX_HEADS, 1)),
            const((tm, tm)),
            const(pk.shape), const(pqt.shape), const(gk.shape), const(gq.shape),
        ],
        out_specs=[
            pl.BlockSpec((1, tm, 2 * width), lambda bi, i: (bi, i, 0)),
            pl.BlockSpec((1, width, tm), lambda bi, i: (bi, 0, i)),
            pl.BlockSpec((1, tm // ch, width, ch), lambda bi, i: (bi, i, 0, 0)),
            pl.BlockSpec((1, width, tm), lambda bi, i: (bi, 0, i)),
            pl.BlockSpec((1, FOX_HEADS * AUG_SLOTS_PER_HEAD, tm), lambda bi, i: (bi, 0, i)),
            pl.BlockSpec((1, 1, 8, LANES), lambda bi, i: (bi, i, 0, 0)),
            pl.BlockSpec((1, 1, FOX_HEADS, LANES), lambda bi, i: (bi, i, 0, 0)),
        ],
        out_shape=[
            jax.ShapeDtypeStruct((b, s, 2 * width), BF16),
            jax.ShapeDtypeStruct((b, width, s), BF16),
            jax.ShapeDtypeStruct((b, s // ch, width, ch), BF16),
            jax.ShapeDtypeStruct((b, width, s), BF16),
            jax.ShapeDtypeStruct((b, FOX_HEADS * AUG_SLOTS_PER_HEAD, s), BF16),
            jax.ShapeDtypeStruct((b, s // tm, 8, LANES), F32),
            jax.ShapeDtypeStruct((b, s // tm, FOX_HEADS, LANES), F32),
        ],
        scratch_shapes=[pltpu.VMEM((FOX_HEADS, LANES), F32)],
        compiler_params=_cparams(2),
        name="fox_in",
    )(x, norm_g.reshape(1, d), wk.astype(BF16), wt, wft, bt, triu, pk, pqt, gk, gq)


def _fox_attn_kernel(kstart_ref, kaug_ref, vt_ref, qt_ref, augq_ref, ot_ref,
                     s_scr, p_scr, *, tq, tk, pairs_per_step):
    qi = pl.program_id(2)
    hd = FOX_HEAD_DIM
    zeros_hd = jnp.zeros((hd, tq), BF16)
    zeros_tail = jnp.zeros((LANES - AUG_SLOTS_PER_HEAD, tq), BF16)
    ones_block = lambda n: jnp.where(lax.broadcasted_iota(jnp.int32, (16, n), 0) == 0, 1.0, 0.0).astype(BF16)
    ones_rows = ones_block(tk)
    n_blocks = (qi * tq) // tk + 1
    k_first = kstart_ref[pl.program_id(0), pl.program_id(1), qi]

    heads = [(u, h) for u in range(pairs_per_step) for h in range(2)]
    w_head = {}
    for u, h in heads:
        q_rows = qt_ref[0, LANES * u + hd * h:LANES * u + hd * (h + 1), :]
        aug_rows = augq_ref[0, 2 * AUG_SLOTS_PER_HEAD * u + AUG_SLOTS_PER_HEAD * h:
                            2 * AUG_SLOTS_PER_HEAD * u + AUG_SLOTS_PER_HEAD * (h + 1), :]
        q_part = [q_rows, zeros_hd] if h == 0 else [zeros_hd, q_rows]
        w_head[u, h] = jnp.concatenate(q_part + [aug_rows, zeros_tail], axis=0)
    chains = [(u, h, sub) for u, h in heads for sub in range(tq // QUERY_LANES)]

    def issue_scores(kj, slot):
        ks = pl.multiple_of(kj * tk, tk)
        block_max = []
        for c, (u, h, sub) in enumerate(chains):
            kb = kaug_ref[0, pl.ds(ks, tk), 2 * LANES * u:2 * LANES * (u + 1)]
            s = _dot(kb, w_head[u, h][:, QUERY_LANES * sub:QUERY_LANES * (sub + 1)])
            s_scr[slot, c] = s
            block_max.append(jnp.max(s, axis=0, keepdims=True))
        return tuple(block_max)

    tri_key = lax.broadcasted_iota(jnp.int32, (QUERY_LANES, QUERY_LANES), 0)
    tri_qry = lax.broadcasted_iota(jnp.int32, (QUERY_LANES, QUERY_LANES), 1)
    causal = jnp.where(tri_key <= tri_qry, 0.0, MASK_VALUE)

    def pv_block(kj, c):
        u, h, _ = chains[c]
        v_rows = vt_ref[0, kj, LANES * u + hd * h:LANES * u + hd * (h + 1), :]
        return _dot(jnp.concatenate([v_rows, ones_rows], axis=0), p_scr[c])

    def step(kj, slot, state, has_prev=True):
        carry, block_max = state
        next_max = issue_scores(kj + 1, 1 - slot)
        out = []
        for c in range(len(chains)):
            m_old, alpha_prev, acc = carry[c]
            if has_prev:
                acc = acc * alpha_prev + pv_block(kj - 1, c)
            m_new = jnp.maximum(m_old, block_max[c])
            p_scr[c] = jnp.exp2(s_scr[slot, c] - m_new).astype(BF16)
            out.append((m_new, jnp.exp2(m_old - m_new), acc))
        return tuple(out), next_max

    def diagonal(kj, slot, state):
        carry, _ = state
        for c, (u, h, sub) in enumerate(chains):
            m_old, alpha_prev, acc = carry[c]
            acc = acc * alpha_prev + pv_block(jnp.maximum(kj - 1, 0), c)
            lo = QUERY_LANES * sub
            vis = lo + QUERY_LANES
            mid = s_scr[slot, c, lo:vis, :] + causal
            m_new = jnp.maximum(m_old, jnp.max(mid, axis=0, keepdims=True))
            if lo:
                top = s_scr[slot, c, 0:lo, :]
                m_new = jnp.maximum(m_new, jnp.max(top, axis=0, keepdims=True))
                p = jnp.concatenate([jnp.exp2(top - m_new).astype(BF16),
                                     jnp.exp2(mid - m_new).astype(BF16)], axis=0)
            else:
                p = jnp.exp2(mid - m_new).astype(BF16)
            v_rows = vt_ref[0, kj, LANES * u + hd * h:LANES * u + hd * (h + 1), 0:vis]
            pv = _dot(jnp.concatenate([v_rows, ones_block(vis)], axis=0), p)
            acc = acc * jnp.exp2(m_old - m_new) + pv
            ot_ref[0, LANES * u + hd * h:LANES * u + hd * (h + 1),
                   QUERY_LANES * sub:QUERY_LANES * (sub + 1)] = (acc[0:hd] / acc[hd:hd + 1]).astype(BF16)

    row = lambda value: jnp.full((1, QUERY_LANES), value, F32)
    carry = tuple((row(MASK_VALUE), row(1.0), jnp.zeros((hd + 16, QUERY_LANES), F32)) for _ in chains)
    state = (carry, tuple(row(0.0) for _ in chains))
    n_unmasked = n_blocks - 1 - k_first
    has_unmasked = jnp.minimum(n_unmasked, 1)

    def first(_, st):
        return step(k_first, 0, (st[0], issue_scores(k_first, 0)), has_prev=False)

    def only_diagonal(_, st):
        p_scr[...] = jnp.zeros_like(p_scr)
        return st[0], issue_scores(k_first, 0)

    def pair(i, st):
        kj = k_first + 1 + 2 * i
        return step(kj + 1, 0, step(kj, 1, st))

    state = lax.fori_loop(0, has_unmasked, first, state)
    state = lax.fori_loop(0, 1 - has_unmasked, only_diagonal, state)
    n_rest = jnp.maximum(n_unmasked - 1, 0)
    state = lax.fori_loop(0, n_rest // 2, pair, state)
    state = lax.fori_loop(0, lax.rem(n_rest, 2), lambda _, st: step(n_blocks - 2, 1, st), state)
    diagonal(n_blocks - 1, lax.rem(n_unmasked, 2), state)


def _fox_prune_table(kst, qst, heads_per_step):
    kn = jnp.sqrt(kst[:, :, 0, :FOX_HEADS]) * NORM_SLACK
    qn = jnp.sqrt(qst[:, :, :, 0]) * NORM_SLACK
    c_first = kst[:, :, 1, :FOX_HEADS] * LOG2E
    c_last = kst[:, :, 2, :FOX_HEADS] * LOG2E
    bound = (qn[:, :, None, :] * kn[:, None, :, :] + c_first[:, :, None, :] - c_last[:, None, :, :]
             + (qn * kn)[:, :, None, :])
    n = kst.shape[1]
    earlier = jnp.arange(n)[None, :, None] < jnp.arange(n)[:, None, None]
    skip = (bound < -PRUNE_LOG2) & earlier[None]
    block = jnp.arange(n, dtype=jnp.int32)[None, None, :, None]
    lead = jnp.min(jnp.where(skip, n, block), axis=2)
    lead = jnp.min(lead.reshape(lead.shape[0], n, FOX_HEADS // heads_per_step, heads_per_step), axis=3)
    return jnp.transpose(lead, (0, 2, 1)).astype(jnp.int32)


def _fox_attn(kaug, qt, vt, augq, kstart, *, tq, tk, pairs_per_step):
    b, s, _ = kaug.shape
    width = FOX_HEADS * FOX_HEAD_DIM
    nk = s // tk
    pps = pairs_per_step
    n_chains = 2 * pps * (tq // QUERY_LANES)
    grid_spec = pltpu.PrefetchScalarGridSpec(
        num_scalar_prefetch=1,
        grid=(b, FOX_PAIRS // pps, s // tq),
        in_specs=[
            pl.BlockSpec((1, s, 2 * LANES * pps), lambda bi, p, qi, ks: (bi, 0, p)),
            pl.BlockSpec((1, nk, LANES * pps, tk), lambda bi, p, qi, ks: (bi, 0, p, 0)),
            pl.BlockSpec((1, LANES * pps, tq), lambda bi, p, qi, ks: (bi, p, qi)),
            pl.BlockSpec((1, 2 * AUG_SLOTS_PER_HEAD * pps, tq), lambda bi, p, qi, ks: (bi, p, qi)),
        ],
        out_specs=pl.BlockSpec((1, LANES * pps, tq), lambda bi, p, qi, ks: (bi, p, qi)),
        scratch_shapes=[pltpu.VMEM((2, n_chains, tk, QUERY_LANES), F32),
                        pltpu.VMEM((n_chains, tk, QUERY_LANES), BF16)],
    )
    return pl.pallas_call(
        functools.partial(_fox_attn_kernel, tq=tq, tk=tk, pairs_per_step=pps),
        grid_spec=grid_spec,
        out_shape=jax.ShapeDtypeStruct((b, width, s), BF16),
        compiler_params=_cparams(3),
        name="fox_attn",
    )(kstart, kaug, vt, qt, augq)


def _ple(h1, p, wup_ref, wgate_ref):
    gate = jax.nn.sigmoid(_dot(h1.astype(BF16), wgate_ref[...]))
    up = _dot(p.astype(BF16), wup_ref[...])
    return h1 + up * gate


def _fox_out_kernel(ot_ref, zt_ref, x_ref, p_ref, wout_ref, wup_ref, wgate_ref, h_ref):
    z = zt_ref[0].astype(F32)
    gt = (ot_ref[0].astype(F32) * (z * jax.nn.sigmoid(z))).astype(BF16)
    y = lax.dot_general(gt, wout_ref[...], TN_DIMS, preferred_element_type=F32)
    h_ref[0] = _ple(x_ref[0] + y, p_ref[0, 0], wup_ref, wgate_ref)


def _fox_out(ot, zt, x, p, layer, w_out, w_up, w_gate, *, tm):
    b, s, d = x.shape
    width = ot.shape[1]
    pd = p.shape[-1]
    const = lambda shape: pl.BlockSpec(shape, lambda bi, i: (0,) * len(shape))
    return pl.pallas_call(
        _fox_out_kernel,
        grid=(b, s // tm),
        in_specs=[
            pl.BlockSpec((1, width, tm), lambda bi, i: (bi, 0, i)),
            pl.BlockSpec((1, width, tm), lambda bi, i: (bi, 0, i)),
            pl.BlockSpec((1, tm, d), lambda bi, i: (bi, i, 0)),
            pl.BlockSpec((1, 1, tm, pd), lambda bi, i: (layer, bi, i, 0)),
            const((width, d)), const((pd, d)), const((d, d)),
        ],
        out_specs=pl.BlockSpec((1, tm, d), lambda bi, i: (bi, i, 0)),
        out_shape=jax.ShapeDtypeStruct((b, s, d), F32),
        compiler_params=_cparams(2),
        name="fox_out",
    )(ot, zt, x, p, w_out.astype(BF16), w_up.astype(BF16), w_gate.astype(BF16))


PERM_ROWS = 256
DIL_SUB_BLOCKS = 8


def _phase_major_matrix(dilation):
    n = PERM_ROWS // dilation
    p = np.zeros((PERM_ROWS, PERM_ROWS), np.float32)
    for r in range(dilation):
        for i in range(n):
            p[r * n + i, dilation * i + r] = 1.0
    return p


def _dil_in_kernel(h_ref, g_ref, *rest, dilations, parts, tm):
    n_w = sum(parts)
    n_perm = sum(d > 1 for d in dilations)
    w_refs, perm_refs, out_refs = rest[:n_w], list(rest[n_w:n_w + n_perm]), rest[n_w + n_perm:]
    hn = _rms_norm(h_ref[0], g_ref[...]).astype(BF16)
    first = 0
    for dil, n_parts, o_ref in zip(dilations, parts, out_refs):
        group_w = w_refs[first:first + n_parts]
        first += n_parts
        pc = group_w[0].shape[1]
        if dil == 1:
            for part, w_ref in enumerate(group_w):
                o_ref[0, :, pc * part:pc * (part + 1)] = _dot(hn, w_ref[...]).astype(BF16)
            continue
        perm = perm_refs.pop(0)[...]
        n = PERM_ROWS // dil
        for c in range(tm // PERM_ROWS):
            rows = slice(PERM_ROWS * c, PERM_ROWS * (c + 1))
            hp = _dot(perm, hn[rows]).astype(BF16)
            for part, w_ref in enumerate(group_w):
                res = _dot(hp, w_ref[...]).astype(BF16)
                for r in range(dil):
                    dst = pc * (part * dil + r)
                    o_ref[0, n * c:n * (c + 1), dst:dst + pc] = res[n * r:n * (r + 1)]


def _dil_in(h, norm_g, w, groups, dilations, *, tm, part_cols):
    b, s, d = h.shape
    parts = tuple(len(blocks) for blocks in groups)
    perms = [jnp.asarray(_phase_major_matrix(dil), BF16) for dil in dilations if dil > 1]
    const = lambda shape: pl.BlockSpec(shape, lambda bi, i: (0,) * len(shape))
    w_specs = [pl.BlockSpec((d, part_cols), lambda bi, i, cb=cb: (0, cb),
                            pipeline_mode=pl.Buffered(1))
               for blocks in groups for cb in blocks]
    widths = [dil * n_parts * part_cols for dil, n_parts in zip(dilations, parts)]
    return pl.pallas_call(
        functools.partial(_dil_in_kernel, dilations=dilations, parts=parts, tm=tm),
        grid=(b, s // tm),
        in_specs=[pl.BlockSpec((1, tm, d), lambda bi, i: (bi, i, 0)), const((1, d))]
                 + w_specs + [const(p.shape) for p in perms],
        out_specs=[pl.BlockSpec((1, tm // dil, width), lambda bi, i: (bi, i, 0))
                   for dil, width in zip(dilations, widths)],
        out_shape=[jax.ShapeDtypeStruct((b, s // dil, width), BF16)
                   for dil, width in zip(dilations, widths)],
        compiler_params=pltpu.CompilerParams(
            dimension_semantics=("arbitrary",) * 2, vmem_limit_bytes=VMEM_LIMIT_BYTES,
            allow_input_fusion=[False, False] + [True] * sum(parts) + [False] * len(perms)),
        name="dil_in_" + "_".join(str(dil) for dil in dilations),
    )(h, norm_g.reshape(1, d), *([w] * sum(parts)), *perms)


def _dil_attn_kernel(q_ref, kc_ref, vc_ref, kp_ref, vp_ref, o_ref, st_ref,
                     kcat, vcat, bias_scr, *, dilation, kb, phases, neg_slopes):
    n = pl.program_id(2)
    nw = DIL_WINDOW_STEPS
    hd = DIL_HEAD_DIM
    gw = DIL_HEADS_PER_GROUP * hd

    @pl.when((pl.program_id(0) == 0) & (pl.program_id(1) == 0) & (n == 0))
    def _():
        row = lax.broadcasted_iota(jnp.int32, (nw, 2 * nw), 0)
        col = lax.broadcasted_iota(jnp.int32, (nw, 2 * nw), 1)
        dist = nw + row - col
        band = (dist >= 0) & (dist <= nw)
        dist_f = (dist * dilation).astype(F32)
        for h in range(DIL_HEADS_PER_GROUP):
            bias_scr[h] = jnp.where(band, neg_slopes[h] * dist_f * LOG2E, MASK_VALUE)

    for ph in range(phases):
        cols = slice(gw * ph, gw * (ph + 1))
        kcat[ph, 0:nw] = kp_ref[0, :, cols]
        kcat[ph, nw:nw * (kb + 1)] = kc_ref[0, :, cols]
        vcat[ph, 0:nw] = vp_ref[0, :, cols]
        vcat[ph, nw:nw * (kb + 1)] = vc_ref[0, :, cols]
    lane = lax.broadcasted_iota(jnp.int32, (nw, LANES), 1)
    col = lax.broadcasted_iota(jnp.int32, (nw, 2 * nw), 1)
    no_history = jnp.where(col < jnp.where(n == 0, nw, 0), MASK_VALUE, 0.0)

    heads = [slice(hd * h, hd * (h + 1)) for h in range(DIL_HEADS_PER_GROUP)]
    problems = [(ph, jb) for ph in range(phases) for jb in range(kb)]
    scores = [[lax.dot_general(q_ref[0, nw * jb:nw * (jb + 1), gw * ph + hd * h:gw * ph + hd * (h + 1)],
                               kcat[ph, nw * jb:nw * (jb + 2), hs], NT_DIMS,
                               preferred_element_type=F32) for h, hs in enumerate(heads)]
              for ph, jb in problems]
    for i, (ph, jb) in enumerate(problems):
        rows = slice(nw * jb, nw * (jb + 1))
        stat_tile = jnp.zeros((nw, LANES), F32)
        for h, hs in enumerate(heads):
            t = scores[i][h] + bias_scr[h]
            if jb == 0:
                t = t + no_history
            m = jnp.max(t, axis=1, keepdims=True)
            p = jnp.exp2(t - m)
            l = jnp.sum(p, axis=1, keepdims=True)
            o_ref[0, rows, gw * ph + hd * h:gw * ph + hd * (h + 1)] = _dot(
                p.astype(BF16), vcat[ph, nw * jb:nw * (jb + 2), hs]).astype(BF16)
            stat_tile = jnp.where(lane == h, m, stat_tile)
            stat_tile = jnp.where(lane == h + DIL_HEADS_PER_GROUP, l, stat_tile)
        st_ref[0, rows, LANES * ph:LANES * (ph + 1)] = stat_tile


def _alibi_neg_slopes(group):
    n = len(DIL_PATTERN) * DIL_HEADS_PER_GROUP
    k = np.arange(1, n + 1, dtype=np.float32)
    slopes = np.float32(2.0) ** (np.float32(-ALIBI_MAX_EXP) * k / np.float32(n))
    lo = group * DIL_HEADS_PER_GROUP
    return tuple(float(-v) for v in slopes[lo:lo + DIL_HEADS_PER_GROUP])


def _dil_attn(view, group, *, kb, phases):
    _, dilation = DIL_PATTERN[group]
    b, length, _ = view.shape
    nw = DIL_WINDOW_STEPS
    gw = DIL_HEADS_PER_GROUP * DIL_HEAD_DIM
    rows = nw * kb
    steps = dilation // phases

    def cur(part):
        return pl.BlockSpec((1, rows, phases * gw), lambda bi, r, n: (bi, n, part * steps + r))

    def prev(part):
        return pl.BlockSpec((1, nw, phases * gw),
                            lambda bi, r, n: (bi, jnp.maximum(n * kb - 1, 0), part * steps + r))

    o, st = pl.pallas_call(
        functools.partial(_dil_attn_kernel, dilation=dilation, kb=kb, phases=phases,
                          neg_slopes=_alibi_neg_slopes(group)),
        grid=(b, steps, length // rows),
        in_specs=[cur(0), cur(1), cur(2), prev(1), prev(2)],
        out_specs=[
            pl.BlockSpec((1, rows, phases * gw), lambda bi, r, n: (bi, n, r)),
            pl.BlockSpec((1, rows, phases * LANES), lambda bi, r, n: (bi, n, r)),
        ],
        out_shape=[
            jax.ShapeDtypeStruct((b, length, dilation * gw), BF16),
            jax.ShapeDtypeStruct((b, length, dilation * LANES), F32),
        ],
        scratch_shapes=[pltpu.VMEM((phases, nw * (kb + 1), gw), BF16),
                        pltpu.VMEM((phases, nw * (kb + 1), gw), BF16),
                        pltpu.VMEM((DIL_HEADS_PER_GROUP, nw, 2 * nw), F32)],
        compiler_params=_cparams(3),
        name=f"dil_attn_g{group}",
    )(view, view, view, view, view)
    return o, st


def _natural_rows(blk, perm_t, dilation, tm):
    cols = blk.shape[1] // dilation
    if dilation == 1:
        return blk
    n = PERM_ROWS // dilation
    chunks = []
    for c in range(tm // PERM_ROWS):
        phase_major = jnp.concatenate(
            [blk[n * c:n * (c + 1), cols * r:cols * (r + 1)] for r in range(dilation)], axis=0)
        chunks.append(_dot(perm_t, phase_major))
    return jnp.concatenate(chunks, axis=0)


def _dil_out_kernel(o0_ref, o1_ref, o2_ref, s0_ref, s1_ref, s2_ref, z_ref, h_ref, p_ref,
                    pt1_ref, pt2_ref, e_ref, wout_ref, wup_ref, wgate_ref, g_ref,
                    out_ref, *, dilations, tm):
    perm_ts = (None, pt1_ref[...], pt2_ref[...])
    n_heads = DIL_HEADS_PER_GROUP
    row_max, row_sum = [], []
    for st_ref, perm_t, dil in zip((s0_ref, s1_ref, s2_ref), perm_ts, dilations):
        if dil == 1:
            st = st_ref[0]
        else:
            hi, mid, lo = (_natural_rows(piece.astype(BF16), perm_t, dil, tm)
                           for piece in _split3(st_ref[0]))
            st = hi + mid + lo
        row_max.append(st)
        row_sum.append(pltpu.roll(st, LANES - n_heads, 1))
    m = jnp.maximum(jnp.maximum(row_max[0], row_max[1]), row_max[2])
    e = [jnp.exp2(v - m) for v in row_max]
    inv = 1.0 / (e[0] * row_sum[0] + e[1] * row_sum[1] + e[2] * row_sum[2])
    live = lax.broadcasted_iota(jnp.int32, (tm, LANES), 1) < n_heads
    o = None
    for ev, o_ref, perm_t, dil in zip(e, (o0_ref, o1_ref, o2_ref), perm_ts, dilations):
        w = jnp.where(live, ev * inv, 0.0)
        w_hi = w.astype(BF16)
        w_lo = (w - w_hi.astype(F32)).astype(BF16)
        w_full = _dot(jnp.concatenate([w_hi, w_lo], axis=1), e_ref[...])
        term = w_full * _natural_rows(o_ref[0], perm_t, dil, tm)
        o = term if o is None else o + term
    z = z_ref[0].astype(F32)
    g = (o * (z * jax.nn.sigmoid(z))).astype(BF16)
    h1 = h_ref[0] + _dot(g, wout_ref[...])
    out_ref[0] = _rms_norm(_ple(h1, p_ref[0, 0], wup_ref, wgate_ref), g_ref[...])


def _dil_out(outs, stats, proj0, h, p, layer, w_out, w_up, w_gate, final_g, *, tm):
    b, s, d = h.shape
    gw = DIL_HEADS_PER_GROUP * DIL_HEAD_DIM
    pd = p.shape[-1]
    dilations = tuple(dil for _, dil in DIL_PATTERN)
    z_block = proj0.shape[-1] // gw - 1
    expand = np.zeros((2 * LANES, gw), np.float32)
    for hh in range(DIL_HEADS_PER_GROUP):
        expand[hh, DIL_HEAD_DIM * hh:DIL_HEAD_DIM * (hh + 1)] = 1.0
        expand[LANES + hh, DIL_HEAD_DIM * hh:DIL_HEAD_DIM * (hh + 1)] = 1.0
    perm_ts = [jnp.asarray(_phase_major_matrix(dil).T, BF16) for dil in dilations[1:]]
    const = lambda shape: pl.BlockSpec(shape, lambda bi, i: (0,) * len(shape))
    row = lambda w: pl.BlockSpec((1, tm, w), lambda bi, i: (bi, i, 0))
    view = lambda w: [pl.BlockSpec((1, tm // dil, dil * w), lambda bi, i: (bi, i, 0))
                      for dil in dilations]
    return pl.pallas_call(
        functools.partial(_dil_out_kernel, dilations=dilations, tm=tm),
        grid=(b, s // tm),
        in_specs=view(gw) + view(LANES) + [
            pl.BlockSpec((1, tm, gw), lambda bi, i: (bi, i, z_block)),
            row(d), pl.BlockSpec((1, 1, tm, pd), lambda bi, i: (layer, bi, i, 0)),
            const((PERM_ROWS, PERM_ROWS)), const((PERM_ROWS, PERM_ROWS)),
            const((2 * LANES, gw)), const((gw, d)), const((pd, d)), const((d, d)), const((1, d)),
        ],
        out_specs=row(d),
        out_shape=jax.ShapeDtypeStruct((b, s, d), F32),
        compiler_params=_cparams(2),
        name="dil_out",
    )(*outs, *stats, proj0, h, p, *perm_ts, jnp.asarray(expand, BF16),
      w_out.astype(BF16), w_up.astype(BF16), w_gate.astype(BF16), final_g.reshape(1, d))


def kernel(x, p, fox_norm, fox_w_in, fox_b_f, fox_w_out, dil_norm, dil_w_in, dil_w_out,
           ple_w_up, ple_w_gate, final_norm):
    s = x.shape[1]
    tm = min(512, s)
    tq = min(512, s)
    tk = min(512, s)
    assert tm == tq == tk, "the pruning table pairs fox_in tiles with query tiles and key blocks"
    kaug, qt, vt, zt, augq, kst, qst = _fox_in(x, fox_norm[0], fox_w_in[0], fox_b_f[0], tm=tm, ch=tk)
    pps = 2
    ot = _fox_attn(kaug, qt, vt, augq, _fox_prune_table(kst, qst, 2 * pps), tq=tq, tk=tk,
                   pairs_per_step=pps)
    h = _fox_out(ot, zt, x, p, 0, fox_w_out[0], ple_w_up[0], ple_w_gate[0], tm=min(2 * tm, s))

    gw = DIL_HEADS_PER_GROUP * DIL_HEAD_DIM
    n_groups = len(DIL_PATTERN)
    k_scale = DIL_HEAD_DIM ** -0.5 * LOG2E
    col = jnp.arange(dil_w_in.shape[-1]) // (n_groups * gw)
    w1 = (dil_w_in[0] * jnp.where(col == 1, k_scale, 1.0)[None, :]).astype(BF16)
    qkv = lambda g: [n_groups * part + g for part in range(3)]
    views = list(_dil_in(h, dil_norm[0], w1, [qkv(0) + [3 * n_groups], qkv(1), qkv(2)],
                         tuple(dil for _, dil in DIL_PATTERN), tm=tm, part_cols=gw))
    proj0 = views[0]
    outs, stats = [], []
    for group, (_, dilation) in enumerate(DIL_PATTERN):
        kb = min(DIL_SUB_BLOCKS, s // dilation // DIL_WINDOW_STEPS)
        phases = min(dilation, DIL_SUB_BLOCKS // kb)
        o, st = _dil_attn(views[group], group, kb=kb, phases=phases)
        outs.append(o)
        stats.append(st)
    return _dil_out(outs, stats, proj0, h, p, 1, dil_w_out[0], ple_w_up[1], ple_w_gate[1],
                    final_norm, tm=tm)
```

```python
import functools

import numpy as np
import jax
import jax.numpy as jnp
from jax import lax
from jax.experimental import pallas as pl
from jax.experimental.pallas import tpu as pltpu

F32 = jnp.float32
BF16 = jnp.bfloat16

RMS_EPS = 1e-6
FOX_HEADS = 16
FOX_HEAD_DIM = 64
FOX_PAIRS = FOX_HEADS // 2
DIL_PATTERN = ((128, 1), (512, 4), (2048, 16))
DIL_HEADS_PER_GROUP = 8
DIL_HEAD_DIM = 128
DIL_WINDOW_STEPS = 128
ALIBI_MAX_EXP = 8.0
MASK_VALUE = -1e30
LOG2E = 1.4426950408889634

LANES = 128
AUG_SLOTS_PER_HEAD = 16
QUERY_LANES = 256
PRUNE_LOG2 = 50.0
NORM_SLACK = 1.02
VMEM_LIMIT_BYTES = 56 * 1024 * 1024
SMALL_VMEM_LIMIT_BYTES = 44 * 1024 * 1024

NT_DIMS = (((1,), (1,)), ((), ()))
TN_DIMS = (((0,), (0,)), ((), ()))


def _cparams(n_axes, vmem_limit_bytes=VMEM_LIMIT_BYTES):
    return pltpu.CompilerParams(
        dimension_semantics=("arbitrary",) * n_axes,
        vmem_limit_bytes=vmem_limit_bytes,
    )


def _rms_norm(x, g):
    ms = jnp.mean(x * x, axis=-1, keepdims=True)
    return x * lax.rsqrt(ms + RMS_EPS) * g


def _log_sigmoid(x):
    return jnp.minimum(x, 0.0) - jnp.log1p(jnp.exp(-jnp.abs(x)))


def _split3(x):
    hi = x.astype(BF16).astype(F32)
    r1 = x - hi
    mid = r1.astype(BF16).astype(F32)
    lo = (r1 - mid).astype(BF16).astype(F32)
    return hi, mid, lo


def _dot(a, b):
    return jnp.dot(a, b, preferred_element_type=F32)


def _fox_in_kernel(x_ref, g_ref, wk_ref, wt_ref, wft_ref, bt_ref,
                   triu_ref, pk_ref, pqt_ref, gk_ref, gq_ref,
                   kaug_ref, qt_ref, vt_ref, zt_ref, augq_ref, kst_ref, qst_ref,
                   carryt_ref, *, tm, ch):
    i = pl.program_id(1)

    @pl.when(i == 0)
    def _():
        carryt_ref[...] = jnp.zeros_like(carryt_ref)

    hn = _rms_norm(x_ref[0], g_ref[...]).astype(BF16)
    width = FOX_HEADS * FOX_HEAD_DIM

    ft = lax.dot_general(wft_ref[...], hn, NT_DIMS, preferred_element_type=F32)
    hi, mid, lo = _split3(_log_sigmoid(ft + bt_ref[...]))
    stack = jnp.concatenate([hi, mid, lo], axis=0).astype(BF16)
    cs = _dot(stack, triu_ref[...])
    ct = cs[0:16] + cs[16:32] + cs[32:48] + carryt_ref[:, 0:1]
    carryt_ref[...] = jnp.broadcast_to(ct[:, tm - 1:tm], carryt_ref.shape)
    c = jnp.concatenate([ct, jnp.zeros((LANES - FOX_HEADS, tm), F32)], axis=0).T

    k = _dot(hn, wk_ref[...])
    nhi, nmid, nlo = _split3(c * -LOG2E)
    lane = lax.broadcasted_iota(jnp.int32, (tm, LANES), 1)
    slots = jnp.where(lane < 16, nhi,
                      jnp.where(lane < 32, pltpu.roll(nmid, 16, 1),
                                jnp.where(lane < 48, pltpu.roll(nlo, 32, 1),
                                          jnp.where(lane == 48, 1.0, 0.0))))
    kaug = _dot(slots.astype(BF16), pk_ref[...])
    kb = k.astype(BF16)
    for p in range(FOX_PAIRS):
        kaug_ref[0, :, 2 * LANES * p:2 * LANES * p + LANES] = kb[:, LANES * p:LANES * (p + 1)]
        kaug_ref[0, :, 2 * LANES * p + LANES:2 * LANES * (p + 1)] = (
            kaug[:, LANES * p:LANES * (p + 1)].astype(BF16))
    kf = kb.astype(F32)
    kn2 = jnp.max(_dot((kf * kf).astype(BF16), gk_ref[...]), axis=0, keepdims=True)
    kst_ref[0, 0] = jnp.concatenate(
        [kn2, c[0:1, :], c[tm - 1:tm, :], jnp.zeros((5, LANES), F32)], axis=0)

    rows = 512
    for c0 in range(0, 3 * width, rows):
        res = lax.dot_general(wt_ref[c0:c0 + rows, :], hn, NT_DIMS,
                              preferred_element_type=F32).astype(BF16)
        which, off = divmod(c0, width)
        if which == 0:
            qt_ref[0, off:off + rows, :] = res
        elif which == 1:
            for j in range(tm // ch):
                vt_ref[0, j, off:off + rows, :] = res[:, j * ch:(j + 1) * ch]
        else:
            zt_ref[0, off:off + rows, :] = res

    hi, mid, lo = _split3(ct * LOG2E)
    ones = jnp.where(lax.broadcasted_iota(jnp.int32, (16, tm), 0) == 0, 1.0, 0.0)
    slots_t = jnp.concatenate([hi, mid, lo, ones], axis=0).astype(BF16)
    augq_ref[0] = _dot(pqt_ref[...], slots_t).astype(BF16)
    qf = qt_ref[0].astype(F32)
    qn2 = jnp.max(_dot(gq_ref[...], (qf * qf).astype(BF16)), axis=1, keepdims=True)
    qst_ref[0, 0] = jnp.broadcast_to(qn2, (FOX_HEADS, LANES))


def _placement_matrices():
    pk = np.zeros((LANES, FOX_PAIRS * LANES), np.float32)
    pqt = np.zeros((FOX_HEADS * AUG_SLOTS_PER_HEAD, 64), np.float32)
    for h in range(FOX_HEADS):
        p, odd = divmod(h, 2)
        base = 6 * odd
        for piece in range(3):
            pk[48, LANES * p + base + piece] = 1.0
            pk[16 * piece + h, LANES * p + base + 3 + piece] = 1.0
            pqt[AUG_SLOTS_PER_HEAD * h + base + piece, 16 * piece + h] = 1.0
            pqt[AUG_SLOTS_PER_HEAD * h + base + 3 + piece, 48] = 1.0
    return jnp.asarray(pk, BF16), jnp.asarray(pqt, BF16)


def _fox_in(x, norm_g, w_in, b_f, *, tm, ch):
    b, s, d = x.shape
    width = FOX_HEADS * FOX_HEAD_DIM
    wq = w_in[:, :width] * (FOX_HEAD_DIM ** -0.5 * LOG2E)
    wk = w_in[:, width:2 * width]
    wv = w_in[:, 2 * width:3 * width]
    wz = w_in[:, 3 * width:4 * width]
    wf = w_in[:, 4 * width:]
    wt = jnp.concatenate([wq, wv, wz], axis=1).astype(BF16).T
    wft = wf.T.astype(BF16)
    bt = b_f.reshape(FOX_HEADS, 1)
    r = np.arange(tm)
    triu = jnp.asarray(r[:, None] <= r[None, :], BF16)
    pk, pqt = _placement_matrices()
    head_of = np.arange(width) // FOX_HEAD_DIM
    gk = jnp.asarray(head_of[:, None] == np.arange(LANES)[None, :], BF16)
    gq = jnp.asarray(np.arange(FOX_HEADS)[:, None] == head_of[None, :], BF16)

    const = lambda shape: pl.BlockSpec(shape, lambda bi, i: (0,) * len(shape))
    return pl.pallas_call(
        functools.partial(_fox_in_kernel, tm=tm, ch=ch),
        grid=(b, s // tm),
        in_specs=[
            pl.BlockSpec((1, tm, d), lambda bi, i: (bi, i, 0)),
            const((1, d)),
            const(wk.shape), const(wt.shape), const(wft.shape),
            const((FOX_HEADS, 1)),
            const((tm, tm)),
            const(pk.shape), const(pqt.shape), const(gk.shape), const(gq.shape),
        ],
        out_specs=[
            pl.BlockSpec((1, tm, 2 * width), lambda bi, i: (bi, i, 0)),
            pl.BlockSpec((1, width, tm), lambda bi, i: (bi, 0, i)),
            pl.BlockSpec((1, tm // ch, width, ch), lambda bi, i: (bi, i, 0, 0)),
            pl.BlockSpec((1, width, tm), lambda bi, i: (bi, 0, i)),
            pl.BlockSpec((1, FOX_HEADS * AUG_SLOTS_PER_HEAD, tm), lambda bi, i: (bi, 0, i)),
            pl.BlockSpec((1, 1, 8, LANES), lambda bi, i: (bi, i, 0, 0)),
            pl.BlockSpec((1, 1, FOX_HEADS, LANES), lambda bi, i: (bi, i, 0, 0)),
        ],
        out_shape=[
            jax.ShapeDtypeStruct((b, s, 2 * width), BF16),
            jax.ShapeDtypeStruct((b, width, s), BF16),
            jax.ShapeDtypeStruct((b, s // ch, width, ch), BF16),
            jax.ShapeDtypeStruct((b, width, s), BF16),
            jax.ShapeDtypeStruct((b, FOX_HEADS * AUG_SLOTS_PER_HEAD, s), BF16),
            jax.ShapeDtypeStruct((b, s // tm, 8, LANES), F32),
            jax.ShapeDtypeStruct((b, s // tm, FOX_HEADS, LANES), F32),
        ],
        scratch_shapes=[pltpu.VMEM((FOX_HEADS, LANES), F32)],
        compiler_params=_cparams(2),
        name="fox_in",
    )(x, norm_g.reshape(1, d), wk.astype(BF16), wt, wft, bt, triu, pk, pqt, gk, gq)


def _fox_attn_kernel(kstart_ref, kaug_ref, vt_ref, qt_ref, augq_ref, ot_ref,
                     s_scr, p_scr, *, tq, tk, pairs_per_step):
    qi = pl.program_id(2)
    hd = FOX_HEAD_DIM
    zeros_hd = jnp.zeros((hd, tq), BF16)
    zeros_tail = jnp.zeros((LANES - AUG_SLOTS_PER_HEAD, tq), BF16)
    ones_block = lambda n: jnp.where(lax.broadcasted_iota(jnp.int32, (16, n), 0) == 0, 1.0, 0.0).astype(BF16)
    ones_rows = ones_block(tk)
    n_blocks = (qi * tq) // tk + 1
    k_first = kstart_ref[pl.program_id(0), pl.program_id(1), qi]

    heads = [(u, h) for u in range(pairs_per_step) for h in range(2)]
    w_head = {}
    for u, h in heads:
        q_rows = qt_ref[0, LANES * u + hd * h:LANES * u + hd * (h + 1), :]
        aug_rows = augq_ref[0, 2 * AUG_SLOTS_PER_HEAD * u + AUG_SLOTS_PER_HEAD * h:
                            2 * AUG_SLOTS_PER_HEAD * u + AUG_SLOTS_PER_HEAD * (h + 1), :]
        q_part = [q_rows, zeros_hd] if h == 0 else [zeros_hd, q_rows]
        w_head[u, h] = jnp.concatenate(q_part + [aug_rows, zeros_tail], axis=0)
    chains = [(u, h, sub) for u, h in heads for sub in range(tq // QUERY_LANES)]

    def issue_scores(kj, slot):
        ks = pl.multiple_of(kj * tk, tk)
        block_max = []
        for c, (u, h, sub) in enumerate(chains):
            kb = kaug_ref[0, pl.ds(ks, tk), 2 * LANES * u:2 * LANES * (u + 1)]
            s = _dot(kb, w_head[u, h][:, QUERY_LANES * sub:QUERY_LANES * (sub + 1)])
            s_scr[slot, c] = s
            block_max.append(jnp.max(s, axis=0, keepdims=True))
        return tuple(block_max)

    tri_key = lax.broadcasted_iota(jnp.int32, (QUERY_LANES, QUERY_LANES), 0)
    tri_qry = lax.broadcasted_iota(jnp.int32, (QUERY_LANES, QUERY_LANES), 1)
    causal = jnp.where(tri_key <= tri_qry, 0.0, MASK_VALUE)

    def pv_block(kj, c):
        u, h, _ = chains[c]
        v_rows = vt_ref[0, kj, LANES * u + hd * h:LANES * u + hd * (h + 1), :]
        return _dot(jnp.concatenate([v_rows, ones_rows], axis=0), p_scr[c])

    def step(kj, slot, state, has_prev=True):
        carry, block_max = state
        next_max = issue_scores(kj + 1, 1 - slot)
        out = []
        for c in range(len(chains)):
            m_old, alpha_prev, acc = carry[c]
            if has_prev:
                acc = acc * alpha_prev + pv_block(kj - 1, c)
            m_new = jnp.maximum(m_old, block_max[c])
            p_scr[c] = jnp.exp2(s_scr[slot, c] - m_new).astype(BF16)
            out.append((m_new, jnp.exp2(m_old - m_new), acc))
        return tuple(out), next_max

    def diagonal(kj, slot, state):
        carry, _ = state
        for c, (u, h, sub) in enumerate(chains):
            m_old, alpha_prev, acc = carry[c]
            acc = acc * alpha_prev + pv_block(jnp.maximum(kj - 1, 0), c)
            lo = QUERY_LANES * sub
            vis = lo + QUERY_LANES
            mid = s_scr[slot, c, lo:vis, :] + causal
            m_new = jnp.maximum(m_old, jnp.max(mid, axis=0, keepdims=True))
            if lo:
                top = s_scr[slot, c, 0:lo, :]
                m_new = jnp.maximum(m_new, jnp.max(top, axis=0, keepdims=True))
                p = jnp.concatenate([jnp.exp2(top - m_new).astype(BF16),
                                     jnp.exp2(mid - m_new).astype(BF16)], axis=0)
            else:
                p = jnp.exp2(mid - m_new).astype(BF16)
            v_rows = vt_ref[0, kj, LANES * u + hd * h:LANES * u + hd * (h + 1), 0:vis]
            pv = _dot(jnp.concatenate([v_rows, ones_block(vis)], axis=0), p)
            acc = acc * jnp.exp2(m_old - m_new) + pv
            ot_ref[0, LANES * u + hd * h:LANES * u + hd * (h + 1),
                   QUERY_LANES * sub:QUERY_LANES * (sub + 1)] = (acc[0:hd] / acc[hd:hd + 1]).astype(BF16)

    row = lambda value: jnp.full((1, QUERY_LANES), value, F32)
    carry = tuple((row(MASK_VALUE), row(1.0), jnp.zeros((hd + 16, QUERY_LANES), F32)) for _ in chains)
    state = (carry, tuple(row(0.0) for _ in chains))
    n_unmasked = n_blocks - 1 - k_first
    has_unmasked = jnp.minimum(n_unmasked, 1)

    def first(_, st):
        return step(k_first, 0, (st[0], issue_scores(k_first, 0)), has_prev=False)

    def only_diagonal(_, st):
        p_scr[...] = jnp.zeros_like(p_scr)
        return st[0], issue_scores(k_first, 0)

    def pair(i, st):
        kj = k_first + 1 + 2 * i
        return step(kj + 1, 0, step(kj, 1, st))

    state = lax.fori_loop(0, has_unmasked, first, state)
    state = lax.fori_loop(0, 1 - has_unmasked, only_diagonal, state)
    n_rest = jnp.maximum(n_unmasked - 1, 0)
    state = lax.fori_loop(0, n_rest // 2, pair, state)
    state = lax.fori_loop(0, lax.rem(n_rest, 2), lambda _, st: step(n_blocks - 2, 1, st), state)
    diagonal(n_blocks - 1, lax.rem(n_unmasked, 2), state)


def _fox_prune_table(kst, qst, heads_per_step):
    kn = jnp.sqrt(kst[:, :, 0, :FOX_HEADS]) * NORM_SLACK
    qn = jnp.sqrt(qst[:, :, :, 0]) * NORM_SLACK
    c_first = kst[:, :, 1, :FOX_HEADS] * LOG2E
    c_last = kst[:, :, 2, :FOX_HEADS] * LOG2E
    bound = (qn[:, :, None, :] * kn[:, None, :, :] + c_first[:, :, None, :] - c_last[:, None, :, :]
             + (qn * kn)[:, :, None, :])
    n = kst.shape[1]
    earlier = jnp.arange(n)[None, :, None] < jnp.arange(n)[:, None, None]
    skip = (bound < -PRUNE_LOG2) & earlier[None]
    block = jnp.arange(n, dtype=jnp.int32)[None, None, :, None]
    lead = jnp.min(jnp.where(skip, n, block), axis=2)
    lead = jnp.min(lead.reshape(lead.shape[0], n, FOX_HEADS // heads_per_step, heads_per_step), axis=3)
    return jnp.transpose(lead, (0, 2, 1)).astype(jnp.int32)


def _fox_attn(kaug, qt, vt, augq, kstart, *, tq, tk, pairs_per_step):
    b, s, _ = kaug.shape
    width = FOX_HEADS * FOX_HEAD_DIM
    nk = s // tk
    pps = pairs_per_step
    n_chains = 2 * pps * (tq // QUERY_LANES)
    grid_spec = pltpu.PrefetchScalarGridSpec(
        num_scalar_prefetch=1,
        grid=(b, FOX_PAIRS // pps, s // tq),
        in_specs=[
            pl.BlockSpec((1, s, 2 * LANES * pps), lambda bi, p, qi, ks: (bi, 0, p)),
            pl.BlockSpec((1, nk, LANES * pps, tk), lambda bi, p, qi, ks: (bi, 0, p, 0)),
            pl.BlockSpec((1, LANES * pps, tq), lambda bi, p, qi, ks: (bi, p, qi)),
            pl.BlockSpec((1, 2 * AUG_SLOTS_PER_HEAD * pps, tq), lambda bi, p, qi, ks: (bi, p, qi)),
        ],
        out_specs=pl.BlockSpec((1, LANES * pps, tq), lambda bi, p, qi, ks: (bi, p, qi)),
        scratch_shapes=[pltpu.VMEM((2, n_chains, tk, QUERY_LANES), F32),
                        pltpu.VMEM((n_chains, tk, QUERY_LANES), BF16)],
    )
    return pl.pallas_call(
        functools.partial(_fox_attn_kernel, tq=tq, tk=tk, pairs_per_step=pps),
        grid_spec=grid_spec,
        out_shape=jax.ShapeDtypeStruct((b, width, s), BF16),
        compiler_params=_cparams(3),
        name="fox_attn",
    )(kstart, kaug, vt, qt, augq)


def _ple(h1, p, wup_ref, wgate_ref):
    gate = jax.nn.sigmoid(_dot(h1.astype(BF16), wgate_ref[...]))
    up = _dot(p.astype(BF16), wup_ref[...])
    return h1 + up * gate


def _fox_out_kernel(ot_ref, zt_ref, x_ref, p_ref, wout_ref, wup_ref, wgate_ref, h_ref):
    z = zt_ref[0].astype(F32)
    gt = (ot_ref[0].astype(F32) * (z * jax.nn.sigmoid(z))).astype(BF16)
    y = lax.dot_general(gt, wout_ref[...], TN_DIMS, preferred_element_type=F32)
    h_ref[0] = _ple(x_ref[0] + y, p_ref[0, 0], wup_ref, wgate_ref)


def _fox_out(ot, zt, x, p, layer, w_out, w_up, w_gate, *, tm):
    b, s, d = x.shape
    width = ot.shape[1]
    pd = p.shape[-1]
    const = lambda shape: pl.BlockSpec(shape, lambda bi, i: (0,) * len(shape))
    return pl.pallas_call(
        _fox_out_kernel,
        grid=(b, s // tm),
        in_specs=[
            pl.BlockSpec((1, width, tm), lambda bi, i: (bi, 0, i)),
            pl.BlockSpec((1, width, tm), lambda bi, i: (bi, 0, i)),
            pl.BlockSpec((1, tm, d), lambda bi, i: (bi, i, 0)),
            pl.BlockSpec((1, 1, tm, pd), lambda bi, i: (layer, bi, i, 0)),
            const((width, d)), const((pd, d)), const((d, d)),
        ],
        out_specs=pl.BlockSpec((1, tm, d), lambda bi, i: (bi, i, 0)),
        out_shape=jax.ShapeDtypeStruct((b, s, d), F32),
        compiler_params=_cparams(2, SMALL_VMEM_LIMIT_BYTES),
        name="fox_out",
    )(ot, zt, x, p, w_out.astype(BF16), w_up.astype(BF16), w_gate.astype(BF16))


PERM_ROWS = 256
DIL_SUB_BLOCKS = 8


def _phase_major_matrix(dilation):
    n = PERM_ROWS // dilation
    p = np.zeros((PERM_ROWS, PERM_ROWS), np.float32)
    for r in range(dilation):
        for i in range(n):
            p[r * n + i, dilation * i + r] = 1.0
    return p


def _dil_in_kernel(h_ref, g_ref, *rest, dilations, parts, tm):
    n_w = sum(parts)
    n_perm = sum(d > 1 for d in dilations)
    w_refs, perm_refs, out_refs = rest[:n_w], list(rest[n_w:n_w + n_perm]), rest[n_w + n_perm:]
    hn = _rms_norm(h_ref[0], g_ref[...]).astype(BF16)
    first = 0
    for dil, n_parts, o_ref in zip(dilations, parts, out_refs):
        group_w = w_refs[first:first + n_parts]
        first += n_parts
        pc = group_w[0].shape[1]
        if dil == 1:
            for part, w_ref in enumerate(group_w):
                o_ref[0, :, pc * part:pc * (part + 1)] = _dot(hn, w_ref[...]).astype(BF16)
            continue
        perm = perm_refs.pop(0)[...]
        n = PERM_ROWS // dil
        for c in range(tm // PERM_ROWS):
            rows = slice(PERM_ROWS * c, PERM_ROWS * (c + 1))
            hp = _dot(perm, hn[rows]).astype(BF16)
            for part, w_ref in enumerate(group_w):
                res = _dot(hp, w_ref[...]).astype(BF16)
                for r in range(dil):
                    dst = pc * (part * dil + r)
                    o_ref[0, n * c:n * (c + 1), dst:dst + pc] = res[n * r:n * (r + 1)]


def _dil_in(h, norm_g, w, groups, dilations, *, tm, part_cols):
    b, s, d = h.shape
    parts = tuple(len(blocks) for blocks in groups)
    perms = [jnp.asarray(_phase_major_matrix(dil), BF16) for dil in dilations if dil > 1]
    const = lambda shape: pl.BlockSpec(shape, lambda bi, i: (0,) * len(shape))
    w_specs = [pl.BlockSpec((d, part_cols), lambda bi, i, cb=cb: (0, cb),
                            pipeline_mode=pl.Buffered(1))
               for blocks in groups for cb in blocks]
    widths = [dil * n_parts * part_cols for dil, n_parts in zip(dilations, parts)]
    return pl.pallas_call(
        functools.partial(_dil_in_kernel, dilations=dilations, parts=parts, tm=tm),
        grid=(b, s // tm),
        in_specs=[pl.BlockSpec((1, tm, d), lambda bi, i: (bi, i, 0)), const((1, d))]
                 + w_specs + [const(p.shape) for p in perms],
        out_specs=[pl.BlockSpec((1, tm // dil, width), lambda bi, i: (bi, i, 0))
                   for dil, width in zip(dilations, widths)],
        out_shape=[jax.ShapeDtypeStruct((b, s // dil, width), BF16)
                   for dil, width in zip(dilations, widths)],
        compiler_params=_cparams(2),
        name="dil_in_" + "_".join(str(dil) for dil in dilations),
    )(h, norm_g.reshape(1, d), *([w] * sum(parts)), *perms)


def _dil_attn_kernel(q_ref, kc_ref, vc_ref, kp_ref, vp_ref, o_ref, st_ref,
                     kcat, vcat, bias_scr, *, dilation, kb, phases, neg_slopes):
    n = pl.program_id(2)
    nw = DIL_WINDOW_STEPS
    hd = DIL_HEAD_DIM
    gw = DIL_HEADS_PER_GROUP * hd

    @pl.when((pl.program_id(0) == 0) & (pl.program_id(1) == 0) & (n == 0))
    def _():
        row = lax.broadcasted_iota(jnp.int32, (nw, 2 * nw), 0)
        col = lax.broadcasted_iota(jnp.int32, (nw, 2 * nw), 1)
        dist = nw + row - col
        band = (dist >= 0) & (dist <= nw)
        dist_f = (dist * dilation).astype(F32)
        for h in range(DIL_HEADS_PER_GROUP):
            bias_scr[h] = jnp.where(band, neg_slopes[h] * dist_f * LOG2E, MASK_VALUE)

    for ph in range(phases):
        cols = slice(gw * ph, gw * (ph + 1))
        kcat[ph, 0:nw] = kp_ref[0, :, cols]
        kcat[ph, nw:nw * (kb + 1)] = kc_ref[0, :, cols]
        vcat[ph, 0:nw] = vp_ref[0, :, cols]
        vcat[ph, nw:nw * (kb + 1)] = vc_ref[0, :, cols]
    lane = lax.broadcasted_iota(jnp.int32, (nw, LANES), 1)
    col = lax.broadcasted_iota(jnp.int32, (nw, 2 * nw), 1)
    no_history = jnp.where(col < jnp.where(n == 0, nw, 0), MASK_VALUE, 0.0)

    heads = [slice(hd * h, hd * (h + 1)) for h in range(DIL_HEADS_PER_GROUP)]
    problems = [(ph, jb) for ph in range(phases) for jb in range(kb)]
    scores = [[lax.dot_general(q_ref[0, nw * jb:nw * (jb + 1), gw * ph + hd * h:gw * ph + hd * (h + 1)],
                               kcat[ph, nw * jb:nw * (jb + 2), hs], NT_DIMS,
                               preferred_element_type=F32) for h, hs in enumerate(heads)]
              for ph, jb in problems]
    for i, (ph, jb) in enumerate(problems):
        rows = slice(nw * jb, nw * (jb + 1))
        stat_tile = jnp.zeros((nw, LANES), F32)
        for h, hs in enumerate(heads):
            t = scores[i][h] + bias_scr[h]
            if jb == 0:
                t = t + no_history
            m = jnp.max(t, axis=1, keepdims=True)
            p = jnp.exp2(t - m)
            l = jnp.sum(p, axis=1, keepdims=True)
            o_ref[0, rows, gw * ph + hd * h:gw * ph + hd * (h + 1)] = _dot(
                p.astype(BF16), vcat[ph, nw * jb:nw * (jb + 2), hs]).astype(BF16)
            stat_tile = jnp.where(lane == h, m, stat_tile)
            stat_tile = jnp.where(lane == h + DIL_HEADS_PER_GROUP, l, stat_tile)
        st_ref[0, rows, LANES * ph:LANES * (ph + 1)] = stat_tile


def _alibi_neg_slopes(group):
    n = len(DIL_PATTERN) * DIL_HEADS_PER_GROUP
    k = np.arange(1, n + 1, dtype=np.float32)
    slopes = np.float32(2.0) ** (np.float32(-ALIBI_MAX_EXP) * k / np.float32(n))
    lo = group * DIL_HEADS_PER_GROUP
    return tuple(float(-v) for v in slopes[lo:lo + DIL_HEADS_PER_GROUP])


def _dil_attn(view, group, *, kb, phases):
    _, dilation = DIL_PATTERN[group]
    b, length, _ = view.shape
    nw = DIL_WINDOW_STEPS
    gw = DIL_HEADS_PER_GROUP * DIL_HEAD_DIM
    rows = nw * kb
    steps = dilation // phases

    def cur(part):
        return pl.BlockSpec((1, rows, phases * gw), lambda bi, r, n: (bi, n, part * steps + r))

    def prev(part):
        return pl.BlockSpec((1, nw, phases * gw),
                            lambda bi, r, n: (bi, jnp.maximum(n * kb - 1, 0), part * steps + r))

    o, st = pl.pallas_call(
        functools.partial(_dil_attn_kernel, dilation=dilation, kb=kb, phases=phases,
                          neg_slopes=_alibi_neg_slopes(group)),
        grid=(b, steps, length // rows),
        in_specs=[cur(0), cur(1), cur(2), prev(1), prev(2)],
        out_specs=[
            pl.BlockSpec((1, rows, phases * gw), lambda bi, r, n: (bi, n, r)),
            pl.BlockSpec((1, rows, phases * LANES), lambda bi, r, n: (bi, n, r)),
        ],
        out_shape=[
            jax.ShapeDtypeStruct((b, length, dilation * gw), BF16),
            jax.ShapeDtypeStruct((b, length, dilation * LANES), F32),
        ],
        scratch_shapes=[pltpu.VMEM((phases, nw * (kb + 1), gw), BF16),
                        pltpu.VMEM((phases, nw * (kb + 1), gw), BF16),
                        pltpu.VMEM((DIL_HEADS_PER_GROUP, nw, 2 * nw), F32)],
        compiler_params=_cparams(3, SMALL_VMEM_LIMIT_BYTES),
        name=f"dil_attn_g{group}",
    )(view, view, view, view, view)
    return o, st


def _natural_rows(blk, perm_t, dilation, tm):
    cols = blk.shape[1] // dilation
    if dilation == 1:
        return blk
    n = PERM_ROWS // dilation
    chunks = []
    for c in range(tm // PERM_ROWS):
        phase_major = jnp.concatenate(
            [blk[n * c:n * (c + 1), cols * r:cols * (r + 1)] for r in range(dilation)], axis=0)
        chunks.append(_dot(perm_t, phase_major))
    return jnp.concatenate(chunks, axis=0)


def _dil_out_kernel(o0_ref, o1_ref, o2_ref, s0_ref, s1_ref, s2_ref, z_ref, h_ref, p_ref,
                    pt1_ref, pt2_ref, e_ref, wout_ref, wup_ref, wgate_ref, g_ref,
                    out_ref, *, dilations, tm):
    perm_ts = (None, pt1_ref[...], pt2_ref[...])
    n_heads = DIL_HEADS_PER_GROUP
    row_max, row_sum = [], []
    for st_ref, perm_t, dil in zip((s0_ref, s1_ref, s2_ref), perm_ts, dilations):
        if dil == 1:
            st = st_ref[0]
        else:
            hi, mid, lo = (_natural_rows(piece.astype(BF16), perm_t, dil, tm)
                           for piece in _split3(st_ref[0]))
            st = hi + mid + lo
        row_max.append(st)
        row_sum.append(pltpu.roll(st, LANES - n_heads, 1))
    m = jnp.maximum(jnp.maximum(row_max[0], row_max[1]), row_max[2])
    e = [jnp.exp2(v - m) for v in row_max]
    inv = 1.0 / (e[0] * row_sum[0] + e[1] * row_sum[1] + e[2] * row_sum[2])
    live = lax.broadcasted_iota(jnp.int32, (tm, LANES), 1) < n_heads
    o = None
    for ev, o_ref, perm_t, dil in zip(e, (o0_ref, o1_ref, o2_ref), perm_ts, dilations):
        w = jnp.where(live, ev * inv, 0.0)
        w_hi = w.astype(BF16)
        w_lo = (w - w_hi.astype(F32)).astype(BF16)
        w_full = _dot(jnp.concatenate([w_hi, w_lo], axis=1), e_ref[...])
        term = w_full * _natural_rows(o_ref[0], perm_t, dil, tm)
        o = term if o is None else o + term
    z = z_ref[0].astype(F32)
    g = (o * (z * jax.nn.sigmoid(z))).astype(BF16)
    h1 = h_ref[0] + _dot(g, wout_ref[...])
    out_ref[0] = _rms_norm(_ple(h1, p_ref[0, 0], wup_ref, wgate_ref), g_ref[...])


def _dil_out(outs, stats, proj0, h, p, layer, w_out, w_up, w_gate, final_g, *, tm):
    b, s, d = h.shape
    gw = DIL_HEADS_PER_GROUP * DIL_HEAD_DIM
    pd = p.shape[-1]
    dilations = tuple(dil for _, dil in DIL_PATTERN)
    z_block = proj0.shape[-1] // gw - 1
    expand = np.zeros((2 * LANES, gw), np.float32)
    for hh in range(DIL_HEADS_PER_GROUP):
        expand[hh, DIL_HEAD_DIM * hh:DIL_HEAD_DIM * (hh + 1)] = 1.0
        expand[LANES + hh, DIL_HEAD_DIM * hh:DIL_HEAD_DIM * (hh + 1)] = 1.0
    perm_ts = [jnp.asarray(_phase_major_matrix(dil).T, BF16) for dil in dilations[1:]]
    const = lambda shape: pl.BlockSpec(shape, lambda bi, i: (0,) * len(shape))
    row = lambda w: pl.BlockSpec((1, tm, w), lambda bi, i: (bi, i, 0))
    view = lambda w: [pl.BlockSpec((1, tm // dil, dil * w), lambda bi, i: (bi, i, 0))
                      for dil in dilations]
    return pl.pallas_call(
        functools.partial(_dil_out_kernel, dilations=dilations, tm=tm),
        grid=(b, s // tm),
        in_specs=view(gw) + view(LANES) + [
            pl.BlockSpec((1, tm, gw), lambda bi, i: (bi, i, z_block)),
            row(d), pl.BlockSpec((1, 1, tm, pd), lambda bi, i: (layer, bi, i, 0)),
            const((PERM_ROWS, PERM_ROWS)), const((PERM_ROWS, PERM_ROWS)),
            const((2 * LANES, gw)), const((gw, d)), const((pd, d)), const((d, d)), const((1, d)),
        ],
        out_specs=row(d),
        out_shape=jax.ShapeDtypeStruct((b, s, d), F32),
        compiler_params=_cparams(2),
        name="dil_out",
    )(*outs, *stats, proj0, h, p, *perm_ts, jnp.asarray(expand, BF16),
      w_out.astype(BF16), w_up.astype(BF16), w_gate.astype(BF16), final_g.reshape(1, d))


def kernel(x, p, fox_norm, fox_w_in, fox_b_f, fox_w_out, dil_norm, dil_w_in, dil_w_out,
           ple_w_up, ple_w_gate, final_norm):
    s = x.shape[1]
    tm = min(512, s)
    tq = min(512, s)
    tk = min(512, s)
    assert tm == tq == tk, "the pruning table pairs fox_in tiles with query tiles and key blocks"
    kaug, qt, vt, zt, augq, kst, qst = _fox_in(x, fox_norm[0], fox_w_in[0], fox_b_f[0], tm=tm, ch=tk)
    pps = 2
    ot = _fox_attn(kaug, qt, vt, augq, _fox_prune_table(kst, qst, 2 * pps), tq=tq, tk=tk,
                   pairs_per_step=pps)
    h = _fox_out(ot, zt, x, p, 0, fox_w_out[0], ple_w_up[0], ple_w_gate[0], tm=min(2 * tm, s))

    gw = DIL_HEADS_PER_GROUP * DIL_HEAD_DIM
    n_groups = len(DIL_PATTERN)
    k_scale = DIL_HEAD_DIM ** -0.5 * LOG2E
    col = jnp.arange(dil_w_in.shape[-1]) // (n_groups * gw)
    w1 = (dil_w_in[0] * jnp.where(col == 1, k_scale, 1.0)[None, :]).astype(BF16)
    qkv = lambda g: [n_groups * part + g for part in range(3)]
    views = list(_dil_in(h, dil_norm[0], w1, [qkv(0) + [3 * n_groups], qkv(1), qkv(2)],
                         tuple(dil for _, dil in DIL_PATTERN), tm=tm, part_cols=gw))
    proj0 = views[0]
    outs, stats = [], []
    for group, (_, dilation) in enumerate(DIL_PATTERN):
        kb = min(DIL_SUB_BLOCKS, s // dilation // DIL_WINDOW_STEPS)
        phases = min(dilation, DIL_SUB_BLOCKS // kb)
        o, st = _dil_attn(views[group], group, kb=kb, phases=phases)
        outs.append(o)
        stats.append(st)
    return _dil_out(outs, stats, proj0, h, p, 1, dil_w_out[0], ple_w_up[1], ple_w_gate[1],
                    final_norm, tm=tm)
```
